```python
import jax, jax.numpy as jnp
from jax import lax
import numpy as np

D_MODEL = 1024
BATCH = 4
SEQ = 8192
DEPTH = 1
DEC_BATCH = 128
DEC_SEQ = 8
PAST_LEN = 16384
PAGE_SIZE = 128

HEAD_DIM = 64
MIX_WIDTH = D_MODEL
RET_HEADS = MIX_WIDTH // 2 // HEAD_DIM
SWA_HEADS = MIX_WIDTH // 2 // HEAD_DIM
SWA_KV_HEADS = 2
SWA_GROUP = SWA_HEADS // SWA_KV_HEADS
WINDOW = 128
SWA_BLOCK = WINDOW
RET_CHUNK = 128
D_FF = ((8 * D_MODEL // 3 + 127) // 128) * 128
RET_W = RET_HEADS * HEAD_DIM
SWA_W = SWA_HEADS * HEAD_DIM
KV_W = SWA_KV_HEADS * HEAD_DIM
IN_COLS = 4 * RET_W + SWA_W + 2 * KV_W
LN_EPS = 1e-5
GN_EPS = 1e-5
DN_ALPHA = (2.0 * DEPTH) ** 0.25
DN_BETA = (8.0 * DEPTH) ** -0.25

kernel_name = "hymba_retention_swa_sink_macaron_deepnorm_step"


def _layer_norm(x, g, b):
    xf = x.astype(jnp.float32)
    mu = jnp.mean(xf, axis=-1, keepdims=True)
    var = jnp.mean(jnp.square(xf - mu), axis=-1, keepdims=True)
    return ((xf - mu) * lax.rsqrt(var + LN_EPS) * g.astype(jnp.float32) + b.astype(jnp.float32)).astype(x.dtype)


def _swiglu(x, w_gate, w_up, w_down):
    return (jax.nn.silu(x @ w_gate) * (x @ w_up)) @ w_down


def _project_in(h, w_in):
    B, S = h.shape[:2]
    proj = h @ w_in
    cuts = [RET_W, 2 * RET_W, 3 * RET_W, 4 * RET_W, 4 * RET_W + SWA_W, 4 * RET_W + SWA_W + KV_W]
    rq, rk, rv, rg, sq, sk, sv = jnp.split(proj, cuts, axis=-1)
    heads = lambda t, n: t.reshape(B, S, n, HEAD_DIM)
    return (heads(rq, RET_HEADS), heads(rk, RET_HEADS), heads(rv, RET_HEADS), rg,
            heads(sq, SWA_HEADS), heads(sk, SWA_KV_HEADS), heads(sv, SWA_KV_HEADS))


def _retention(q, k, v, s0, chunk):
    B, S = q.shape[:2]
    nc = S // chunk
    shp = (B, nc, chunk, RET_HEADS, HEAD_DIM)
    qf = q.astype(jnp.float32).reshape(shp)
    kf = (k.astype(jnp.float32) * HEAD_DIM ** -0.5).reshape(shp)
    vf = v.astype(jnp.float32).reshape(shp)
    log_g = jnp.log1p(-jnp.exp2(-5.0 - jnp.arange(RET_HEADS, dtype=jnp.float32)))
    idx = jnp.arange(chunk, dtype=jnp.float32)
    diff = idx[:, None] - idx[None, :]
    causal = diff >= 0
    dmask = jnp.where(causal[None], jnp.exp(jnp.where(causal, diff, 0.0)[None] * log_g[:, None, None]), 0.0)
    scores = jnp.einsum('bnihd,bnjhd->bnhij', qf, kf) * dmask
    intra = jnp.einsum('bnhij,bnjhe->bnihe', scores, vf)
    k_w = jnp.exp((chunk - 1.0 - idx)[:, None] * log_g[None, :])
    kv = jnp.einsum('bnjhd,bnjhe->nbhde', kf * k_w[:, :, None], vf)
    g_c = jnp.exp(chunk * log_g)[None, :, None, None]

    def step(s, kv_n):
        return g_c * s + kv_n, s

    s_final, s_start = lax.scan(step, s0, kv)
    q_w = jnp.exp((idx + 1.0)[:, None] * log_g[None, :])
    inter = jnp.einsum('bnihd,nbhde->bnihe', qf * q_w[:, :, None], s_start)
    return (intra + inter).reshape(B, S, RET_HEADS, HEAD_DIM), s_final


def _head_group_norm(o, gain):
    mu = jnp.mean(o, axis=-1, keepdims=True)
    var = jnp.mean(jnp.square(o - mu), axis=-1, keepdims=True)
    y = (o - mu) * lax.rsqrt(var + GN_EPS)
    return y.reshape(o.shape[:2] + (RET_W,)) * gain.astype(jnp.float32)


def _sink_attention(q, k, v, dist, valid, sinks):
    qg = q.astype(jnp.float32).reshape(q.shape[:-2] + (SWA_KV_HEADS, SWA_GROUP, HEAD_DIM))
    s = jnp.einsum('...qngd,...knd->...ngqk', qg, k.astype(jnp.float32)) * HEAD_DIM ** -0.5
    slopes = jnp.exp2(-8.0 / SWA_HEADS * jnp.arange(1, SWA_HEADS + 1, dtype=jnp.float32))
    s = s - slopes.reshape(SWA_KV_HEADS, SWA_GROUP, 1, 1) * dist
    s = jnp.where(valid, s, -jnp.inf)
    sink = sinks.astype(jnp.float32).reshape(SWA_KV_HEADS, SWA_GROUP, 1)
    m = jnp.maximum(jnp.max(s, axis=-1), sink)
    p = jnp.exp(s - m[..., None])
    denom = jnp.sum(p, axis=-1) + jnp.exp(sink - m)
    o = jnp.einsum('...ngqk,...knd->...qngd', p / denom[..., None], v.astype(jnp.float32))
    return o.reshape(q.shape[:-2] + (SWA_HEADS, HEAD_DIM))


def _swa_prompt(q, k, v, sinks):
    B, S = q.shape[:2]
    nb = S // SWA_BLOCK
    qb = q.reshape(B, nb, SWA_BLOCK, SWA_HEADS, HEAD_DIM)
    kb = k.reshape(B, nb, SWA_BLOCK, SWA_KV_HEADS, HEAD_DIM)
    vb = v.reshape(B, nb, SWA_BLOCK, SWA_KV_HEADS, HEAD_DIM)
    shift = lambda t: jnp.concatenate([jnp.zeros_like(t[:, :1]), t[:, :-1]], axis=1)
    kk = jnp.concatenate([shift(kb), kb], axis=2)
    vv = jnp.concatenate([shift(vb), vb], axis=2)
    i = jnp.arange(SWA_BLOCK)
    j = jnp.arange(2 * SWA_BLOCK)
    dist = i[:, None] + SWA_BLOCK - j[None, :]
    within = (dist >= 0) & (dist <= WINDOW)
    has_prev = jnp.arange(nb) > 0
    valid = within[None] & (has_prev[:, None, None] | (j >= SWA_BLOCK)[None, None, :])
    o = _sink_attention(qb, kk, vv, dist.astype(jnp.float32), valid[:, None, None], sinks)
    return o.reshape(B, S, SWA_HEADS, HEAD_DIM)


def _swa_sample(q, k, v, k_buf, v_buf, sinks):
    T = q.shape[1]
    kk = jnp.concatenate([k_buf.astype(k.dtype), k], axis=1)
    vv = jnp.concatenate([v_buf.astype(v.dtype), v], axis=1)
    i = jnp.arange(T)
    j = jnp.arange(WINDOW + T)
    dist = i[:, None] + WINDOW - j[None, :]
    valid = (dist >= 0) & (dist <= WINDOW)
    o = _sink_attention(q, kk, vv, dist.astype(jnp.float32), valid, sinks)
    return o, kk[:, T:], vv[:, T:]


def _mixer(h, w_in, w_out, gn_w, sinks, ret_state=None, k_buf=None, v_buf=None):
    B, S = h.shape[:2]
    rq, rk, rv, rg, sq, sk, sv = _project_in(h, w_in)
    if ret_state is None:
        s0 = jnp.zeros((B, RET_HEADS, HEAD_DIM, HEAD_DIM), jnp.float32)
        o_ret, s_new = _retention(rq, rk, rv, s0, RET_CHUNK)
        o_swa = _swa_prompt(sq, sk, sv, sinks)
        new_k, new_v = sk[:, -WINDOW:], sv[:, -WINDOW:]
    else:
        o_ret, s_new = _retention(rq, rk, rv, ret_state.astype(jnp.float32), S)
        o_swa, new_k, new_v = _swa_sample(sq, sk, sv, k_buf, v_buf, sinks)
    y_ret = jax.nn.silu(rg.astype(jnp.float32)) * _head_group_norm(o_ret, gn_w)
    merged = jnp.concatenate([y_ret, o_swa.reshape(B, S, SWA_W)], axis=-1).astype(h.dtype)
    return merged @ w_out, s_new.astype(h.dtype), new_k, new_v


def _decoder_layer(x, ln_g, ln_b, w_in, gn_w, sinks, w_out, f1g, f1u, f1d, f2g, f2u, f2d,
                   ret_state=None, k_buf=None, v_buf=None):
    x = _layer_norm(DN_ALPHA * x + 0.5 * _swiglu(x, f1g, f1u, f1d), ln_g[0], ln_b[0])
    m, s_new, new_k, new_v = _mixer(x, w_in, w_out, gn_w, sinks, ret_state, k_buf, v_buf)
    x = _layer_norm(DN_ALPHA * x + m, ln_g[1], ln_b[1])
    x = _layer_norm(DN_ALPHA * x + 0.5 * _swiglu(x, f2g, f2u, f2d), ln_g[2], ln_b[2])
    return x, s_new, new_k, new_v


def setup_inputs(seed: int = 0) -> dict:
    key = jax.random.key(seed)
    ks = jax.random.split(key, 18)
    n = lambda i, shape: jax.random.normal(ks[i], shape, jnp.float32)
    return {
        "x_prompt": n(0, (BATCH, SEQ, D_MODEL)),
        "x_sample": n(1, (DEC_BATCH, DEC_SEQ, D_MODEL)),
        "state_ret": 0.5 * n(2, (DEPTH, DEC_BATCH, RET_HEADS, HEAD_DIM, HEAD_DIM)),
        "cache_swa_k": n(3, (DEPTH, DEC_BATCH, WINDOW, SWA_KV_HEADS, HEAD_DIM)),
        "cache_swa_v": n(4, (DEPTH, DEC_BATCH, WINDOW, SWA_KV_HEADS, HEAD_DIM)),
        "ln_gain": 1.0 + 0.02 * n(5, (DEPTH, 3, D_MODEL)),
        "ln_bias": 0.02 * n(6, (DEPTH, 3, D_MODEL)),
        "w_in": n(7, (DEPTH, D_MODEL, IN_COLS)) * D_MODEL ** -0.5,
        "ret_gn_w": 1.0 + 0.02 * n(8, (DEPTH, RET_W)),
        "swa_sinks": 0.5 * n(9, (DEPTH, SWA_HEADS)),
        "w_out": n(10, (DEPTH, MIX_WIDTH, D_MODEL)) * (MIX_WIDTH ** -0.5 * DN_BETA),
        "ffn1_gate": n(11, (DEPTH, D_MODEL, D_FF)) * D_MODEL ** -0.5,
        "ffn1_up": n(12, (DEPTH, D_MODEL, D_FF)) * D_MODEL ** -0.5,
        "ffn1_down": n(13, (DEPTH, D_FF, D_MODEL)) * (D_FF ** -0.5 * DN_BETA),
        "ffn2_gate": n(14, (DEPTH, D_MODEL, D_FF)) * D_MODEL ** -0.5,
        "ffn2_up": n(15, (DEPTH, D_MODEL, D_FF)) * D_MODEL ** -0.5,
        "ffn2_down": n(16, (DEPTH, D_FF, D_MODEL)) * (D_FF ** -0.5 * DN_BETA),
    }


def reference(x_prompt, x_sample, state_ret, cache_swa_k, cache_swa_v, ln_gain, ln_bias, w_in,
              ret_gn_w, swa_sinks, w_out, ffn1_gate, ffn1_up, ffn1_down, ffn2_gate, ffn2_up, ffn2_down):
    yp, ys = x_prompt, x_sample
    sp_list, kp_list, vp_list, ss_list, ks_list, vs_list = [], [], [], [], [], []
    for l in range(DEPTH):
        params = (ln_gain[l], ln_bias[l], w_in[l], ret_gn_w[l], swa_sinks[l], w_out[l],
                  ffn1_gate[l], ffn1_up[l], ffn1_down[l], ffn2_gate[l], ffn2_up[l], ffn2_down[l])
        yp, sp, kp, vp = _decoder_layer(yp, *params)
        ys, s_s, k_s, v_s = _decoder_layer(ys, *params, ret_state=state_ret[l],
                                           k_buf=cache_swa_k[l], v_buf=cache_swa_v[l])
        sp_list.append(sp); kp_list.append(kp); vp_list.append(vp)
        ss_list.append(s_s); ks_list.append(k_s); vs_list.append(v_s)
    state_ret_prompt = jnp.stack(sp_list, axis=0)
    cache_swa_k_prompt = jnp.stack(kp_list, axis=0)
    cache_swa_v_prompt = jnp.stack(vp_list, axis=0)
    state_ret_sample = jnp.stack(ss_list, axis=0)
    cache_swa_k_sample = jnp.stack(ks_list, axis=0)
    cache_swa_v_sample = jnp.stack(vs_list, axis=0)
    return (yp, ys, state_ret_prompt, cache_swa_k_prompt, cache_swa_v_prompt,
            state_ret_sample, cache_swa_k_sample, cache_swa_v_sample)
```

```python
import functools

import numpy as np
import jax
import jax.numpy as jnp
from jax import lax
from jax.experimental import pallas as pl
from jax.experimental.pallas import tpu as pltpu

F32 = jnp.float32
BF16 = jnp.bfloat16

D_MODEL = 1024
HEAD_DIM = 64
RET_HEADS = 8
SWA_HEADS = 8
SWA_KV_HEADS = 2
SWA_GROUP = SWA_HEADS // SWA_KV_HEADS
WINDOW = 128
CHUNK = 128
D_FF = 2816
RET_W = RET_HEADS * HEAD_DIM
SWA_W = SWA_HEADS * HEAD_DIM
KV_W = SWA_KV_HEADS * HEAD_DIM
IN_COLS = 4 * RET_W + SWA_W + 2 * KV_W
LN_EPS = 1e-5
GN_EPS = 1e-5
DEPTH = 1
DN_ALPHA = (2.0 * DEPTH) ** 0.25
QK_SCALE = HEAD_DIM ** -0.5
NEG_BIG = -1e30

LANES = 128
PAIRS = RET_HEADS // 2
VMEM_LIMIT_BYTES = 56 * 1024 * 1024

_RQ, _RK, _RV, _RG, _SQ = 0, RET_W, 2 * RET_W, 3 * RET_W, 4 * RET_W
_SK = 4 * RET_W + SWA_W
_SV = _SK + KV_W

_SWA_PAIR_HEADS = [(p, SWA_GROUP + p) for p in range(PAIRS)]
_SWA_HEAD_ORDER = [h for pair in _SWA_PAIR_HEADS for h in pair]

_FF_CHUNKS = ((0, 1536), (1536, 1280))


def _log_gamma():
    h = np.arange(RET_HEADS, dtype=np.float64)
    return np.log1p(-np.exp2(-5.0 - h))


def _alibi_slopes():
    return np.exp2(-8.0 / SWA_HEADS * np.arange(1, SWA_HEADS + 1, dtype=np.float64))


def _layer_norm(z, g, b):
    mu = jnp.mean(z, axis=-1, keepdims=True)
    d = z - mu
    var = jnp.mean(d * d, axis=-1, keepdims=True)
    return d * lax.rsqrt(var + LN_EPS) * g + b


def _dot(a, b):
    return jnp.dot(a, b, preferred_element_type=F32)


def _dot_nt(a, b):
    return lax.dot_general(a, b, (((1,), (1,)), ((), ())), preferred_element_type=F32)


def _dot_tn(a, b):
    return lax.dot_general(a, b, (((0,), (0,)), ((), ())), preferred_element_type=F32)


def _ffn_ln_body(x_ref, wg_ref, wu_ref, wd_ref, g_ref, b_ref, o_ref):
    x = x_ref[...]
    xb = x.astype(BF16)
    acc = None
    for c0, cw in _FF_CHUNKS:
        gate = _dot(xb, wg_ref[:, c0:c0 + cw])
        up = _dot(xb, wu_ref[:, c0:c0 + cw])
        hid = (gate * jax.nn.sigmoid(gate) * up).astype(BF16)
        part = _dot(hid, wd_ref[c0:c0 + cw, :])
        acc = part if acc is None else acc + part
    z = DN_ALPHA * x + 0.5 * acc
    o_ref[...] = _layer_norm(z, g_ref[...], b_ref[...])


def _const_spec(shape):
    nd = len(shape)
    return pl.BlockSpec(shape, lambda *_: (0,) * nd, pipeline_mode=pl.Buffered(1))


def _ffn_ln(x, wg, wu, wd, g, b, *, tm):
    rows = x.shape[0]
    assert rows % tm == 0
    return pl.pallas_call(
        _ffn_ln_body,
        grid=(rows // tm,),
        in_specs=[
            pl.BlockSpec((tm, D_MODEL), lambda i: (i, 0)),
            _const_spec((D_MODEL, D_FF)),
            _const_spec((D_MODEL, D_FF)),
            _const_spec((D_FF, D_MODEL)),
            _const_spec((1, D_MODEL)),
            _const_spec((1, D_MODEL)),
        ],
        out_specs=pl.BlockSpec((tm, D_MODEL), lambda i: (i, 0)),
        out_shape=jax.ShapeDtypeStruct((rows, D_MODEL), F32),
        compiler_params=pltpu.CompilerParams(
            dimension_semantics=("arbitrary",), vmem_limit_bytes=VMEM_LIMIT_BYTES),
        name="ffn_ln",
    )(x, wg, wu, wd, g, b)


def _prompt_constants():
    lg = _log_gamma()
    idx = np.arange(CHUNK, dtype=np.float64)
    diff = idx[:, None] - idx[None, :]
    dm = np.where(diff >= 0, np.exp(np.where(diff >= 0, diff, 0.0)[None] * lg[:, None, None]), 0.0)
    dm_st = np.stack([np.concatenate([dm[2 * p], dm[2 * p + 1]], axis=0) for p in range(PAIRS)])
    qw = np.repeat(np.exp((idx + 1.0)[:, None] * lg[None, :]), HEAD_DIM, axis=1)
    kw = np.repeat(np.exp((CHUNK - 1.0 - idx)[:, None] * lg[None, :]), HEAD_DIM, axis=1) * QK_SCALE
    gc = np.exp(CHUNK * lg)
    blk = np.kron(np.eye(2), np.ones((HEAD_DIM, HEAD_DIM)))
    gbd = np.stack([blk * np.repeat(gc[2 * p:2 * p + 2], HEAD_DIM)[:, None] for p in range(PAIRS)])
    slopes = _alibi_slopes()
    i = np.arange(CHUNK)
    j = np.arange(2 * CHUNK)
    dist = i[:, None] + CHUNK - j[None, :]
    within = (dist >= 0) & (dist <= WINDOW)
    valid = [within & (j >= CHUNK)[None, :], within]
    bias = np.empty((PAIRS, 2, 2 * CHUNK, 2 * CHUNK), np.float64)
    for p, heads in enumerate(_SWA_PAIR_HEADS):
        for v in range(2):
            bias[p, v] = np.concatenate(
                [np.where(valid[v], -slopes[h] * dist, NEG_BIG) for h in heads], axis=0)
    f = lambda a: jnp.asarray(a, F32)
    return f(dm_st), f(qw), f(kw), f(gbd), f(blk), f(bias)


def _mixer_prompt_body(x_ref, win_ref, wout_ref, lng_ref, lnb_ref, gnw_ref, sink_ref,
                       dm_ref, qw_ref, kw_ref, gbd_ref, blk_ref, bias_ref,
                       y_ref, st_ref, ck_ref, cv_ref,
                       proj_ref, oret_ref, merged_ref, s_ref, kprev_ref, vprev_ref, *, tq):
    step = pl.program_id(1)
    nsteps = pl.num_programs(1)
    nchunks = tq // CHUNK

    @pl.when(step == 0)
    def _():
        s_ref[...] = jnp.zeros_like(s_ref)
        kprev_ref[...] = jnp.zeros_like(kprev_ref)
        vprev_ref[...] = jnp.zeros_like(vprev_ref)

    x = x_ref[0]
    proj_ref[...] = _dot(x.astype(BF16), win_ref[...])

    lane = lax.broadcasted_iota(jnp.int32, (CHUNK, LANES), 1)
    lo = lane < HEAD_DIM
    blk = blk_ref[...]

    k_prev = kprev_ref[...]
    v_prev = vprev_ref[...]
    for c in range(nchunks):
        r0 = c * CHUNK
        rows = slice(r0, r0 + CHUNK)
        for p in range(PAIRS):
            cols = slice(p * LANES, (p + 1) * LANES)
            q2 = proj_ref[rows, _RQ + p * LANES:_RQ + (p + 1) * LANES]
            k2 = proj_ref[rows, _RK + p * LANES:_RK + (p + 1) * LANES]
            v2 = proj_ref[rows, _RV + p * LANES:_RV + (p + 1) * LANES].astype(BF16)
            zero = jnp.zeros_like(q2)
            q_st = jnp.concatenate([jnp.where(lo, q2, zero), jnp.where(lo, zero, q2)], axis=0)
            sc = _dot_nt(q_st.astype(BF16), (k2 * QK_SCALE).astype(BF16)) * dm_ref[p]
            qd = q2 * qw_ref[:, cols]
            qd_st = jnp.concatenate([jnp.where(lo, qd, zero), jnp.where(lo, zero, qd)], axis=0)
            lhs = jnp.concatenate([sc.astype(BF16), qd_st.astype(BF16)], axis=1)
            s_old = s_ref[p]
            rhs = jnp.concatenate([v2, s_old.astype(BF16)], axis=0)
            r = _dot(lhs, rhs)
            oret_ref[rows, cols] = jnp.where(lo, r[:CHUNK], r[CHUNK:])
            kd = (k2 * kw_ref[:, cols]).astype(BF16)
            s_ref[p] = gbd_ref[p] * s_old + blk * _dot_tn(kd, v2)
        k_cur = proj_ref[rows, _SK:_SK + KV_W].astype(BF16)
        v_cur = proj_ref[rows, _SV:_SV + KV_W].astype(BF16)
        kk = jnp.concatenate([k_prev, k_cur], axis=0)
        vv = jnp.concatenate([v_prev, v_cur], axis=0)
        for p in range(PAIRS):
            q2 = proj_ref[rows, _SQ + p * LANES:_SQ + (p + 1) * LANES] * QK_SCALE
            zero = jnp.zeros_like(q2)
            q_st = jnp.concatenate([jnp.where(lo, q2, zero), jnp.where(lo, zero, q2)], axis=0)
            s = _dot_nt(q_st.astype(BF16), kk)
            if c == 0:
                s = s + bias_ref[p, jnp.minimum(step, 1)]
            else:
                s = s + bias_ref[p, 1]
            sink = sink_ref[p]
            m = jnp.maximum(jnp.max(s, axis=-1, keepdims=True), sink)
            e = jnp.exp(s - m)
            den = jnp.sum(e, axis=-1, keepdims=True) + jnp.exp(sink - m)
            r = _dot(e.astype(BF16), vv) / den
            o2 = jnp.where(lo, r[:CHUNK], r[CHUNK:])
            merged_ref[rows, RET_W + p * LANES:RET_W + (p + 1) * LANES] = o2.astype(BF16)
        k_prev, v_prev = k_cur, v_cur
    kprev_ref[...] = k_prev
    vprev_ref[...] = v_prev

    lane_t = lax.broadcasted_iota(jnp.int32, (tq, LANES), 1)
    lo_t = lane_t < HEAD_DIM
    for p in range(PAIRS):
        cols = slice(p * LANES, (p + 1) * LANES)
        o = oret_ref[:, cols]
        zero = jnp.zeros_like(o)
        s_lo = jnp.sum(jnp.where(lo_t, o, zero), axis=-1, keepdims=True)
        s_hi = jnp.sum(jnp.where(lo_t, zero, o), axis=-1, keepdims=True)
        d = o - jnp.where(lo_t, s_lo, s_hi) * (1.0 / HEAD_DIM)
        d2 = d * d
        v_lo = jnp.sum(jnp.where(lo_t, d2, zero), axis=-1, keepdims=True)
        v_hi = jnp.sum(jnp.where(lo_t, zero, d2), axis=-1, keepdims=True)
        var = jnp.where(lo_t, v_lo, v_hi) * (1.0 / HEAD_DIM)
        gn = d * lax.rsqrt(var + GN_EPS) * gnw_ref[:, cols]
        gate = proj_ref[:, _RG + p * LANES:_RG + (p + 1) * LANES]
        merged_ref[:, cols] = (gate * jax.nn.sigmoid(gate) * gn).astype(BF16)

    mix = _dot(merged_ref[...], wout_ref[...])
    y_ref[0] = _layer_norm(DN_ALPHA * x + mix, lng_ref[...], lnb_ref[...])

    @pl.when(step == nsteps - 1)
    def _():
        for p in range(PAIRS):
            s_p = s_ref[p]
            st_ref[0, 2 * p] = s_p[:HEAD_DIM, :HEAD_DIM]
            st_ref[0, 2 * p + 1] = pltpu.roll(s_p, HEAD_DIM, 1)[HEAD_DIM:, :HEAD_DIM]
        ck_ref[0] = proj_ref[tq - WINDOW:tq, _SK:_SK + KV_W]
        cv_ref[0] = proj_ref[tq - WINDOW:tq, _SV:_SV + KV_W]


def _mixer_prompt(x, w_in_p, w_out_p, lng, lnb, gnw, sink_st, consts, *, tq):
    bsz, seq, _ = x.shape
    assert seq % tq == 0 and tq % CHUNK == 0
    dm_st, qw, kw, gbd, blk, bias = consts
    body = functools.partial(_mixer_prompt_body, tq=tq)
    return pl.pallas_call(
        body,
        grid=(bsz, seq // tq),
        in_specs=[
            pl.BlockSpec((1, tq, D_MODEL), lambda b, s: (b, s, 0)),
            _const_spec((D_MODEL, IN_COLS)),
            _const_spec((2 * RET_W, D_MODEL)),
            _const_spec((1, D_MODEL)),
            _const_spec((1, D_MODEL)),
            _const_spec((1, RET_W)),
            _const_spec((PAIRS, 2 * CHUNK, 1)),
            _const_spec(dm_st.shape),
            _const_spec(qw.shape),
            _const_spec(kw.shape),
            _const_spec(gbd.shape),
            _const_spec(blk.shape),
            _const_spec(bias.shape),
        ],
        out_specs=[
            pl.BlockSpec((1, tq, D_MODEL), lambda b, s: (b, s, 0)),
            pl.BlockSpec((1, RET_HEADS, HEAD_DIM, HEAD_DIM), lambda b, s: (b, 0, 0, 0)),
            pl.BlockSpec((1, WINDOW, KV_W), lambda b, s: (b, 0, 0)),
            pl.BlockSpec((1, WINDOW, KV_W), lambda b, s: (b, 0, 0)),
        ],
        out_shape=[
            jax.ShapeDtypeStruct((bsz, seq, D_MODEL), F32),
            jax.ShapeDtypeStruct((bsz, RET_HEADS, HEAD_DIM, HEAD_DIM), F32),
            jax.ShapeDtypeStruct((bsz, WINDOW, KV_W), F32),
            jax.ShapeDtypeStruct((bsz, WINDOW, KV_W), F32),
        ],
        scratch_shapes=[
            pltpu.VMEM((tq, IN_COLS), F32),
            pltpu.VMEM((tq, RET_W), F32),
            pltpu.VMEM((tq, 2 * RET_W), BF16),
            pltpu.VMEM((PAIRS, LANES, LANES), F32),
            pltpu.VMEM((CHUNK, KV_W), BF16),
            pltpu.VMEM((CHUNK, KV_W), BF16),
        ],
        compiler_params=pltpu.CompilerParams(
            dimension_semantics=("arbitrary", "arbitrary"), vmem_limit_bytes=VMEM_LIMIT_BYTES),
        name="mixer_prompt",
    )(x, w_in_p, w_out_p, lng, lnb, gnw, sink_st, dm_st, qw, kw, gbd, blk, bias)


def _sample_constants(t):
    lg = _log_gamma()
    hh = np.arange(RET_HEADS)
    tt = np.arange(t, dtype=np.float64)
    row_h = np.repeat(hh, t)
    row_t = np.tile(tt, RET_HEADS)
    col_h = np.repeat(hh, HEAD_DIM)
    same = (row_h[:, None] == col_h[None, :]).astype(np.float64)
    mask_q1 = same * QK_SCALE
    mask_qw = same * np.exp((row_t + 1.0) * lg[row_h])[:, None]
    mask_kw = same * (np.exp((t - 1.0 - row_t) * lg[row_h]) * QK_SCALE)[:, None]
    diff = row_t[:, None] - row_t[None, :]
    same_h = row_h[:, None] == row_h[None, :]
    dm = np.where(same_h & (diff >= 0), np.exp(np.where(diff >= 0, diff, 0.0) * lg[row_h][:, None]), 0.0)
    g_t = np.repeat(np.exp(t * lg), HEAD_DIM)[:, None] * np.ones((1, HEAD_DIM))
    slopes = _alibi_slopes()
    j = np.arange(WINDOW + t)
    dist = row_t[:, None] + WINDOW - j[None, :]
    valid = (dist >= 0) & (dist <= WINDOW)
    bias = np.where(valid, -slopes[row_h][:, None] * dist, NEG_BIG)
    f = lambda a: jnp.asarray(a, F32)
    return f(mask_q1), f(mask_qw), f(mask_kw), f(dm), f(g_t), f(bias)


def _mixer_sample_body(x_ref, st_ref, ck_ref, cv_ref,
                       wqk_ref, wv_ref, wg_ref, wsq_ref, wskv_ref, wor_ref, wos_ref,
                       lng_ref, lnb_ref, gnw_ref, sink_ref,
                       mq1_ref, mqw_ref, mkw_ref, dm_ref, gt_ref, bias_ref,
                       y_ref, sto_ref, cko_ref, cvo_ref,
                       qk_ref, vh_ref, qs_ref, kvn_ref, oh_ref, os_ref, *, sb, t):
    ht = RET_HEADS * t
    x = x_ref[...]
    xb = x.astype(BF16)
    qk_ref[...] = _dot(xb, wqk_ref[...])
    kvn_ref[...] = _dot(xb, wskv_ref[...])
    for h in range(RET_HEADS):
        vh_ref[h] = _dot(xb, wv_ref[h])
        qs_ref[h] = _dot(xb, wsq_ref[h]) * QK_SCALE

    mq1 = mq1_ref[...]
    mqw = mqw_ref[...]
    mkw = mkw_ref[...]
    dm = dm_ref[...]
    g_t = gt_ref[...]
    bias = bias_ref[...]
    sink = sink_ref[...]

    def per_seq(b, carry):
        r0 = pl.multiple_of(b * t, t)
        rows = pl.ds(r0, t)
        q_b = qk_ref[rows, 0:RET_W]
        k_b = qk_ref[rows, RET_W:2 * RET_W]
        q_rep = jnp.concatenate([q_b] * RET_HEADS, axis=0)
        k_rep = jnp.concatenate([k_b] * RET_HEADS, axis=0)
        v_st = vh_ref[:, rows, :].reshape(ht, HEAD_DIM).astype(BF16)
        s0 = st_ref[b]
        sc = _dot_nt((q_rep * mq1).astype(BF16), k_rep.astype(BF16)) * dm
        o = _dot((q_rep * mqw).astype(BF16), s0.astype(BF16)) + _dot(sc.astype(BF16), v_st)
        oh_ref[:, rows, :] = o.reshape(RET_HEADS, t, HEAD_DIM)
        sto_ref[b] = g_t * s0 + _dot_tn((k_rep * mkw).astype(BF16), v_st)
        k_new = kvn_ref[rows, 0:KV_W]
        v_new = kvn_ref[rows, KV_W:2 * KV_W]
        k_old = ck_ref[b]
        v_old = cv_ref[b]
        kk = jnp.concatenate([k_old, k_new], axis=0)
        vv = jnp.concatenate([v_old, v_new], axis=0)
        q_s = qs_ref[:, rows, :].reshape(ht, LANES)
        s = _dot_nt(q_s.astype(BF16), kk.astype(BF16)) + bias
        m = jnp.maximum(jnp.max(s, axis=-1, keepdims=True), sink)
        e = jnp.exp(s - m)
        den = jnp.sum(e, axis=-1, keepdims=True) + jnp.exp(sink - m)
        o_s = _dot(e.astype(BF16), vv.astype(BF16)) / den
        os_ref[:, rows, :] = o_s.reshape(SWA_HEADS, t, LANES)
        cko_ref[b] = kk[t:]
        cvo_ref[b] = vv[t:]
        return carry

    lax.fori_loop(0, sb, per_seq, 0)

    mix = None
    for h in range(RET_HEADS):
        o = oh_ref[h]
        mu = jnp.mean(o, axis=-1, keepdims=True)
        d = o - mu
        var = jnp.mean(d * d, axis=-1, keepdims=True)
        gn = d * lax.rsqrt(var + GN_EPS) * gnw_ref[h]
        gate = _dot(xb, wg_ref[h])
        y_h = (gate * jax.nn.sigmoid(gate) * gn).astype(BF16)
        part = _dot(y_h, wor_ref[h]) + _dot(os_ref[h].astype(BF16), wos_ref[h])
        mix = part if mix is None else mix + part
    y_ref[...] = _layer_norm(DN_ALPHA * x + mix, lng_ref[...], lnb_ref[...])


def _mixer_sample(x, state, ck, cv, weights, lng, lnb, gnw_h, sink_rows, consts, *, sb, t):
    rows_total = x.shape[0]
    nseq = rows_total // t
    assert nseq % sb == 0
    rows = sb * t
    wqk, wv, wg, wsq, wskv, wor, wos = weights
    body = functools.partial(_mixer_sample_body, sb=sb, t=t)
    return pl.pallas_call(
        body,
        grid=(nseq // sb,),
        in_specs=[
            pl.BlockSpec((rows, D_MODEL), lambda i: (i, 0)),
            pl.BlockSpec((sb, RET_W, HEAD_DIM), lambda i: (i, 0, 0)),
            pl.BlockSpec((sb, WINDOW, KV_W), lambda i: (i, 0, 0)),
            pl.BlockSpec((sb, WINDOW, KV_W), lambda i: (i, 0, 0)),
        ] + [_const_spec(w.shape) for w in weights] + [
            _const_spec(lng.shape), _const_spec(lnb.shape), _const_spec(gnw_h.shape),
            _const_spec(sink_rows.shape),
        ] + [_const_spec(c.shape) for c in consts],
        out_specs=[
            pl.BlockSpec((rows, D_MODEL), lambda i: (i, 0)),
            pl.BlockSpec((sb, RET_W, HEAD_DIM), lambda i: (i, 0, 0)),
            pl.BlockSpec((sb, WINDOW, KV_W), lambda i: (i, 0, 0)),
            pl.BlockSpec((sb, WINDOW, KV_W), lambda i: (i, 0, 0)),
        ],
        out_shape=[
            jax.ShapeDtypeStruct((rows_total, D_MODEL), F32),
            jax.ShapeDtypeStruct((nseq, RET_W, HEAD_DIM), F32),
            jax.ShapeDtypeStruct((nseq, WINDOW, KV_W), F32),
            jax.ShapeDtypeStruct((nseq, WINDOW, KV_W), F32),
        ],
        scratch_shapes=[
            pltpu.VMEM((rows, 2 * RET_W), F32),
            pltpu.VMEM((RET_HEADS, rows, HEAD_DIM), F32),
            pltpu.VMEM((SWA_HEADS, rows, LANES), F32),
            pltpu.VMEM((rows, 2 * KV_W), F32),
            pltpu.VMEM((RET_HEADS, rows, HEAD_DIM), F32),
            pltpu.VMEM((SWA_HEADS, rows, LANES), F32),
        ],
        compiler_params=pltpu.CompilerParams(
            dimension_semantics=("arbitrary",), vmem_limit_bytes=VMEM_LIMIT_BYTES),
        name="mixer_sample",
    )(x, state, ck, cv, *weights, lng, lnb, gnw_h, sink_rows, *consts)


def _prep_prompt_weights(w_in, w_out, sinks):
    sq = w_in[:, _SQ:_SQ + SWA_W].reshape(D_MODEL, SWA_HEADS, HEAD_DIM)
    sq = sq[:, jnp.asarray(_SWA_HEAD_ORDER), :].reshape(D_MODEL, SWA_W)
    w_in_p = jnp.concatenate([w_in[:, :_SQ], sq, w_in[:, _SK:]], axis=1).astype(BF16)
    wo_s = w_out[RET_W:].reshape(SWA_HEADS, HEAD_DIM, D_MODEL)
    wo_s = wo_s[jnp.asarray(_SWA_HEAD_ORDER)].reshape(SWA_W, D_MODEL)
    w_out_p = jnp.concatenate([w_out[:RET_W], wo_s], axis=0).astype(BF16)
    pair_heads = jnp.asarray(_SWA_PAIR_HEADS)
    sink_st = jnp.repeat(sinks.astype(F32)[pair_heads], CHUNK, axis=1)[..., None]
    return w_in_p, w_out_p, sink_st


def _prep_sample_weights(w_in, w_out, sinks, gnw, t):
    wqk = w_in[:, _RQ:_RQ + 2 * RET_W].astype(BF16)
    per_head = lambda w: jnp.transpose(w.reshape(D_MODEL, RET_HEADS, HEAD_DIM), (1, 0, 2))
    wv = per_head(w_in[:, _RV:_RV + RET_W]).astype(BF16)
    wg = per_head(w_in[:, _RG:_RG + RET_W]).astype(BF16)
    wsq_h = per_head(w_in[:, _SQ:_SQ + SWA_W])
    group = (jnp.arange(SWA_HEADS) // SWA_GROUP)[:, None, None, None]
    slot = jnp.arange(SWA_KV_HEADS)[None, None, :, None]
    wsq = jnp.where(group == slot, wsq_h[:, :, None, :], 0.0).reshape(SWA_HEADS, D_MODEL, KV_W).astype(BF16)
    wskv = w_in[:, _SK:].astype(BF16)
    wor = w_out[:RET_W].reshape(RET_HEADS, HEAD_DIM, D_MODEL).astype(BF16)
    wos_h = w_out[RET_W:].reshape(SWA_HEADS, 1, HEAD_DIM, D_MODEL)
    slot_r = jnp.arange(SWA_KV_HEADS)[None, :, None, None]
    wos = jnp.where(group == slot_r, wos_h, 0.0).reshape(SWA_HEADS, KV_W, D_MODEL).astype(BF16)
    gnw_h = gnw.astype(F32).reshape(RET_HEADS, 1, HEAD_DIM)
    sink_rows = jnp.repeat(sinks.astype(F32), t)[:, None]
    return (wqk, wv, wg, wsq, wskv, wor, wos), gnw_h, sink_rows


def kernel(x_prompt, x_sample, state_ret, cache_swa_k, cache_swa_v, ln_gain, ln_bias, w_in,
           ret_gn_w, swa_sinks, w_out, ffn1_gate, ffn1_up, ffn1_down, ffn2_gate, ffn2_up, ffn2_down):
    assert ln_gain.shape[0] == DEPTH == 1
    bsz, seq, _ = x_prompt.shape
    nseq, t, _ = x_sample.shape
    lng = ln_gain[0].astype(F32).reshape(3, 1, D_MODEL)
    lnb = ln_bias[0].astype(F32).reshape(3, 1, D_MODEL)
    f1 = (ffn1_gate[0].astype(BF16), ffn1_up[0].astype(BF16), ffn1_down[0].astype(BF16))
    f2 = (ffn2_gate[0].astype(BF16), ffn2_up[0].astype(BF16), ffn2_down[0].astype(BF16))
    gnw = ret_gn_w[0].astype(F32)

    tm_p = 512
    tm_s = min(512, nseq * t)
    xp = x_prompt.reshape(bsz * seq, D_MODEL)
    xs = x_sample.reshape(nseq * t, D_MODEL)

    xp = _ffn_ln(xp, *f1, lng[0], lnb[0], tm=tm_p)
    w_in_p, w_out_p, sink_st = _prep_prompt_weights(w_in[0], w_out[0], swa_sinks[0])
    yp, st_p, ck_p, cv_p = _mixer_prompt(
        xp.reshape(bsz, seq, D_MODEL), w_in_p, w_out_p, lng[1], lnb[1], gnw.reshape(1, RET_W),
        sink_st, _prompt_constants(), tq=min(512, seq))
    yp = _ffn_ln(yp.reshape(bsz * seq, D_MODEL), *f2, lng[2], lnb[2], tm=tm_p)

    xs = _ffn_ln(xs, *f1, lng[0], lnb[0], tm=tm_s)
    weights, gnw_h, sink_rows = _prep_sample_weights(w_in[0], w_out[0], swa_sinks[0], gnw, t)
    ys, st_s, ck_s, cv_s = _mixer_sample(
        xs, state_ret[0].astype(F32).reshape(nseq, RET_W, HEAD_DIM),
        cache_swa_k[0].reshape(nseq, WINDOW, KV_W), cache_swa_v[0].reshape(nseq, WINDOW, KV_W),
        weights, lng[1], lnb[1], gnw_h, sink_rows, _sample_constants(t), sb=min(16, nseq), t=t)
    ys = _ffn_ln(ys, *f2, lng[2], lnb[2], tm=tm_s)

    kv_shape = lambda n: (1, n, WINDOW, SWA_KV_HEADS, HEAD_DIM)
    return (yp.reshape(bsz, seq, D_MODEL), ys.reshape(nseq, t, D_MODEL),
            st_p[None], ck_p.reshape(kv_shape(bsz)), cv_p.reshape(kv_shape(bsz)),
            st_s.reshape(1, nseq, RET_HEADS, HEAD_DIM, HEAD_DIM),
            ck_s.reshape(kv_shape(nseq)), cv_s.reshape(kv_shape(nseq)))
```

```python
import functools

import numpy as np
import jax
import jax.numpy as jnp
from jax import lax
from jax.experimental import pallas as pl
from jax.experimental.pallas import tpu as pltpu

F32 = jnp.float32
BF16 = jnp.bfloat16

D_MODEL = 1024
HEAD_DIM = 64
RET_HEADS = 8
SWA_HEADS = 8
SWA_KV_HEADS = 2
SWA_GROUP = SWA_HEADS // SWA_KV_HEADS
WINDOW = 128
CHUNK = 128
D_FF = 2816
RET_W = RET_HEADS * HEAD_DIM
SWA_W = SWA_HEADS * HEAD_DIM
KV_W = SWA_KV_HEADS * HEAD_DIM
IN_COLS = 4 * RET_W + SWA_W + 2 * KV_W
LN_EPS = 1e-5
GN_EPS = 1e-5
DEPTH = 1
DN_ALPHA = (2.0 * DEPTH) ** 0.25
QK_SCALE = HEAD_DIM ** -0.5
NEG_BIG = -1e30

LANES = 128
PAIRS = RET_HEADS // 2
GROUPS = RET_HEADS // 4
GROUP_W = 4 * HEAD_DIM
VMEM_LIMIT_BYTES = 56 * 1024 * 1024

_RQ, _RK, _RV, _RG, _SQ = 0, RET_W, 2 * RET_W, 3 * RET_W, 4 * RET_W
_SK = 4 * RET_W + SWA_W
_SV = _SK + KV_W

_SWA_PAIR_HEADS = [(p, SWA_GROUP + p) for p in range(PAIRS)]
_SWA_HEAD_ORDER = [h for pair in _SWA_PAIR_HEADS for h in pair]

_FF_CHUNKS = ((0, 1536), (1536, 1280))


def _log_gamma():
    h = np.arange(RET_HEADS, dtype=np.float64)
    return np.log1p(-np.exp2(-5.0 - h))


def _alibi_slopes():
    return np.exp2(-8.0 / SWA_HEADS * np.arange(1, SWA_HEADS + 1, dtype=np.float64))


def _layer_norm(z, g, b):
    mu = jnp.mean(z, axis=-1, keepdims=True)
    d = z - mu
    var = jnp.mean(d * d, axis=-1, keepdims=True)
    return d * lax.rsqrt(var + LN_EPS) * g + b


def _dot(a, b):
    return jnp.dot(a, b, preferred_element_type=F32)


def _dot_nt(a, b):
    return lax.dot_general(a, b, (((1,), (1,)), ((), ())), preferred_element_type=F32)


def _dot_tn(a, b):
    return lax.dot_general(a, b, (((0,), (0,)), ((), ())), preferred_element_type=F32)


def _ffn_ln_body(x_ref, wg_ref, wu_ref, wd_ref, g_ref, b_ref, o_ref):
    x = x_ref[...]
    xb = x.astype(BF16)
    acc = None
    for c0, cw in _FF_CHUNKS:
        gate = _dot(xb, wg_ref[:, c0:c0 + cw])
        up = _dot(xb, wu_ref[:, c0:c0 + cw])
        hid = (gate * jax.nn.sigmoid(gate) * up).astype(BF16)
        part = _dot(hid, wd_ref[c0:c0 + cw, :])
        acc = part if acc is None else acc + part
    z = DN_ALPHA * x + 0.5 * acc
    o_ref[...] = _layer_norm(z, g_ref[...], b_ref[...])


def _const_spec(shape):
    nd = len(shape)
    return pl.BlockSpec(shape, lambda *_: (0,) * nd, pipeline_mode=pl.Buffered(1))


def _ffn_ln(x, wg, wu, wd, g, b, *, tm):
    rows = x.shape[0]
    assert rows % tm == 0
    return pl.pallas_call(
        _ffn_ln_body,
        grid=(rows // tm,),
        in_specs=[
            pl.BlockSpec((tm, D_MODEL), lambda i: (i, 0)),
            _const_spec((D_MODEL, D_FF)),
            _const_spec((D_MODEL, D_FF)),
            _const_spec((D_FF, D_MODEL)),
            _const_spec((1, D_MODEL)),
            _const_spec((1, D_MODEL)),
        ],
        out_specs=pl.BlockSpec((tm, D_MODEL), lambda i: (i, 0)),
        out_shape=jax.ShapeDtypeStruct((rows, D_MODEL), F32),
        compiler_params=pltpu.CompilerParams(
            dimension_semantics=("arbitrary",), vmem_limit_bytes=VMEM_LIMIT_BYTES),
        name="ffn_ln",
    )(x, wg, wu, wd, g, b)


def _prompt_constants():
    lg = _log_gamma()
    idx = np.arange(CHUNK, dtype=np.float64)
    diff = idx[:, None] - idx[None, :]
    dm = np.where(diff >= 0, np.exp(np.where(diff >= 0, diff, 0.0)[None] * lg[:, None, None]), 0.0)
    dm_st = np.stack([np.block([[dm[4 * g], dm[4 * g + 2]], [dm[4 * g + 1], dm[4 * g + 3]]])
                      for g in range(GROUPS)])
    qw = np.repeat(np.exp((idx + 1.0)[:, None] * lg[None, :]), HEAD_DIM, axis=1)
    kw = np.repeat(np.exp((CHUNK - 1.0 - idx)[:, None] * lg[None, :]), HEAD_DIM, axis=1) * QK_SCALE
    gc = np.exp(CHUNK * lg)
    blk = np.kron(np.eye(4), np.ones((HEAD_DIM, HEAD_DIM)))
    gbd = np.stack([blk * np.repeat(gc[4 * g:4 * g + 4], HEAD_DIM)[:, None] for g in range(GROUPS)])
    slopes = _alibi_slopes()
    i = np.arange(CHUNK)
    j = np.arange(2 * CHUNK)
    dist = i[:, None] + CHUNK - j[None, :]
    within = (dist >= 0) & (dist <= WINDOW)
    valid = [within & (j >= CHUNK)[None, :], within]
    bias = np.stack([np.concatenate([np.where(valid[v], -slopes[h] * dist, NEG_BIG).T
                                     for h in _SWA_HEAD_ORDER], axis=1) for v in range(2)])
    f = lambda a: jnp.asarray(a, F32)
    return f(dm_st), f(qw), f(kw), f(gbd), f(blk), f(bias)


def _mixer_prompt_body(x_ref, win_ref, wout_ref, lng_ref, lnb_ref, gnw_ref, sink_ref,
                       dm_ref, qw_ref, kw_ref, gbd_ref, blk_ref, bias_ref,
                       y_ref, st_ref, ck_ref, cv_ref,
                       proj_ref, oret_ref, yret_ref, swat_ref, s_ref, kprev_ref, vprev_ref, *, tq):
    step = pl.program_id(1)
    nsteps = pl.num_programs(1)
    nchunks = tq // CHUNK

    @pl.when(step == 0)
    def _():
        s_ref[...] = jnp.zeros_like(s_ref)
        kprev_ref[...] = jnp.zeros_like(kprev_ref)
        vprev_ref[...] = jnp.zeros_like(vprev_ref)

    x = x_ref[0]
    proj_ref[...] = _dot(x.astype(BF16), win_ref[...])

    lane = lax.broadcasted_iota(jnp.int32, (CHUNK, LANES), 1)
    lo = lane < HEAD_DIM
    lane_g = lax.broadcasted_iota(jnp.int32, (CHUNK, GROUP_W), 1)
    lo_g = jnp.bitwise_and(lane_g, LANES - 1) < HEAD_DIM
    blk = blk_ref[...]
    zero_b = jnp.zeros((CHUNK, LANES), BF16)

    def pair_diag(a):
        return jnp.concatenate([jnp.concatenate([a[:, :LANES], zero_b], axis=1),
                                jnp.concatenate([zero_b, a[:, LANES:]], axis=1)], axis=0)

    def head_stack(a):
        zero = jnp.zeros_like(a)
        return jnp.concatenate([jnp.where(lo_g, a, zero), jnp.where(lo_g, zero, a)], axis=0)

    k_prev = kprev_ref[...]
    vt_prev = vprev_ref[...]
    for c in range(nchunks):
        r0 = c * CHUNK
        rows = slice(r0, r0 + CHUNK)
        for g in range(GROUPS):
            cols = slice(g * GROUP_W, (g + 1) * GROUP_W)
            q4 = proj_ref[rows, _RQ + g * GROUP_W:_RQ + (g + 1) * GROUP_W]
            k4 = proj_ref[rows, _RK + g * GROUP_W:_RK + (g + 1) * GROUP_W]
            v4 = proj_ref[rows, _RV + g * GROUP_W:_RV + (g + 1) * GROUP_W].astype(BF16)
            k_bd = pair_diag((k4 * QK_SCALE).astype(BF16))
            sc = _dot_nt(head_stack(q4).astype(BF16), k_bd) * dm_ref[g]
            qd_st = head_stack(q4 * qw_ref[:, cols])
            lhs = jnp.concatenate([sc.astype(BF16), qd_st.astype(BF16)], axis=1)
            s_old = s_ref[g]
            rhs = jnp.concatenate([pair_diag(v4), s_old.astype(BF16)], axis=0)
            r = _dot(lhs, rhs)
            oret_ref[rows, cols] = jnp.where(lo_g, r[:CHUNK], r[CHUNK:])
            kd = (k4 * kw_ref[:, cols]).astype(BF16)
            s_ref[g] = gbd_ref[g] * s_old + blk * _dot_tn(kd, v4)
        k_cur = proj_ref[rows, _SK:_SK + KV_W].astype(BF16)
        vt_cur = proj_ref[rows, _SV:_SV + KV_W].T.astype(BF16)
        kk = jnp.concatenate([k_prev, k_cur], axis=0)
        vvt = jnp.concatenate([vt_prev, vt_cur], axis=1)
        q_parts = []
        for p in range(PAIRS):
            q2 = proj_ref[rows, _SQ + p * LANES:_SQ + (p + 1) * LANES] * QK_SCALE
            zero = jnp.zeros_like(q2)
            q_parts += [jnp.where(lo, q2, zero).astype(BF16), jnp.where(lo, zero, q2).astype(BF16)]
        st = _dot_nt(kk, jnp.concatenate(q_parts, axis=0))
        if c == 0:
            st = st + bias_ref[jnp.minimum(step, 1)]
        else:
            st = st + bias_ref[1]
        sink = sink_ref[...]
        m = jnp.maximum(jnp.max(st, axis=0, keepdims=True), sink)
        e = jnp.exp(st - m)
        den = jnp.sum(e, axis=0, keepdims=True) + jnp.exp(sink - m)
        pt = (e * (1.0 / den)).astype(BF16)
        ot = _dot(vvt, pt)
        for p in range(PAIRS):
            c0 = 2 * p * CHUNK
            swat_ref[p * LANES:p * LANES + HEAD_DIM, rows] = ot[:HEAD_DIM, c0:c0 + CHUNK]
            swat_ref[p * LANES + HEAD_DIM:(p + 1) * LANES, rows] = ot[HEAD_DIM:, c0 + CHUNK:c0 + 2 * CHUNK]
        k_prev, vt_prev = k_cur, vt_cur
    kprev_ref[...] = k_prev
    vprev_ref[...] = vt_prev

    lane_t = lax.broadcasted_iota(jnp.int32, (tq, LANES), 1)
    lo_t = lane_t < HEAD_DIM
    for p in range(PAIRS):
        cols = slice(p * LANES, (p + 1) * LANES)
        o = oret_ref[:, cols]
        zero = jnp.zeros_like(o)
        s_lo = jnp.sum(jnp.where(lo_t, o, zero), axis=-1, keepdims=True)
        s_hi = jnp.sum(jnp.where(lo_t, zero, o), axis=-1, keepdims=True)
        d = o - jnp.where(lo_t, s_lo, s_hi) * (1.0 / HEAD_DIM)
        d2 = d * d
        v_lo = jnp.sum(jnp.where(lo_t, d2, zero), axis=-1, keepdims=True)
        v_hi = jnp.sum(jnp.where(lo_t, zero, d2), axis=-1, keepdims=True)
        var = jnp.where(lo_t, v_lo, v_hi) * (1.0 / HEAD_DIM)
        gn = d * lax.rsqrt(var + GN_EPS) * gnw_ref[:, cols]
        gate = proj_ref[:, _RG + p * LANES:_RG + (p + 1) * LANES]
        yret_ref[:, cols] = (gate * jax.nn.sigmoid(gate) * gn).astype(BF16)

    mix = (_dot(yret_ref[...], wout_ref[:RET_W, :])
           + _dot_tn(swat_ref[...].astype(BF16), wout_ref[RET_W:, :]))
    y_ref[0] = _layer_norm(DN_ALPHA * x + mix, lng_ref[...], lnb_ref[...])

    @pl.when(step == nsteps - 1)
    def _():
        for h in range(RET_HEADS):
            g, i = divmod(h, 4)
            blk_h = s_ref[g, i * HEAD_DIM:(i + 1) * HEAD_DIM, (i // 2) * LANES:(i // 2 + 1) * LANES]
            if i % 2:
                blk_h = pltpu.roll(blk_h, HEAD_DIM, 1)
            st_ref[0, h] = blk_h[:, :HEAD_DIM]
        ck_ref[0] = proj_ref[tq - WINDOW:tq, _SK:_SK + KV_W]
        cv_ref[0] = proj_ref[tq - WINDOW:tq, _SV:_SV + KV_W]


def _mixer_prompt(x, w_in_p, w_out_p, lng, lnb, gnw, sink_st, consts, *, tq):
    bsz, seq, _ = x.shape
    assert seq % tq == 0 and tq % CHUNK == 0
    dm_st, qw, kw, gbd, blk, bias = consts
    body = functools.partial(_mixer_prompt_body, tq=tq)
    return pl.pallas_call(
        body,
        grid=(bsz, seq // tq),
        in_specs=[
            pl.BlockSpec((1, tq, D_MODEL), lambda b, s: (b, s, 0)),
            _const_spec((D_MODEL, IN_COLS)),
            _const_spec((2 * RET_W, D_MODEL)),
            _const_spec((1, D_MODEL)),
            _const_spec((1, D_MODEL)),
            _const_spec((1, RET_W)),
            _const_spec((1, SWA_HEADS * CHUNK)),
            _const_spec(dm_st.shape),
            _const_spec(qw.shape),
            _const_spec(kw.shape),
            _const_spec(gbd.shape),
            _const_spec(blk.shape),
            _const_spec(bias.shape),
        ],
        out_specs=[
            pl.BlockSpec((1, tq, D_MODEL), lambda b, s: (b, s, 0)),
            pl.BlockSpec((1, RET_HEADS, HEAD_DIM, HEAD_DIM), lambda b, s: (b, 0, 0, 0)),
            pl.BlockSpec((1, WINDOW, KV_W), lambda b, s: (b, 0, 0)),
            pl.BlockSpec((1, WINDOW, KV_W), lambda b, s: (b, 0, 0)),
        ],
        out_shape=[
            jax.ShapeDtypeStruct((bsz, seq, D_MODEL), F32),
            jax.ShapeDtypeStruct((bsz, RET_HEADS, HEAD_DIM, HEAD_DIM), F32),
            jax.ShapeDtypeStruct((bsz, WINDOW, KV_W), F32),
            jax.ShapeDtypeStruct((bsz, WINDOW, KV_W), F32),
        ],
        scratch_shapes=[
            pltpu.VMEM((tq, IN_COLS), F32),
            pltpu.VMEM((tq, RET_W), F32),
            pltpu.VMEM((tq, RET_W), BF16),
            pltpu.VMEM((SWA_W, tq), F32),
            pltpu.VMEM((GROUPS, GROUP_W, GROUP_W), F32),
            pltpu.VMEM((CHUNK, KV_W), BF16),
            pltpu.VMEM((KV_W, CHUNK), BF16),
        ],
        compiler_params=pltpu.CompilerParams(
            dimension_semantics=("arbitrary", "arbitrary"), vmem_limit_bytes=VMEM_LIMIT_BYTES),
        name="mixer_prompt",
    )(x, w_in_p, w_out_p, lng, lnb, gnw, sink_st, dm_st, qw, kw, gbd, blk, bias)


def _sample_constants(t):
    lg = _log_gamma()
    hh = np.arange(RET_HEADS)
    tt = np.arange(t, dtype=np.float64)
    row_h = np.repeat(hh, t)
    row_t = np.tile(tt, RET_HEADS)
    col_h = np.repeat(hh, HEAD_DIM)
    same = (row_h[:, None] == col_h[None, :]).astype(np.float64)
    mask_q1 = same * QK_SCALE
    mask_qw = same * np.exp((row_t + 1.0) * lg[row_h])[:, None]
    mask_kw = same * (np.exp((t - 1.0 - row_t) * lg[row_h]) * QK_SCALE)[:, None]
    diff = row_t[:, None] - row_t[None, :]
    same_h = row_h[:, None] == row_h[None, :]
    dm = np.where(same_h & (diff >= 0), np.exp(np.where(diff >= 0, diff, 0.0) * lg[row_h][:, None]), 0.0)
    g_t = np.repeat(np.exp(t * lg), HEAD_DIM)[:, None] * np.ones((1, HEAD_DIM))
    slopes = _alibi_slopes()
    j = np.arange(WINDOW + t)
    dist = row_t[:, None] + WINDOW - j[None, :]
    valid = (dist >= 0) & (dist <= WINDOW)
    bias = np.where(valid, -slopes[row_h][:, None] * dist, NEG_BIG)
    f = lambda a: jnp.asarray(a, F32)
    return f(mask_q1), f(mask_qw), f(mask_kw), f(dm), f(g_t), f(bias)


def _mixer_sample_body(x_ref, st_ref, ck_ref, cv_ref,
                       wqk_ref, wv_ref, wg_ref, wsq_ref, wskv_ref, wor_ref, wos_ref,
                       lng_ref, lnb_ref, gnw_ref, sink_ref,
                       mq1_ref, mqw_ref, mkw_ref, dm_ref, gt_ref, bias_ref,
                       y_ref, sto_ref, cko_ref, cvo_ref,
                       qk_ref, vh_ref, qs_ref, kvn_ref, oh_ref, os_ref, *, sb, t):
    ht = RET_HEADS * t
    x = x_ref[...]
    xb = x.astype(BF16)
    qk_ref[...] = _dot(xb, wqk_ref[...])
    kvn_ref[...] = _dot(xb, wskv_ref[...])
    for h in range(RET_HEADS):
        vh_ref[h] = _dot(xb, wv_ref[h])
        qs_ref[h] = _dot(xb, wsq_ref[h]) * QK_SCALE

    mq1 = mq1_ref[...]
    mqw = mqw_ref[...]
    mkw = mkw_ref[...]
    dm = dm_ref[...]
    g_t = gt_ref[...]
    bias = bias_ref[...]
    sink = sink_ref[...]

    def per_seq(b, carry):
        r0 = pl.multiple_of(b * t, t)
        rows = pl.ds(r0, t)
        q_b = qk_ref[rows, 0:RET_W]
        k_b = qk_ref[rows, RET_W:2 * RET_W]
        q_rep = jnp.concatenate([q_b] * RET_HEADS, axis=0)
        k_rep = jnp.concatenate([k_b] * RET_HEADS, axis=0)
        v_st = vh_ref[:, rows, :].reshape(ht, HEAD_DIM).astype(BF16)
        s0 = st_ref[b]
        sc = _dot_nt((q_rep * mq1).astype(BF16), k_rep.astype(BF16)) * dm
        o = _dot((q_rep * mqw).astype(BF16), s0.astype(BF16)) + _dot(sc.astype(BF16), v_st)
        oh_ref[:, rows, :] = o.reshape(RET_HEADS, t, HEAD_DIM)
        sto_ref[b] = g_t * s0 + _dot_tn((k_rep * mkw).astype(BF16), v_st)
        k_new = kvn_ref[rows, 0:KV_W]
        v_new = kvn_ref[rows, KV_W:2 * KV_W]
        k_old = ck_ref[b]
        v_old = cv_ref[b]
        kk = jnp.concatenate([k_old, k_new], axis=0)
        vv = jnp.concatenate([v_old, v_new], axis=0)
        q_s = qs_ref[:, rows, :].reshape(ht, LANES)
        s = _dot_nt(q_s.astype(BF16), kk.astype(BF16)) + bias
        m = jnp.maximum(jnp.max(s, axis=-1, keepdims=True), sink)
        e = jnp.exp(s - m)
        den = jnp.sum(e, axis=-1, keepdims=True) + jnp.exp(sink - m)
        o_s = _dot(e.astype(BF16), vv.astype(BF16)) / den
        os_ref[:, rows, :] = o_s.reshape(SWA_HEADS, t, LANES)
        cko_ref[b] = kk[t:]
        cvo_ref[b] = vv[t:]
        return carry

    lax.fori_loop(0, sb, per_seq, 0)

    mix = None
    for h in range(RET_HEADS):
        o = oh_ref[h]
        mu = jnp.mean(o, axis=-1, keepdims=True)
        d = o - mu
        var = jnp.mean(d * d, axis=-1, keepdims=True)
        gn = d * lax.rsqrt(var + GN_EPS) * gnw_ref[h]
        gate = _dot(xb, wg_ref[h])
        y_h = (gate * jax.nn.sigmoid(gate) * gn).astype(BF16)
        part = _dot(y_h, wor_ref[h]) + _dot(os_ref[h].astype(BF16), wos_ref[h])
        mix = part if mix is None else mix + part
    y_ref[...] = _layer_norm(DN_ALPHA * x + mix, lng_ref[...], lnb_ref[...])


def _mixer_sample(x, state, ck, cv, weights, lng, lnb, gnw_h, sink_rows, consts, *, sb, t):
    rows_total = x.shape[0]
    nseq = rows_total // t
    assert nseq % sb == 0
    rows = sb * t
    wqk, wv, wg, wsq, wskv, wor, wos = weights
    body = functools.partial(_mixer_sample_body, sb=sb, t=t)
    return pl.pallas_call(
        body,
        grid=(nseq // sb,),
        in_specs=[
            pl.BlockSpec((rows, D_MODEL), lambda i: (i, 0)),
            pl.BlockSpec((sb, RET_W, HEAD_DIM), lambda i: (i, 0, 0)),
            pl.BlockSpec((sb, WINDOW, KV_W), lambda i: (i, 0, 0)),
            pl.BlockSpec((sb, WINDOW, KV_W), lambda i: (i, 0, 0)),
        ] + [_const_spec(w.shape) for w in weights] + [
            _const_spec(lng.shape), _const_spec(lnb.shape), _const_spec(gnw_h.shape),
            _const_spec(sink_rows.shape),
        ] + [_const_spec(c.shape) for c in consts],
        out_specs=[
            pl.BlockSpec((rows, D_MODEL), lambda i: (i, 0)),
            pl.BlockSpec((sb, RET_W, HEAD_DIM), lambda i: (i, 0, 0)),
            pl.BlockSpec((sb, WINDOW, KV_W), lambda i: (i, 0, 0)),
            pl.BlockSpec((sb, WINDOW, KV_W), lambda i: (i, 0, 0)),
        ],
        out_shape=[
            jax.ShapeDtypeStruct((rows_total, D_MODEL), F32),
            jax.ShapeDtypeStruct((nseq, RET_W, HEAD_DIM), F32),
            jax.ShapeDtypeStruct((nseq, WINDOW, KV_W), F32),
            jax.ShapeDtypeStruct((nseq, WINDOW, KV_W), F32),
        ],
        scratch_shapes=[
            pltpu.VMEM((rows, 2 * RET_W), F32),
            pltpu.VMEM((RET_HEADS, rows, HEAD_DIM), F32),
            pltpu.VMEM((SWA_HEADS, rows, LANES), F32),
            pltpu.VMEM((rows, 2 * KV_W), F32),
            pltpu.VMEM((RET_HEADS, rows, HEAD_DIM), F32),
            pltpu.VMEM((SWA_HEADS, rows, LANES), F32),
        ],
        compiler_params=pltpu.CompilerParams(
            dimension_semantics=("arbitrary",), vmem_limit_bytes=VMEM_LIMIT_BYTES),
        name="mixer_sample",
    )(x, state, ck, cv, *weights, lng, lnb, gnw_h, sink_rows, *consts)


def _prep_prompt_weights(w_in, w_out, sinks):
    sq = w_in[:, _SQ:_SQ + SWA_W].reshape(D_MODEL, SWA_HEADS, HEAD_DIM)
    sq = sq[:, jnp.asarray(_SWA_HEAD_ORDER), :].reshape(D_MODEL, SWA_W)
    w_in_p = jnp.concatenate([w_in[:, :_SQ], sq, w_in[:, _SK:]], axis=1).astype(BF16)
    wo_s = w_out[RET_W:].reshape(SWA_HEADS, HEAD_DIM, D_MODEL)
    wo_s = wo_s[jnp.asarray(_SWA_HEAD_ORDER)].reshape(SWA_W, D_MODEL)
    w_out_p = jnp.concatenate([w_out[:RET_W], wo_s], axis=0).astype(BF16)
    sink_st = jnp.repeat(sinks.astype(F32)[jnp.asarray(_SWA_HEAD_ORDER)], CHUNK)[None, :]
    return w_in_p, w_out_p, sink_st


def _prep_sample_weights(w_in, w_out, sinks, gnw, t):
    wqk = w_in[:, _RQ:_RQ + 2 * RET_W].astype(BF16)
    per_head = lambda w: jnp.transpose(w.reshape(D_MODEL, RET_HEADS, HEAD_DIM), (1, 0, 2))
    wv = per_head(w_in[:, _RV:_RV + RET_W]).astype(BF16)
    wg = per_head(w_in[:, _RG:_RG + RET_W]).astype(BF16)
    wsq_h = per_head(w_in[:, _SQ:_SQ + SWA_W])
    group = (jnp.arange(SWA_HEADS) // SWA_GROUP)[:, None, None, None]
    slot = jnp.arange(SWA_KV_HEADS)[None, None, :, None]
    wsq = jnp.where(group == slot, wsq_h[:, :, None, :], 0.0).reshape(SWA_HEADS, D_MODEL, KV_W).astype(BF16)
    wskv = w_in[:, _SK:].astype(BF16)
    wor = w_out[:RET_W].reshape(RET_HEADS, HEAD_DIM, D_MODEL).astype(BF16)
    wos_h = w_out[RET_W:].reshape(SWA_HEADS, 1, HEAD_DIM, D_MODEL)
    slot_r = jnp.arange(SWA_KV_HEADS)[None, :, None, None]
    wos = jnp.where(group == slot_r, wos_h, 0.0).reshape(SWA_HEADS, KV_W, D_MODEL).astype(BF16)
    gnw_h = gnw.astype(F32).reshape(RET_HEADS, 1, HEAD_DIM)
    sink_rows = jnp.repeat(sinks.astype(F32), t)[:, None]
    return (wqk, wv, wg, wsq, wskv, wor, wos), gnw_h, sink_rows


def kernel(x_prompt, x_sample, state_ret, cache_swa_k, cache_swa_v, ln_gain, ln_bias, w_in,
           ret_gn_w, swa_sinks, w_out, ffn1_gate, ffn1_up, ffn1_down, ffn2_gate, ffn2_up, ffn2_down):
    assert ln_gain.shape[0] == DEPTH == 1
    bsz, seq, _ = x_prompt.shape
    nseq, t, _ = x_sample.shape
    lng = ln_gain[0].astype(F32).reshape(3, 1, D_MODEL)
    lnb = ln_bias[0].astype(F32).reshape(3, 1, D_MODEL)
    f1 = (ffn1_gate[0].astype(BF16), ffn1_up[0].astype(BF16), ffn1_down[0].astype(BF16))
    f2 = (ffn2_gate[0].astype(BF16), ffn2_up[0].astype(BF16), ffn2_down[0].astype(BF16))
    gnw = ret_gn_w[0].astype(F32)

    tm_p = 512
    tm_s = min(512, nseq * t)
    xp = x_prompt.reshape(bsz * seq, D_MODEL)
    xs = x_sample.reshape(nseq * t, D_MODEL)

    xp = _ffn_ln(xp, *f1, lng[0], lnb[0], tm=tm_p)
    w_in_p, w_out_p, sink_st = _prep_prompt_weights(w_in[0], w_out[0], swa_sinks[0])
    yp, st_p, ck_p, cv_p = _mixer_prompt(
        xp.reshape(bsz, seq, D_MODEL), w_in_p, w_out_p, lng[1], lnb[1], gnw.reshape(1, RET_W),
        sink_st, _prompt_constants(), tq=min(512, seq))
    yp = _ffn_ln(yp.reshape(bsz * seq, D_MODEL), *f2, lng[2], lnb[2], tm=tm_p)

    xs = _ffn_ln(xs, *f1, lng[0], lnb[0], tm=tm_s)
    weights, gnw_h, sink_rows = _prep_sample_weights(w_in[0], w_out[0], swa_sinks[0], gnw, t)
    ys, st_s, ck_s, cv_s = _mixer_sample(
        xs, state_ret[0].astype(F32).reshape(nseq, RET_W, HEAD_DIM),
        cache_swa_k[0].reshape(nseq, WINDOW, KV_W), cache_swa_v[0].reshape(nseq, WINDOW, KV_W),
        weights, lng[1], lnb[1], gnw_h, sink_rows, _sample_constants(t), sb=min(16, nseq), t=t)
    ys = _ffn_ln(ys, *f2, lng[2], lnb[2], tm=tm_s)

    kv_shape = lambda n: (1, n, WINDOW, SWA_KV_HEADS, HEAD_DIM)
    return (yp.reshape(bsz, seq, D_MODEL), ys.reshape(nseq, t, D_MODEL),
            st_p[None], ck_p.reshape(kv_shape(bsz)), cv_p.reshape(kv_shape(bsz)),
            st_s.reshape(1, nseq, RET_HEADS, HEAD_DIM, HEAD_DIM),
            ck_s.reshape(kv_shape(nseq)), cv_s.reshape(kv_shape(nseq)))
```

```python
import functools

import numpy as np
import jax
import jax.numpy as jnp
from jax import lax
from jax.experimental import pallas as pl
from jax.experimental.pallas import tpu as pltpu

F32 = jnp.float32
BF16 = jnp.bfloat16

D_MODEL = 1024
HEAD_DIM = 64
RET_HEADS = 8
SWA_HEADS = 8
SWA_KV_HEADS = 2
SWA_GROUP = SWA_HEADS // SWA_KV_HEADS
WINDOW = 128
CHUNK = 128
D_FF = 2816
RET_W = RET_HEADS * HEAD_DIM
SWA_W = SWA_HEADS * HEAD_DIM
KV_W = SWA_KV_HEADS * HEAD_DIM
IN_COLS = 4 * RET_W + SWA_W + 2 * KV_W
LN_EPS = 1e-5
GN_EPS = 1e-5
DEPTH = 1
DN_ALPHA = (2.0 * DEPTH) ** 0.25
QK_SCALE = HEAD_DIM ** -0.5
NEG_BIG = -1e30
LOG2E = 1.4426950408889634

LANES = 128
PAIRS = RET_HEADS // 2
GROUPS = RET_HEADS // 4
GROUP_W = 4 * HEAD_DIM
PROJ_ROWS = 256
PROJ_COLS = 512
VMEM_LIMIT_BYTES = 56 * 1024 * 1024

_RQ, _RK, _RV, _RG, _SQ = 0, RET_W, 2 * RET_W, 3 * RET_W, 4 * RET_W
_SK = 4 * RET_W + SWA_W
_SV = _SK + KV_W

_SWA_PAIR_HEADS = [(p, SWA_GROUP + p) for p in range(PAIRS)]
_SWA_HEAD_ORDER = [h for pair in _SWA_PAIR_HEADS for h in pair]

_FF_CHUNKS = ((0, 1536), (1536, 1280))


def _log_gamma():
    h = np.arange(RET_HEADS, dtype=np.float64)
    return np.log1p(-np.exp2(-5.0 - h))


def _alibi_slopes():
    return np.exp2(-8.0 / SWA_HEADS * np.arange(1, SWA_HEADS + 1, dtype=np.float64))


def _layer_norm(z, g, b):
    mu = jnp.mean(z, axis=-1, keepdims=True)
    d = z - mu
    var = jnp.mean(d * d, axis=-1, keepdims=True)
    return d * lax.rsqrt(var + LN_EPS) * g + b


def _dot(a, b):
    return jnp.dot(a, b, preferred_element_type=F32)


def _dot_nt(a, b):
    return lax.dot_general(a, b, (((1,), (1,)), ((), ())), preferred_element_type=F32)


def _dot_tn(a, b):
    return lax.dot_general(a, b, (((0,), (0,)), ((), ())), preferred_element_type=F32)


def _ffn_ln_body(x_ref, wg_ref, wu_ref, wd_ref, g_ref, b_ref, o_ref):
    x = x_ref[...]
    xb = x.astype(BF16)
    acc = None
    for c0, cw in _FF_CHUNKS:
        gate = _dot(xb, wg_ref[:, c0:c0 + cw])
        up = _dot(xb, wu_ref[:, c0:c0 + cw])
        hid = (gate * jax.nn.sigmoid(gate) * up).astype(BF16)
        part = _dot(hid, wd_ref[c0:c0 + cw, :])
        acc = part if acc is None else acc + part
    z = DN_ALPHA * x + 0.5 * acc
    o_ref[...] = _layer_norm(z, g_ref[...], b_ref[...])


def _const_spec(shape):
    nd = len(shape)
    return pl.BlockSpec(shape, lambda *_: (0,) * nd, pipeline_mode=pl.Buffered(1))


def _ffn_ln(x, wg, wu, wd, g, b, *, tm):
    rows = x.shape[0]
    assert rows % tm == 0
    return pl.pallas_call(
        _ffn_ln_body,
        grid=(rows // tm,),
        in_specs=[
            pl.BlockSpec((tm, D_MODEL), lambda i: (i, 0)),
            _const_spec((D_MODEL, D_FF)),
            _const_spec((D_MODEL, D_FF)),
            _const_spec((D_FF, D_MODEL)),
            _const_spec((1, D_MODEL)),
            _const_spec((1, D_MODEL)),
        ],
        out_specs=pl.BlockSpec((tm, D_MODEL), lambda i: (i, 0)),
        out_shape=jax.ShapeDtypeStruct((rows, D_MODEL), F32),
        compiler_params=pltpu.CompilerParams(
            dimension_semantics=("arbitrary",), vmem_limit_bytes=VMEM_LIMIT_BYTES),
        name="ffn_ln",
    )(x, wg, wu, wd, g, b)


def _prompt_constants():
    lg = _log_gamma()
    idx = np.arange(CHUNK, dtype=np.float64)
    diff = idx[:, None] - idx[None, :]
    dm = np.where(diff >= 0, np.exp(np.where(diff >= 0, diff, 0.0)[None] * lg[:, None, None]), 0.0)
    dm_st = np.stack([np.block([[dm[4 * g], dm[4 * g + 2]], [dm[4 * g + 1], dm[4 * g + 3]]])
                      for g in range(GROUPS)])
    qw = np.repeat(np.exp((idx + 1.0)[:, None] * lg[None, :]), HEAD_DIM, axis=1)
    kw = np.repeat(np.exp((CHUNK - 1.0 - idx)[:, None] * lg[None, :]), HEAD_DIM, axis=1) * QK_SCALE
    gc = np.exp(CHUNK * lg)
    blk = np.kron(np.eye(4), np.ones((HEAD_DIM, HEAD_DIM)))
    gbd = np.stack([blk * np.repeat(gc[4 * g:4 * g + 4], HEAD_DIM)[:, None] for g in range(GROUPS)])
    slopes = _alibi_slopes()
    i = np.arange(CHUNK)
    j = np.arange(2 * CHUNK)
    dist = i[:, None] + CHUNK - j[None, :]
    within = (dist >= 0) & (dist <= WINDOW)
    valid = [within & (j >= CHUNK)[None, :], within]
    bias = np.stack([np.concatenate([np.where(valid[v], -slopes[h] * dist * LOG2E, NEG_BIG).T
                                     for h in _SWA_HEAD_ORDER], axis=1) for v in range(2)])
    f = lambda a: jnp.asarray(a, F32)
    return f(dm_st), f(qw), f(kw), f(gbd), f(blk), f(bias)


def _mixer_prompt_body(x_ref, win_ref, wout_ref, lng_ref, lnb_ref, gnw_ref, sink_ref,
                       dm_ref, qw_ref, kw_ref, gbd_ref, blk_ref, bias_ref,
                       y_ref, st_ref, ck_ref, cv_ref,
                       proj_ref, oret_ref, yret_ref, swat_ref, s_ref, kprev_ref, vprev_ref, *, tq):
    step = pl.program_id(1)
    nsteps = pl.num_programs(1)
    nchunks = tq // CHUNK

    @pl.when(step == 0)
    def _():
        s_ref[...] = jnp.zeros_like(s_ref)
        kprev_ref[...] = jnp.zeros_like(kprev_ref)
        vprev_ref[...] = jnp.zeros_like(vprev_ref)

    x = x_ref[0]

    def project_piece(r0, c0):
        c1 = min(c0 + PROJ_COLS, IN_COLS)
        proj_ref[r0:r0 + PROJ_ROWS, c0:c1] = _dot(x[r0:r0 + PROJ_ROWS].astype(BF16), win_ref[:, c0:c1])

    pending = []

    def emit_pieces(n):
        for _ in range(min(n, len(pending))):
            pending.pop(0)()

    for c0 in range(0, IN_COLS, PROJ_COLS):
        project_piece(0, c0)
    pieces_per_chunk = pl.cdiv(pl.cdiv(IN_COLS, PROJ_COLS), PROJ_ROWS // CHUNK)

    lane = lax.broadcasted_iota(jnp.int32, (CHUNK, LANES), 1)
    lo = lane < HEAD_DIM
    lane_g = lax.broadcasted_iota(jnp.int32, (CHUNK, GROUP_W), 1)
    lo_g = jnp.bitwise_and(lane_g, LANES - 1) < HEAD_DIM
    blk = blk_ref[...]
    zero_b = jnp.zeros((CHUNK, LANES), BF16)

    def pair_diag(a):
        return jnp.concatenate([jnp.concatenate([a[:, :LANES], zero_b], axis=1),
                                jnp.concatenate([zero_b, a[:, LANES:]], axis=1)], axis=0)

    def head_stack(a):
        zero = jnp.zeros_like(a)
        return jnp.concatenate([jnp.where(lo_g, a, zero), jnp.where(lo_g, zero, a)], axis=0)

    k_prev = kprev_ref[...]
    vt_prev = vprev_ref[...]
    for c in range(nchunks):
        r0 = c * CHUNK
        rows = slice(r0, r0 + CHUNK)
        if r0 % PROJ_ROWS == 0 and r0 + PROJ_ROWS < tq:
            pending.extend(functools.partial(project_piece, r0 + PROJ_ROWS, c0)
                           for c0 in range(0, IN_COLS, PROJ_COLS))
        budget = pieces_per_chunk
        for g in range(GROUPS):
            cols = slice(g * GROUP_W, (g + 1) * GROUP_W)
            q4 = proj_ref[rows, _RQ + g * GROUP_W:_RQ + (g + 1) * GROUP_W]
            k4 = proj_ref[rows, _RK + g * GROUP_W:_RK + (g + 1) * GROUP_W]
            v4 = proj_ref[rows, _RV + g * GROUP_W:_RV + (g + 1) * GROUP_W].astype(BF16)
            k_bd = pair_diag((k4 * QK_SCALE).astype(BF16))
            sc = _dot_nt(head_stack(q4).astype(BF16), k_bd) * dm_ref[g]
            qd_st = head_stack(q4 * qw_ref[:, cols])
            lhs = jnp.concatenate([sc.astype(BF16), qd_st.astype(BF16)], axis=1)
            s_old = s_ref[g]
            rhs = jnp.concatenate([pair_diag(v4), s_old.astype(BF16)], axis=0)
            r = _dot(lhs, rhs)
            oret_ref[rows, cols] = jnp.where(lo_g, r[:CHUNK], r[CHUNK:])
            kd = (k4 * kw_ref[:, cols]).astype(BF16)
            s_ref[g] = gbd_ref[g] * s_old + blk * _dot_tn(kd, v4)
        k_cur = proj_ref[rows, _SK:_SK + KV_W].astype(BF16)
        vt_cur = proj_ref[rows, _SV:_SV + KV_W].T.astype(BF16)
        kk = jnp.concatenate([k_prev, k_cur], axis=0)
        vvt = jnp.concatenate([vt_prev, vt_cur], axis=1)
        q_parts = []
        for p in range(PAIRS):
            q2 = proj_ref[rows, _SQ + p * LANES:_SQ + (p + 1) * LANES] * (QK_SCALE * LOG2E)
            zero = jnp.zeros_like(q2)
            q_parts += [jnp.where(lo, q2, zero).astype(BF16), jnp.where(lo, zero, q2).astype(BF16)]
        st = _dot_nt(kk, jnp.concatenate(q_parts, axis=0))
        if c == 0:
            st = st + bias_ref[jnp.minimum(step, 1)]
        else:
            st = st + bias_ref[1]
        sink = sink_ref[...] * LOG2E
        e_parts, den_parts = [], []
        for p in range(PAIRS):
            if budget > 0:
                emit_pieces(1)
                budget -= 1
            pc = slice(2 * p * CHUNK, 2 * (p + 1) * CHUNK)
            st_p, sink_p = st[:, pc], sink[:, pc]
            m = jnp.maximum(jnp.max(st_p, axis=0, keepdims=True), sink_p)
            e = jnp.exp2(st_p - m)
            den_parts.append(jnp.sum(e, axis=0, keepdims=True) + jnp.exp2(sink_p - m))
            e_parts.append(e.astype(BF16))
        if (r0 + CHUNK) % PROJ_ROWS == 0:
            emit_pieces(len(pending))
        den = jnp.concatenate(den_parts, axis=1)
        ot = _dot(vvt, jnp.concatenate(e_parts, axis=1)) * (1.0 / den)
        for p in range(PAIRS):
            c0 = 2 * p * CHUNK
            swat_ref[p * LANES:p * LANES + HEAD_DIM, rows] = ot[:HEAD_DIM, c0:c0 + CHUNK]
            swat_ref[p * LANES + HEAD_DIM:(p + 1) * LANES, rows] = ot[HEAD_DIM:, c0 + CHUNK:c0 + 2 * CHUNK]
        k_prev, vt_prev = k_cur, vt_cur
    kprev_ref[...] = k_prev
    vprev_ref[...] = vt_prev

    lane_t = lax.broadcasted_iota(jnp.int32, (tq, LANES), 1)
    lo_t = lane_t < HEAD_DIM
    for p in range(PAIRS):
        cols = slice(p * LANES, (p + 1) * LANES)
        o = oret_ref[:, cols]
        zero = jnp.zeros_like(o)
        s_lo = jnp.sum(jnp.where(lo_t, o, zero), axis=-1, keepdims=True)
        s_hi = jnp.sum(jnp.where(lo_t, zero, o), axis=-1, keepdims=True)
        d = o - jnp.where(lo_t, s_lo, s_hi) * (1.0 / HEAD_DIM)
        d2 = d * d
        v_lo = jnp.sum(jnp.where(lo_t, d2, zero), axis=-1, keepdims=True)
        v_hi = jnp.sum(jnp.where(lo_t, zero, d2), axis=-1, keepdims=True)
        var = jnp.where(lo_t, v_lo, v_hi) * (1.0 / HEAD_DIM)
        gn = d * lax.rsqrt(var + GN_EPS) * gnw_ref[:, cols]
        gate = proj_ref[:, _RG + p * LANES:_RG + (p + 1) * LANES]
        yret_ref[:, cols] = (gate * jax.nn.sigmoid(gate) * gn).astype(BF16)

    mix = (_dot(yret_ref[...], wout_ref[:RET_W, :])
           + _dot_tn(swat_ref[...].astype(BF16), wout_ref[RET_W:, :]))
    y_ref[0] = _layer_norm(DN_ALPHA * x + mix, lng_ref[...], lnb_ref[...])

    @pl.when(step == nsteps - 1)
    def _():
        for h in range(RET_HEADS):
            g, i = divmod(h, 4)
            blk_h = s_ref[g, i * HEAD_DIM:(i + 1) * HEAD_DIM, (i // 2) * LANES:(i // 2 + 1) * LANES]
            if i % 2:
                blk_h = pltpu.roll(blk_h, HEAD_DIM, 1)
            st_ref[0, h] = blk_h[:, :HEAD_DIM]
        ck_ref[0] = proj_ref[tq - WINDOW:tq, _SK:_SK + KV_W]
        cv_ref[0] = proj_ref[tq - WINDOW:tq, _SV:_SV + KV_W]


def _mixer_prompt(x, w_in_p, w_out_p, lng, lnb, gnw, sink_st, consts, *, tq):
    bsz, seq, _ = x.shape
    assert seq % tq == 0 and tq % CHUNK == 0
    dm_st, qw, kw, gbd, blk, bias = consts
    body = functools.partial(_mixer_prompt_body, tq=tq)
    return pl.pallas_call(
        body,
        grid=(bsz, seq // tq),
        in_specs=[
            pl.BlockSpec((1, tq, D_MODEL), lambda b, s: (b, s, 0)),
            _const_spec((D_MODEL, IN_COLS)),
            _const_spec((2 * RET_W, D_MODEL)),
            _const_spec((1, D_MODEL)),
            _const_spec((1, D_MODEL)),
            _const_spec((1, RET_W)),
            _const_spec((1, SWA_HEADS * CHUNK)),
            _const_spec(dm_st.shape),
            _const_spec(qw.shape),
            _const_spec(kw.shape),
            _const_spec(gbd.shape),
            _const_spec(blk.shape),
            _const_spec(bias.shape),
        ],
        out_specs=[
            pl.BlockSpec((1, tq, D_MODEL), lambda b, s: (b, s, 0)),
            pl.BlockSpec((1, RET_HEADS, HEAD_DIM, HEAD_DIM), lambda b, s: (b, 0, 0, 0)),
            pl.BlockSpec((1, WINDOW, KV_W), lambda b, s: (b, 0, 0)),
            pl.BlockSpec((1, WINDOW, KV_W), lambda b, s: (b, 0, 0)),
        ],
        out_shape=[
            jax.ShapeDtypeStruct((bsz, seq, D_MODEL), F32),
            jax.ShapeDtypeStruct((bsz, RET_HEADS, HEAD_DIM, HEAD_DIM), F32),
            jax.ShapeDtypeStruct((bsz, WINDOW, KV_W), F32),
            jax.ShapeDtypeStruct((bsz, WINDOW, KV_W), F32),
        ],
        scratch_shapes=[
            pltpu.VMEM((tq, IN_COLS), F32),
            pltpu.VMEM((tq, RET_W), F32),
            pltpu.VMEM((tq, RET_W), BF16),
            pltpu.VMEM((SWA_W, tq), F32),
            pltpu.VMEM((GROUPS, GROUP_W, GROUP_W), F32),
            pltpu.VMEM((CHUNK, KV_W), BF16),
            pltpu.VMEM((KV_W, CHUNK), BF16),
        ],
        compiler_params=pltpu.CompilerParams(
            dimension_semantics=("arbitrary", "arbitrary"), vmem_limit_bytes=VMEM_LIMIT_BYTES),
        name="mixer_prompt",
    )(x, w_in_p, w_out_p, lng, lnb, gnw, sink_st, dm_st, qw, kw, gbd, blk, bias)


def _sample_constants(t):
    lg = _log_gamma()
    hh = np.arange(RET_HEADS)
    tt = np.arange(t, dtype=np.float64)
    row_h = np.repeat(hh, t)
    row_t = np.tile(tt, RET_HEADS)
    col_h = np.repeat(hh, HEAD_DIM)
    same = (row_h[:, None] == col_h[None, :]).astype(np.float64)
    mask_q1 = same * QK_SCALE
    mask_qw = same * np.exp((row_t + 1.0) * lg[row_h])[:, None]
    mask_kw = same * (np.exp((t - 1.0 - row_t) * lg[row_h]) * QK_SCALE)[:, None]
    diff = row_t[:, None] - row_t[None, :]
    same_h = row_h[:, None] == row_h[None, :]
    dm = np.where(same_h & (diff >= 0), np.exp(np.where(diff >= 0, diff, 0.0) * lg[row_h][:, None]), 0.0)
    g_t = np.repeat(np.exp(t * lg), HEAD_DIM)[:, None] * np.ones((1, HEAD_DIM))
    slopes = _alibi_slopes()
    j = np.arange(WINDOW + t)
    dist = row_t[:, None] + WINDOW - j[None, :]
    valid = (dist >= 0) & (dist <= WINDOW)
    bias = np.where(valid, -slopes[row_h][:, None] * dist, NEG_BIG)
    f = lambda a: jnp.asarray(a, F32)
    return f(mask_q1), f(mask_qw), f(mask_kw), f(dm), f(g_t), f(bias)


def _mixer_sample_body(x_ref, st_ref, ck_ref, cv_ref,
                       wqk_ref, wv_ref, wg_ref, wsq_ref, wskv_ref, wor_ref, wos_ref,
                       lng_ref, lnb_ref, gnw_ref, sink_ref,
                       mq1_ref, mqw_ref, mkw_ref, dm_ref, gt_ref, bias_ref,
                       y_ref, sto_ref, cko_ref, cvo_ref,
                       qk_ref, vh_ref, qs_ref, kvn_ref, oh_ref, os_ref, *, sb, t):
    ht = RET_HEADS * t
    x = x_ref[...]
    xb = x.astype(BF16)
    qk_ref[...] = _dot(xb, wqk_ref[...])
    kvn_ref[...] = _dot(xb, wskv_ref[...])
    for h in range(RET_HEADS):
        vh_ref[h] = _dot(xb, wv_ref[h])
        qs_ref[h] = _dot(xb, wsq_ref[h]) * QK_SCALE

    mq1 = mq1_ref[...]
    mqw = mqw_ref[...]
    mkw = mkw_ref[...]
    dm = dm_ref[...]
    g_t = gt_ref[...]
    bias = bias_ref[...]
    sink = sink_ref[...]

    def per_seq(b, carry):
        r0 = pl.multiple_of(b * t, t)
        rows = pl.ds(r0, t)
        q_b = qk_ref[rows, 0:RET_W]
        k_b = qk_ref[rows, RET_W:2 * RET_W]
        q_rep = jnp.concatenate([q_b] * RET_HEADS, axis=0)
        k_rep = jnp.concatenate([k_b] * RET_HEADS, axis=0)
        v_st = vh_ref[:, rows, :].reshape(ht, HEAD_DIM).astype(BF16)
        s0 = st_ref[b]
        sc = _dot_nt((q_rep * mq1).astype(BF16), k_rep.astype(BF16)) * dm
        o = _dot((q_rep * mqw).astype(BF16), s0.astype(BF16)) + _dot(sc.astype(BF16), v_st)
        oh_ref[:, rows, :] = o.reshape(RET_HEADS, t, HEAD_DIM)
        sto_ref[b] = g_t * s0 + _dot_tn((k_rep * mkw).astype(BF16), v_st)
        k_new = kvn_ref[rows, 0:KV_W]
        v_new = kvn_ref[rows, KV_W:2 * KV_W]
        k_old = ck_ref[b]
        v_old = cv_ref[b]
        kk = jnp.concatenate([k_old, k_new], axis=0)
        vv = jnp.concatenate([v_old, v_new], axis=0)
        q_s = qs_ref[:, rows, :].reshape(ht, LANES)
        s = _dot_nt(q_s.astype(BF16), kk.astype(BF16)) + bias
        m = jnp.maximum(jnp.max(s, axis=-1, keepdims=True), sink)
        e = jnp.exp(s - m)
        den = jnp.sum(e, axis=-1, keepdims=True) + jnp.exp(sink - m)
        o_s = _dot(e.astype(BF16), vv.astype(BF16)) / den
        os_ref[:, rows, :] = o_s.reshape(SWA_HEADS, t, LANES)
        cko_ref[b] = kk[t:]
        cvo_ref[b] = vv[t:]
        return carry

    lax.fori_loop(0, sb, per_seq, 0, unroll=8)

    mix = None
    for h in range(RET_HEADS):
        o = oh_ref[h]
        mu = jnp.mean(o, axis=-1, keepdims=True)
        d = o - mu
        var = jnp.mean(d * d, axis=-1, keepdims=True)
        gn = d * lax.rsqrt(var + GN_EPS) * gnw_ref[h]
        gate = _dot(xb, wg_ref[h])
        y_h = (gate * jax.nn.sigmoid(gate) * gn).astype(BF16)
        part = _dot(y_h, wor_ref[h]) + _dot(os_ref[h].astype(BF16), wos_ref[h])
        mix = part if mix is None else mix + part
    y_ref[...] = _layer_norm(DN_ALPHA * x + mix, lng_ref[...], lnb_ref[...])


def _mixer_sample(x, state, ck, cv, weights, lng, lnb, gnw_h, sink_rows, consts, *, sb, t):
    rows_total = x.shape[0]
    nseq = rows_total // t
    assert nseq % sb == 0
    rows = sb * t
    wqk, wv, wg, wsq, wskv, wor, wos = weights
    body = functools.partial(_mixer_sample_body, sb=sb, t=t)
    return pl.pallas_call(
        body,
        grid=(nseq // sb,),
        in_specs=[
            pl.BlockSpec((rows, D_MODEL), lambda i: (i, 0)),
            pl.BlockSpec((sb, RET_W, HEAD_DIM), lambda i: (i, 0, 0)),
            pl.BlockSpec((sb, WINDOW, KV_W), lambda i: (i, 0, 0)),
            pl.BlockSpec((sb, WINDOW, KV_W), lambda i: (i, 0, 0)),
        ] + [_const_spec(w.shape) for w in weights] + [
            _const_spec(lng.shape), _const_spec(lnb.shape), _const_spec(gnw_h.shape),
            _const_spec(sink_rows.shape),
        ] + [_const_spec(c.shape) for c in consts],
        out_specs=[
            pl.BlockSpec((rows, D_MODEL), lambda i: (i, 0)),
            pl.BlockSpec((sb, RET_W, HEAD_DIM), lambda i: (i, 0, 0)),
            pl.BlockSpec((sb, WINDOW, KV_W), lambda i: (i, 0, 0)),
            pl.BlockSpec((sb, WINDOW, KV_W), lambda i: (i, 0, 0)),
        ],
        out_shape=[
            jax.ShapeDtypeStruct((rows_total, D_MODEL), F32),
            jax.ShapeDtypeStruct((nseq, RET_W, HEAD_DIM), F32),
            jax.ShapeDtypeStruct((nseq, WINDOW, KV_W), F32),
            jax.ShapeDtypeStruct((nseq, WINDOW, KV_W), F32),
        ],
        scratch_shapes=[
            pltpu.VMEM((rows, 2 * RET_W), F32),
            pltpu.VMEM((RET_HEADS, rows, HEAD_DIM), F32),
            pltpu.VMEM((SWA_HEADS, rows, LANES), F32),
            pltpu.VMEM((rows, 2 * KV_W), F32),
            pltpu.VMEM((RET_HEADS, rows, HEAD_DIM), F32),
            pltpu.VMEM((SWA_HEADS, rows, LANES), F32),
        ],
        compiler_params=pltpu.CompilerParams(
            dimension_semantics=("arbitrary",), vmem_limit_bytes=VMEM_LIMIT_BYTES),
        name="mixer_sample",
    )(x, state, ck, cv, *weights, lng, lnb, gnw_h, sink_rows, *consts)


def _prep_prompt_weights(w_in, w_out, sinks):
    sq = w_in[:, _SQ:_SQ + SWA_W].reshape(D_MODEL, SWA_HEADS, HEAD_DIM)
    sq = sq[:, jnp.asarray(_SWA_HEAD_ORDER), :].reshape(D_MODEL, SWA_W)
    w_in_p = jnp.concatenate([w_in[:, :_SQ], sq, w_in[:, _SK:]], axis=1).astype(BF16)
    wo_s = w_out[RET_W:].reshape(SWA_HEADS, HEAD_DIM, D_MODEL)
    wo_s = wo_s[jnp.asarray(_SWA_HEAD_ORDER)].reshape(SWA_W, D_MODEL)
    w_out_p = jnp.concatenate([w_out[:RET_W], wo_s], axis=0).astype(BF16)
    sink_st = jnp.repeat(sinks.astype(F32)[jnp.asarray(_SWA_HEAD_ORDER)], CHUNK)[None, :]
    return w_in_p, w_out_p, sink_st


def _prep_sample_weights(w_in, w_out, sinks, gnw, t):
    wqk = w_in[:, _RQ:_RQ + 2 * RET_W].astype(BF16)
    per_head = lambda w: jnp.transpose(w.reshape(D_MODEL, RET_HEADS, HEAD_DIM), (1, 0, 2))
    wv = per_head(w_in[:, _RV:_RV + RET_W]).astype(BF16)
    wg = per_head(w_in[:, _RG:_RG + RET_W]).astype(BF16)
    wsq_h = per_head(w_in[:, _SQ:_SQ + SWA_W])
    group = (jnp.arange(SWA_HEADS) // SWA_GROUP)[:, None, None, None]
    slot = jnp.arange(SWA_KV_HEADS)[None, None, :, None]
    wsq = jnp.where(group == slot, wsq_h[:, :, None, :], 0.0).reshape(SWA_HEADS, D_MODEL, KV_W).astype(BF16)
    wskv = w_in[:, _SK:].astype(BF16)
    wor = w_out[:RET_W].reshape(RET_HEADS, HEAD_DIM, D_MODEL).astype(BF16)
    wos_h = w_out[RET_W:].reshape(SWA_HEADS, 1, HEAD_DIM, D_MODEL)
    slot_r = jnp.arange(SWA_KV_HEADS)[None, :, None, None]
    wos = jnp.where(group == slot_r, wos_h, 0.0).reshape(SWA_HEADS, KV_W, D_MODEL).astype(BF16)
    gnw_h = gnw.astype(F32).reshape(RET_HEADS, 1, HEAD_DIM)
    sink_rows = jnp.repeat(sinks.astype(F32), t)[:, None]
    return (wqk, wv, wg, wsq, wskv, wor, wos), gnw_h, sink_rows


def kernel(x_prompt, x_sample, state_ret, cache_swa_k, cache_swa_v, ln_gain, ln_bias, w_in,
           ret_gn_w, swa_sinks, w_out, ffn1_gate, ffn1_up, ffn1_down, ffn2_gate, ffn2_up, ffn2_down):
    assert ln_gain.shape[0] == DEPTH == 1
    bsz, seq, _ = x_prompt.shape
    nseq, t, _ = x_sample.shape
    lng = ln_gain[0].astype(F32).reshape(3, 1, D_MODEL)
    lnb = ln_bias[0].astype(F32).reshape(3, 1, D_MODEL)
    f1 = (ffn1_gate[0].astype(BF16), ffn1_up[0].astype(BF16), ffn1_down[0].astype(BF16))
    f2 = (ffn2_gate[0].astype(BF16), ffn2_up[0].astype(BF16), ffn2_down[0].astype(BF16))
    gnw = ret_gn_w[0].astype(F32)

    tm_p = 512
    tm_s = min(512, nseq * t)
    xp = x_prompt.reshape(bsz * seq, D_MODEL)
    xs = x_sample.reshape(nseq * t, D_MODEL)

    xp = _ffn_ln(xp, *f1, lng[0], lnb[0], tm=tm_p)
    w_in_p, w_out_p, sink_st = _prep_prompt_weights(w_in[0], w_out[0], swa_sinks[0])
    yp, st_p, ck_p, cv_p = _mixer_prompt(
        xp.reshape(bsz, seq, D_MODEL), w_in_p, w_out_p, lng[1], lnb[1], gnw.reshape(1, RET_W),
        sink_st, _prompt_constants(), tq=min(512, seq))
    yp = _ffn_ln(yp.reshape(bsz * seq, D_MODEL), *f2, lng[2], lnb[2], tm=tm_p)

    xs = _ffn_ln(xs, *f1, lng[0], lnb[0], tm=tm_s)
    weights, gnw_h, sink_rows = _prep_sample_weights(w_in[0], w_out[0], swa_sinks[0], gnw, t)
    ys, st_s, ck_s, cv_s = _mixer_sample(
        xs, state_ret[0].astype(F32).reshape(nseq, RET_W, HEAD_DIM),
        cache_swa_k[0].reshape(nseq, WINDOW, KV_W), cache_swa_v[0].reshape(nseq, WINDOW, KV_W),
        weights, lng[1], lnb[1], gnw_h, sink_rows, _sample_constants(t), sb=min(16, nseq), t=t)
    ys = _ffn_ln(ys, *f2, lng[2], lnb[2], tm=tm_s)

    kv_shape = lambda n: (1, n, WINDOW, SWA_KV_HEADS, HEAD_DIM)
    return (yp.reshape(bsz, seq, D_MODEL), ys.reshape(nseq, t, D_MODEL),
            st_p[None], ck_p.reshape(kv_shape(bsz)), cv_p.reshape(kv_shape(bsz)),
            st_s.reshape(1, nseq, RET_HEADS, HEAD_DIM, HEAD_DIM),
            ck_s.reshape(kv_shape(nseq)), cv_s.reshape(kv_shape(nseq)))
```

```python
import functools

import numpy as np
import jax
import jax.numpy as jnp
from jax import lax
from jax.experimental import pallas as pl
from jax.experimental.pallas import tpu as pltpu

F32 = jnp.float32
BF16 = jnp.bfloat16

D_MODEL = 1024
HEAD_DIM = 64
RET_HEADS = 8
SWA_HEADS = 8
SWA_KV_HEADS = 2
SWA_GROUP = SWA_HEADS // SWA_KV_HEADS
WINDOW = 128
CHUNK = 128
D_FF = 2816
RET_W = RET_HEADS * HEAD_DIM
SWA_W = SWA_HEADS * HEAD_DIM
KV_W = SWA_KV_HEADS * HEAD_DIM
IN_COLS = 4 * RET_W + SWA_W + 2 * KV_W
LN_EPS = 1e-5
GN_EPS = 1e-5
DEPTH = 1
DN_ALPHA = (2.0 * DEPTH) ** 0.25
QK_SCALE = HEAD_DIM ** -0.5
NEG_BIG = -1e30
LOG2E = 1.4426950408889634

LANES = 128
PAIRS = RET_HEADS // 2
GROUPS = RET_HEADS // 4
GROUP_W = 4 * HEAD_DIM
PROJ_ROWS = 256
PROJ_COLS = 512
VMEM_LIMIT_BYTES = 56 * 1024 * 1024

_RQ, _RK, _RV, _RG, _SQ = 0, RET_W, 2 * RET_W, 3 * RET_W, 4 * RET_W
_SK = 4 * RET_W + SWA_W
_SV = _SK + KV_W

_SWA_PAIR_HEADS = [(p, SWA_GROUP + p) for p in range(PAIRS)]
_SWA_HEAD_ORDER = [h for pair in _SWA_PAIR_HEADS for h in pair]

_FF_CHUNKS = ((0, 1536), (1536, 1280))


def _log_gamma():
    h = np.arange(RET_HEADS, dtype=np.float64)
    return np.log1p(-np.exp2(-5.0 - h))


def _alibi_slopes():
    return np.exp2(-8.0 / SWA_HEADS * np.arange(1, SWA_HEADS + 1, dtype=np.float64))


def _layer_norm(z, g, b):
    mu = jnp.mean(z, axis=-1, keepdims=True)
    d = z - mu
    var = jnp.mean(d * d, axis=-1, keepdims=True)
    return d * lax.rsqrt(var + LN_EPS) * g + b


def _dot(a, b):
    return jnp.dot(a, b, preferred_element_type=F32)


def _dot_nt(a, b):
    return lax.dot_general(a, b, (((1,), (1,)), ((), ())), preferred_element_type=F32)


def _dot_tn(a, b):
    return lax.dot_general(a, b, (((0,), (0,)), ((), ())), preferred_element_type=F32)


def _ffn_ln_body(x_ref, wg_ref, wu_ref, wd_ref, g_ref, b_ref, o_ref):
    x = x_ref[...]
    xb = x.astype(BF16)
    acc = None
    for c0, cw in _FF_CHUNKS:
        gate = _dot(xb, wg_ref[:, c0:c0 + cw])
        up = _dot(xb, wu_ref[:, c0:c0 + cw])
        hid = (gate * jax.nn.sigmoid(gate) * up).astype(BF16)
        part = _dot(hid, wd_ref[c0:c0 + cw, :])
        acc = part if acc is None else acc + part
    z = DN_ALPHA * x + 0.5 * acc
    o_ref[...] = _layer_norm(z, g_ref[...], b_ref[...])


def _const_spec(shape):
    nd = len(shape)
    return pl.BlockSpec(shape, lambda *_: (0,) * nd, pipeline_mode=pl.Buffered(1))


def _ffn_ln(x, wg, wu, wd, g, b, *, tm):
    rows = x.shape[0]
    assert rows % tm == 0
    return pl.pallas_call(
        _ffn_ln_body,
        grid=(rows // tm,),
        in_specs=[
            pl.BlockSpec((tm, D_MODEL), lambda i: (i, 0)),
            _const_spec((D_MODEL, D_FF)),
            _const_spec((D_MODEL, D_FF)),
            _const_spec((D_FF, D_MODEL)),
            _const_spec((1, D_MODEL)),
            _const_spec((1, D_MODEL)),
        ],
        out_specs=pl.BlockSpec((tm, D_MODEL), lambda i: (i, 0)),
        out_shape=jax.ShapeDtypeStruct((rows, D_MODEL), F32),
        compiler_params=pltpu.CompilerParams(
            dimension_semantics=("arbitrary",), vmem_limit_bytes=VMEM_LIMIT_BYTES),
        name="ffn_ln",
    )(x, wg, wu, wd, g, b)


def _prompt_constants():
    lg = _log_gamma()
    idx = np.arange(CHUNK, dtype=np.float64)
    diff = idx[:, None] - idx[None, :]
    dm = np.where(diff >= 0, np.exp(np.where(diff >= 0, diff, 0.0)[None] * lg[:, None, None]), 0.0)
    dm_st = np.stack([np.block([[dm[4 * g], dm[4 * g + 2]], [dm[4 * g + 1], dm[4 * g + 3]]])
                      for g in range(GROUPS)])
    qw = np.repeat(np.exp((idx + 1.0)[:, None] * lg[None, :]), HEAD_DIM, axis=1)
    kw = np.repeat(np.exp((CHUNK - 1.0 - idx)[:, None] * lg[None, :]), HEAD_DIM, axis=1) * QK_SCALE
    gc = np.exp(CHUNK * lg)
    blk = np.kron(np.eye(4), np.ones((HEAD_DIM, HEAD_DIM)))
    gbd = np.stack([blk * np.repeat(gc[4 * g:4 * g + 4], HEAD_DIM)[:, None] for g in range(GROUPS)])
    slopes = _alibi_slopes()
    i = np.arange(CHUNK)
    j = np.arange(2 * CHUNK)
    dist = i[:, None] + CHUNK - j[None, :]
    within = (dist >= 0) & (dist <= WINDOW)
    valid = [within & (j >= CHUNK)[None, :], within]
    bias = np.stack([np.concatenate([np.where(valid[v], -slopes[h] * dist * LOG2E, NEG_BIG).T
                                     for h in _SWA_HEAD_ORDER], axis=1) for v in range(2)])
    f = lambda a: jnp.asarray(a, F32)
    return f(dm_st), f(qw), f(kw), f(gbd), f(blk), f(bias)


def _mixer_prompt_body(x_ref, xn_ref, win_ref, wout_ref, lng_ref, lnb_ref, gnw_ref, sink_ref,
                       dm_ref, qw_ref, kw_ref, gbd_ref, blk_ref, bias_ref,
                       y_ref, st_ref, ck_ref, cv_ref,
                       pb_ref, proj1_ref, oret_ref, yret_ref, swat_ref, s_ref, kprev_ref, vprev_ref,
                       *, tq):
    step = pl.program_id(1)
    nsteps = pl.num_programs(1)
    flat = pl.program_id(0) * nsteps + step
    slot = lax.rem(flat, 2)
    nchunks = tq // CHUNK
    nblocks = tq // PROJ_ROWS
    col_starts = range(0, IN_COLS, PROJ_COLS)

    @pl.when(step == 0)
    def _():
        s_ref[...] = jnp.zeros_like(s_ref)
        kprev_ref[...] = jnp.zeros_like(kprev_ref)
        vprev_ref[...] = jnp.zeros_like(vprev_ref)

    x = x_ref[0]

    def project_piece(dst, xrows, c0):
        c1 = min(c0 + PROJ_COLS, IN_COLS)
        dst[:, c0:c1] = _dot(xrows.astype(BF16), win_ref[:, c0:c1])

    @pl.when(flat == 0)
    def _():
        for c0 in col_starts:
            project_piece(pb_ref.at[0], x[:PROJ_ROWS], c0)

    def proj(r0, nrows, c0, c1):
        if r0 < PROJ_ROWS:
            return pb_ref[slot, r0:r0 + nrows, c0:c1]
        return proj1_ref[r0 - PROJ_ROWS:r0 - PROJ_ROWS + nrows, c0:c1]

    pending = []

    def emit_pieces(n):
        for _ in range(min(n, len(pending))):
            pending.pop(0)()

    pieces_per_chunk = pl.cdiv(len(col_starts), PROJ_ROWS // CHUNK)

    lane = lax.broadcasted_iota(jnp.int32, (CHUNK, LANES), 1)
    lo = lane < HEAD_DIM
    lane_g = lax.broadcasted_iota(jnp.int32, (CHUNK, GROUP_W), 1)
    lo_g = jnp.bitwise_and(lane_g, LANES - 1) < HEAD_DIM
    blk = blk_ref[...]
    zero_b = jnp.zeros((CHUNK, LANES), BF16)

    def pair_diag(a):
        return jnp.concatenate([jnp.concatenate([a[:, :LANES], zero_b], axis=1),
                                jnp.concatenate([zero_b, a[:, LANES:]], axis=1)], axis=0)

    def head_stack(a):
        zero = jnp.zeros_like(a)
        return jnp.concatenate([jnp.where(lo_g, a, zero), jnp.where(lo_g, zero, a)], axis=0)

    lane_t = lax.broadcasted_iota(jnp.int32, (PROJ_ROWS, LANES), 1)
    lo_t = lane_t < HEAD_DIM

    def finish_block(kb):
        rb = slice(kb * PROJ_ROWS, (kb + 1) * PROJ_ROWS)
        for p in range(PAIRS):
            cols = slice(p * LANES, (p + 1) * LANES)
            o = oret_ref[rb, cols]
            zero = jnp.zeros_like(o)
            s_lo = jnp.sum(jnp.where(lo_t, o, zero), axis=-1, keepdims=True)
            s_hi = jnp.sum(jnp.where(lo_t, zero, o), axis=-1, keepdims=True)
            d = o - jnp.where(lo_t, s_lo, s_hi) * (1.0 / HEAD_DIM)
            d2 = d * d
            v_lo = jnp.sum(jnp.where(lo_t, d2, zero), axis=-1, keepdims=True)
            v_hi = jnp.sum(jnp.where(lo_t, zero, d2), axis=-1, keepdims=True)
            var = jnp.where(lo_t, v_lo, v_hi) * (1.0 / HEAD_DIM)
            gn = d * lax.rsqrt(var + GN_EPS) * gnw_ref[:, cols]
            gate = proj(kb * PROJ_ROWS, PROJ_ROWS, _RG + p * LANES, _RG + (p + 1) * LANES)
            yret_ref[rb, cols] = (gate * jax.nn.sigmoid(gate) * gn).astype(BF16)
        mix = (_dot(yret_ref[rb, :], wout_ref[:RET_W, :])
               + _dot_tn(swat_ref[:, rb].astype(BF16), wout_ref[RET_W:, :]))
        y_ref[0, rb, :] = _layer_norm(DN_ALPHA * x[rb] + mix, lng_ref[...], lnb_ref[...])

    k_prev = kprev_ref[...]
    vt_prev = vprev_ref[...]
    for c in range(nchunks):
        r0 = c * CHUNK
        rows = slice(r0, r0 + CHUNK)
        if r0 % PROJ_ROWS == 0:
            nb = r0 // PROJ_ROWS + 1
            if nb < nblocks:
                dst = proj1_ref.at[(nb - 1) * PROJ_ROWS:nb * PROJ_ROWS]
                xrows = x[nb * PROJ_ROWS:(nb + 1) * PROJ_ROWS]
            else:
                dst, xrows = pb_ref.at[1 - slot], xn_ref[0]
            pending.extend(functools.partial(project_piece, dst, xrows, c0) for c0 in col_starts)
        budget = pieces_per_chunk
        for g in range(GROUPS):
            cols = slice(g * GROUP_W, (g + 1) * GROUP_W)
            q4 = proj(r0, CHUNK, _RQ + g * GROUP_W, _RQ + (g + 1) * GROUP_W)
            k4 = proj(r0, CHUNK, _RK + g * GROUP_W, _RK + (g + 1) * GROUP_W)
            v4 = proj(r0, CHUNK, _RV + g * GROUP_W, _RV + (g + 1) * GROUP_W).astype(BF16)
            k_bd = pair_diag((k4 * QK_SCALE).astype(BF16))
            sc = _dot_nt(head_stack(q4).astype(BF16), k_bd) * dm_ref[g]
            qd_st = head_stack(q4 * qw_ref[:, cols])
            lhs = jnp.concatenate([sc.astype(BF16), qd_st.astype(BF16)], axis=1)
            s_old = s_ref[g]
            rhs = jnp.concatenate([pair_diag(v4), s_old.astype(BF16)], axis=0)
            r = _dot(lhs, rhs)
            oret_ref[rows, cols] = jnp.where(lo_g, r[:CHUNK], r[CHUNK:])
            kd = (k4 * kw_ref[:, cols]).astype(BF16)
            s_ref[g] = gbd_ref[g] * s_old + blk * _dot_tn(kd, v4)
        k_cur = proj(r0, CHUNK, _SK, _SK + KV_W).astype(BF16)
        vt_cur = proj(r0, CHUNK, _SV, _SV + KV_W).T.astype(BF16)
        kk = jnp.concatenate([k_prev, k_cur], axis=0)
        vvt = jnp.concatenate([vt_prev, vt_cur], axis=1)
        q_parts = []
        for p in range(PAIRS):
            q2 = proj(r0, CHUNK, _SQ + p * LANES, _SQ + (p + 1) * LANES) * (QK_SCALE * LOG2E)
            zero = jnp.zeros_like(q2)
            q_parts += [jnp.where(lo, q2, zero).astype(BF16), jnp.where(lo, zero, q2).astype(BF16)]
        st = _dot_nt(kk, jnp.concatenate(q_parts, axis=0))
        if c == 0:
            st = st + bias_ref[jnp.minimum(step, 1)]
        else:
            st = st + bias_ref[1]
        sink = sink_ref[...] * LOG2E
        e_parts, den_parts = [], []
        for p in range(PAIRS):
            if budget > 0:
                emit_pieces(1)
                budget -= 1
            pc = slice(2 * p * CHUNK, 2 * (p + 1) * CHUNK)
            st_p, sink_p = st[:, pc], sink[:, pc]
            m = jnp.maximum(jnp.max(st_p, axis=0, keepdims=True), sink_p)
            e = jnp.exp2(st_p - m)
            den_parts.append(jnp.sum(e, axis=0, keepdims=True) + jnp.exp2(sink_p - m))
            e_parts.append(e.astype(BF16))
        if (r0 + CHUNK) % PROJ_ROWS == 0:
            emit_pieces(len(pending))
        den = jnp.concatenate(den_parts, axis=1)
        ot = _dot(vvt, jnp.concatenate(e_parts, axis=1)) * (1.0 / den)
        for p in range(PAIRS):
            c0 = 2 * p * CHUNK
            swat_ref[p * LANES:p * LANES + HEAD_DIM, rows] = ot[:HEAD_DIM, c0:c0 + CHUNK]
            swat_ref[p * LANES + HEAD_DIM:(p + 1) * LANES, rows] = ot[HEAD_DIM:, c0 + CHUNK:c0 + 2 * CHUNK]
        k_prev, vt_prev = k_cur, vt_cur
        if (r0 + CHUNK) % PROJ_ROWS == 0:
            finish_block(r0 // PROJ_ROWS)
    kprev_ref[...] = k_prev
    vprev_ref[...] = vt_prev

    @pl.when(step == nsteps - 1)
    def _():
        for h in range(RET_HEADS):
            g, i = divmod(h, 4)
            blk_h = s_ref[g, i * HEAD_DIM:(i + 1) * HEAD_DIM, (i // 2) * LANES:(i // 2 + 1) * LANES]
            if i % 2:
                blk_h = pltpu.roll(blk_h, HEAD_DIM, 1)
            st_ref[0, h] = blk_h[:, :HEAD_DIM]
        ck_ref[0] = proj(tq - WINDOW, WINDOW, _SK, _SK + KV_W)
        cv_ref[0] = proj(tq - WINDOW, WINDOW, _SV, _SV + KV_W)


def _mixer_prompt(x, w_in_p, w_out_p, lng, lnb, gnw, sink_st, consts, *, tq):
    bsz, seq, _ = x.shape
    assert seq % tq == 0 and tq % PROJ_ROWS == 0 and PROJ_ROWS % CHUNK == 0
    dm_st, qw, kw, gbd, blk, bias = consts
    body = functools.partial(_mixer_prompt_body, tq=tq)
    nsteps = seq // tq
    blocks_per_tile = tq // PROJ_ROWS

    def next_tile_head(b, s):
        nxt = jnp.minimum(b * nsteps + s + 1, bsz * nsteps - 1)
        return nxt // nsteps, (nxt % nsteps) * blocks_per_tile, 0

    return pl.pallas_call(
        body,
        grid=(bsz, nsteps),
        in_specs=[
            pl.BlockSpec((1, tq, D_MODEL), lambda b, s: (b, s, 0)),
            pl.BlockSpec((1, PROJ_ROWS, D_MODEL), next_tile_head),
            _const_spec((D_MODEL, IN_COLS)),
            _const_spec((2 * RET_W, D_MODEL)),
            _const_spec((1, D_MODEL)),
            _const_spec((1, D_MODEL)),
            _const_spec((1, RET_W)),
            _const_spec((1, SWA_HEADS * CHUNK)),
            _const_spec(dm_st.shape),
            _const_spec(qw.shape),
            _const_spec(kw.shape),
            _const_spec(gbd.shape),
            _const_spec(blk.shape),
            _const_spec(bias.shape),
        ],
        out_specs=[
            pl.BlockSpec((1, tq, D_MODEL), lambda b, s: (b, s, 0)),
            pl.BlockSpec((1, RET_HEADS, HEAD_DIM, HEAD_DIM), lambda b, s: (b, 0, 0, 0)),
            pl.BlockSpec((1, WINDOW, KV_W), lambda b, s: (b, 0, 0)),
            pl.BlockSpec((1, WINDOW, KV_W), lambda b, s: (b, 0, 0)),
        ],
        out_shape=[
            jax.ShapeDtypeStruct((bsz, seq, D_MODEL), F32),
            jax.ShapeDtypeStruct((bsz, RET_HEADS, HEAD_DIM, HEAD_DIM), F32),
            jax.ShapeDtypeStruct((bsz, WINDOW, KV_W), F32),
            jax.ShapeDtypeStruct((bsz, WINDOW, KV_W), F32),
        ],
        scratch_shapes=[
            pltpu.VMEM((2, PROJ_ROWS, IN_COLS), F32),
            pltpu.VMEM((tq - PROJ_ROWS, IN_COLS), F32),
            pltpu.VMEM((tq, RET_W), F32),
            pltpu.VMEM((tq, RET_W), BF16),
            pltpu.VMEM((SWA_W, tq), F32),
            pltpu.VMEM((GROUPS, GROUP_W, GROUP_W), F32),
            pltpu.VMEM((CHUNK, KV_W), BF16),
            pltpu.VMEM((KV_W, CHUNK), BF16),
        ],
        compiler_params=pltpu.CompilerParams(
            dimension_semantics=("arbitrary", "arbitrary"), vmem_limit_bytes=VMEM_LIMIT_BYTES),
        name="mixer_prompt",
    )(x, x, w_in_p, w_out_p, lng, lnb, gnw, sink_st, dm_st, qw, kw, gbd, blk, bias)


def _sample_constants(t):
    lg = _log_gamma()
    hh = np.arange(RET_HEADS)
    tt = np.arange(t, dtype=np.float64)
    row_h = np.repeat(hh, t)
    row_t = np.tile(tt, RET_HEADS)
    col_h = np.repeat(hh, HEAD_DIM)
    same = (row_h[:, None] == col_h[None, :]).astype(np.float64)
    mask_q1 = same * QK_SCALE
    mask_qw = same * np.exp((row_t + 1.0) * lg[row_h])[:, None]
    mask_kw = same * (np.exp((t - 1.0 - row_t) * lg[row_h]) * QK_SCALE)[:, None]
    diff = row_t[:, None] - row_t[None, :]
    same_h = row_h[:, None] == row_h[None, :]
    dm = np.where(same_h & (diff >= 0), np.exp(np.where(diff >= 0, diff, 0.0) * lg[row_h][:, None]), 0.0)
    g_t = np.repeat(np.exp(t * lg), HEAD_DIM)[:, None] * np.ones((1, HEAD_DIM))
    slopes = _alibi_slopes()
    j = np.arange(WINDOW + t)
    dist = row_t[:, None] + WINDOW - j[None, :]
    valid = (dist >= 0) & (dist <= WINDOW)
    bias = np.where(valid, -slopes[row_h][:, None] * dist, NEG_BIG)
    f = lambda a: jnp.asarray(a, F32)
    return f(mask_q1), f(mask_qw), f(mask_kw), f(dm), f(g_t), f(bias)


def _mixer_sample_body(x_ref, st_ref, ck_ref, cv_ref,
                       wqk_ref, wv_ref, wg_ref, wsq_ref, wskv_ref, wor_ref, wos_ref,
                       lng_ref, lnb_ref, gnw_ref, sink_ref,
                       mq1_ref, mqw_ref, mkw_ref, dm_ref, gt_ref, bias_ref,
                       y_ref, sto_ref, cko_ref, cvo_ref,
                       qk_ref, vh_ref, qs_ref, kvn_ref, oh_ref, os_ref, *, sb, t):
    ht = RET_HEADS * t
    x = x_ref[...]
    xb = x.astype(BF16)
    qk_ref[...] = _dot(xb, wqk_ref[...])
    kvn_ref[...] = _dot(xb, wskv_ref[...])
    for h in range(RET_HEADS):
        vh_ref[h] = _dot(xb, wv_ref[h])
        qs_ref[h] = _dot(xb, wsq_ref[h]) * QK_SCALE

    mq1 = mq1_ref[...]
    mqw = mqw_ref[...]
    mkw = mkw_ref[...]
    dm = dm_ref[...]
    g_t = gt_ref[...]
    bias = bias_ref[...]
    sink = sink_ref[...]

    def per_seq(b, carry):
        r0 = pl.multiple_of(b * t, t)
        rows = pl.ds(r0, t)
        q_b = qk_ref[rows, 0:RET_W]
        k_b = qk_ref[rows, RET_W:2 * RET_W]
        q_rep = jnp.concatenate([q_b] * RET_HEADS, axis=0)
        k_rep = jnp.concatenate([k_b] * RET_HEADS, axis=0)
        v_st = vh_ref[:, rows, :].reshape(ht, HEAD_DIM).astype(BF16)
        s0 = st_ref[b]
        sc = _dot_nt((q_rep * mq1).astype(BF16), k_rep.astype(BF16)) * dm
        o = _dot((q_rep * mqw).astype(BF16), s0.astype(BF16)) + _dot(sc.astype(BF16), v_st)
        oh_ref[:, rows, :] = o.reshape(RET_HEADS, t, HEAD_DIM)
        sto_ref[b] = g_t * s0 + _dot_tn((k_rep * mkw).astype(BF16), v_st)
        k_new = kvn_ref[rows, 0:KV_W]
        v_new = kvn_ref[rows, KV_W:2 * KV_W]
        k_old = ck_ref[b]
        v_old = cv_ref[b]
        kk = jnp.concatenate([k_old, k_new], axis=0)
        vv = jnp.concatenate([v_old, v_new], axis=0)
        q_s = qs_ref[:, rows, :].reshape(ht, LANES)
        s = _dot_nt(q_s.astype(BF16), kk.astype(BF16)) + bias
        m = jnp.maximum(jnp.max(s, axis=-1, keepdims=True), sink)
        e = jnp.exp(s - m)
        den = jnp.sum(e, axis=-1, keepdims=True) + jnp.exp(sink - m)
        o_s = _dot(e.astype(BF16), vv.astype(BF16)) / den
        os_ref[:, rows, :] = o_s.reshape(SWA_HEADS, t, LANES)
        cko_ref[b] = kk[t:]
        cvo_ref[b] = vv[t:]
        return carry

    lax.fori_loop(0, sb, per_seq, 0, unroll=8)

    mix = None
    for h in range(RET_HEADS):
        o = oh_ref[h]
        mu = jnp.mean(o, axis=-1, keepdims=True)
        d = o - mu
        var = jnp.mean(d * d, axis=-1, keepdims=True)
        gn = d * lax.rsqrt(var + GN_EPS) * gnw_ref[h]
        gate = _dot(xb, wg_ref[h])
        y_h = (gate * jax.nn.sigmoid(gate) * gn).astype(BF16)
        part = _dot(y_h, wor_ref[h]) + _dot(os_ref[h].astype(BF16), wos_ref[h])
        mix = part if mix is None else mix + part
    y_ref[...] = _layer_norm(DN_ALPHA * x + mix, lng_ref[...], lnb_ref[...])


def _mixer_sample(x, state, ck, cv, weights, lng, lnb, gnw_h, sink_rows, consts, *, sb, t):
    rows_total = x.shape[0]
    nseq = rows_total // t
    assert nseq % sb == 0
    rows = sb * t
    wqk, wv, wg, wsq, wskv, wor, wos = weights
    body = functools.partial(_mixer_sample_body, sb=sb, t=t)
    return pl.pallas_call(
        body,
        grid=(nseq // sb,),
        in_specs=[
            pl.BlockSpec((rows, D_MODEL), lambda i: (i, 0)),
            pl.BlockSpec((sb, RET_W, HEAD_DIM), lambda i: (i, 0, 0)),
            pl.BlockSpec((sb, WINDOW, KV_W), lambda i: (i, 0, 0)),
            pl.BlockSpec((sb, WINDOW, KV_W), lambda i: (i, 0, 0)),
        ] + [_const_spec(w.shape) for w in weights] + [
            _const_spec(lng.shape), _const_spec(lnb.shape), _const_spec(gnw_h.shape),
            _const_spec(sink_rows.shape),
        ] + [_const_spec(c.shape) for c in consts],
        out_specs=[
            pl.BlockSpec((rows, D_MODEL), lambda i: (i, 0)),
            pl.BlockSpec((sb, RET_W, HEAD_DIM), lambda i: (i, 0, 0)),
            pl.BlockSpec((sb, WINDOW, KV_W), lambda i: (i, 0, 0)),
            pl.BlockSpec((sb, WINDOW, KV_W), lambda i: (i, 0, 0)),
        ],
        out_shape=[
            jax.ShapeDtypeStruct((rows_total, D_MODEL), F32),
            jax.ShapeDtypeStruct((nseq, RET_W, HEAD_DIM), F32),
            jax.ShapeDtypeStruct((nseq, WINDOW, KV_W), F32),
            jax.ShapeDtypeStruct((nseq, WINDOW, KV_W), F32),
        ],
        scratch_shapes=[
            pltpu.VMEM((rows, 2 * RET_W), F32),
            pltpu.VMEM((RET_HEADS, rows, HEAD_DIM), F32),
            pltpu.VMEM((SWA_HEADS, rows, LANES), F32),
            pltpu.VMEM((rows, 2 * KV_W), F32),
            pltpu.VMEM((RET_HEADS, rows, HEAD_DIM), F32),
            pltpu.VMEM((SWA_HEADS, rows, LANES), F32),
        ],
        compiler_params=pltpu.CompilerParams(
            dimension_semantics=("arbitrary",), vmem_limit_bytes=VMEM_LIMIT_BYTES),
        name="mixer_sample",
    )(x, state, ck, cv, *weights, lng, lnb, gnw_h, sink_rows, *consts)


def _prep_prompt_weights(w_in, w_out, sinks):
    sq = w_in[:, _SQ:_SQ + SWA_W].reshape(D_MODEL, SWA_HEADS, HEAD_DIM)
    sq = sq[:, jnp.asarray(_SWA_HEAD_ORDER), :].reshape(D_MODEL, SWA_W)
    w_in_p = jnp.concatenate([w_in[:, :_SQ], sq, w_in[:, _SK:]], axis=1).astype(BF16)
    wo_s = w_out[RET_W:].reshape(SWA_HEADS, HEAD_DIM, D_MODEL)
    wo_s = wo_s[jnp.asarray(_SWA_HEAD_ORDER)].reshape(SWA_W, D_MODEL)
    w_out_p = jnp.concatenate([w_out[:RET_W], wo_s], axis=0).astype(BF16)
    sink_st = jnp.repeat(sinks.astype(F32)[jnp.asarray(_SWA_HEAD_ORDER)], CHUNK)[None, :]
    return w_in_p, w_out_p, sink_st


def _prep_sample_weights(w_in, w_out, sinks, gnw, t):
    wqk = w_in[:, _RQ:_RQ + 2 * RET_W].astype(BF16)
    per_head = lambda w: jnp.transpose(w.reshape(D_MODEL, RET_HEADS, HEAD_DIM), (1, 0, 2))
    wv = per_head(w_in[:, _RV:_RV + RET_W]).astype(BF16)
    wg = per_head(w_in[:, _RG:_RG + RET_W]).astype(BF16)
    wsq_h = per_head(w_in[:, _SQ:_SQ + SWA_W])
    group = (jnp.arange(SWA_HEADS) // SWA_GROUP)[:, None, None, None]
    slot = jnp.arange(SWA_KV_HEADS)[None, None, :, None]
    wsq = jnp.where(group == slot, wsq_h[:, :, None, :], 0.0).reshape(SWA_HEADS, D_MODEL, KV_W).astype(BF16)
    wskv = w_in[:, _SK:].astype(BF16)
    wor = w_out[:RET_W].reshape(RET_HEADS, HEAD_DIM, D_MODEL).astype(BF16)
    wos_h = w_out[RET_W:].reshape(SWA_HEADS, 1, HEAD_DIM, D_MODEL)
    slot_r = jnp.arange(SWA_KV_HEADS)[None, :, None, None]
    wos = jnp.where(group == slot_r, wos_h, 0.0).reshape(SWA_HEADS, KV_W, D_MODEL).astype(BF16)
    gnw_h = gnw.astype(F32).reshape(RET_HEADS, 1, HEAD_DIM)
    sink_rows = jnp.repeat(sinks.astype(F32), t)[:, None]
    return (wqk, wv, wg, wsq, wskv, wor, wos), gnw_h, sink_rows


def kernel(x_prompt, x_sample, state_ret, cache_swa_k, cache_swa_v, ln_gain, ln_bias, w_in,
           ret_gn_w, swa_sinks, w_out, ffn1_gate, ffn1_up, ffn1_down, ffn2_gate, ffn2_up, ffn2_down):
    assert ln_gain.shape[0] == DEPTH == 1
    bsz, seq, _ = x_prompt.shape
    nseq, t, _ = x_sample.shape
    lng = ln_gain[0].astype(F32).reshape(3, 1, D_MODEL)
    lnb = ln_bias[0].astype(F32).reshape(3, 1, D_MODEL)
    f1 = (ffn1_gate[0].astype(BF16), ffn1_up[0].astype(BF16), ffn1_down[0].astype(BF16))
    f2 = (ffn2_gate[0].astype(BF16), ffn2_up[0].astype(BF16), ffn2_down[0].astype(BF16))
    gnw = ret_gn_w[0].astype(F32)

    tm_p = 512
    tm_s = min(512, nseq * t)
    xp = x_prompt.reshape(bsz * seq, D_MODEL)
    xs = x_sample.reshape(nseq * t, D_MODEL)

    xp = _ffn_ln(xp, *f1, lng[0], lnb[0], tm=tm_p)
    w_in_p, w_out_p, sink_st = _prep_prompt_weights(w_in[0], w_out[0], swa_sinks[0])
    yp, st_p, ck_p, cv_p = _mixer_prompt(
        xp.reshape(bsz, seq, D_MODEL), w_in_p, w_out_p, lng[1], lnb[1], gnw.reshape(1, RET_W),
        sink_st, _prompt_constants(), tq=min(512, seq))
    yp = _ffn_ln(yp.reshape(bsz * seq, D_MODEL), *f2, lng[2], lnb[2], tm=tm_p)

    xs = _ffn_ln(xs, *f1, lng[0], lnb[0], tm=tm_s)
    weights, gnw_h, sink_rows = _prep_sample_weights(w_in[0], w_out[0], swa_sinks[0], gnw, t)
    ys, st_s, ck_s, cv_s = _mixer_sample(
        xs, state_ret[0].astype(F32).reshape(nseq, RET_W, HEAD_DIM),
        cache_swa_k[0].reshape(nseq, WINDOW, KV_W), cache_swa_v[0].reshape(nseq, WINDOW, KV_W),
        weights, lng[1], lnb[1], gnw_h, sink_rows, _sample_constants(t), sb=min(16, nseq), t=t)
    ys = _ffn_ln(ys, *f2, lng[2], lnb[2], tm=tm_s)

    kv_shape = lambda n: (1, n, WINDOW, SWA_KV_HEADS, HEAD_DIM)
    return (yp.reshape(bsz, seq, D_MODEL), ys.reshape(nseq, t, D_MODEL),
            st_p[None], ck_p.reshape(kv_shape(bsz)), cv_p.reshape(kv_shape(bsz)),
            st_s.reshape(1, nseq, RET_HEADS, HEAD_DIM, HEAD_DIM),
            ck_s.reshape(kv_shape(nseq)), cv_s.reshape(kv_shape(nseq)))
```

```python
import functools

import numpy as np
import jax
import jax.numpy as jnp
from jax import lax
from jax.experimental import pallas as pl
from jax.experimental.pallas import tpu as pltpu

F32 = jnp.float32
BF16 = jnp.bfloat16

D_MODEL = 1024
HEAD_DIM = 64
RET_HEADS = 8
SWA_HEADS = 8
SWA_KV_HEADS = 2
SWA_GROUP = SWA_HEADS // SWA_KV_HEADS
WINDOW = 128
CHUNK = 128
D_FF = 2816
RET_W = RET_HEADS * HEAD_DIM
SWA_W = SWA_HEADS * HEAD_DIM
KV_W = SWA_KV_HEADS * HEAD_DIM
IN_COLS = 4 * RET_W + SWA_W + 2 * KV_W
LN_EPS = 1e-5
GN_EPS = 1e-5
DEPTH = 1
DN_ALPHA = (2.0 * DEPTH) ** 0.25
QK_SCALE = HEAD_DIM ** -0.5
NEG_BIG = -1e30
LOG2E = 1.4426950408889634

LANES = 128
PAIRS = RET_HEADS // 2
GROUPS = RET_HEADS // 4
GROUP_W = 4 * HEAD_DIM
PROJ_ROWS = 256
PROJ_COLS = 512
VMEM_LIMIT_BYTES = 56 * 1024 * 1024

_RQ, _RK, _RV, _RG, _SQ = 0, RET_W, 2 * RET_W, 3 * RET_W, 4 * RET_W
_SK = 4 * RET_W + SWA_W
_SV = _SK + KV_W

_SWA_PAIR_HEADS = [(p, SWA_GROUP + p) for p in range(PAIRS)]
_SWA_HEAD_ORDER = [h for pair in _SWA_PAIR_HEADS for h in pair]

_FF_CHUNKS = ((0, 1536), (1536, 1280))
FFN_ROW_BLOCK = 256


def _log_gamma():
    h = np.arange(RET_HEADS, dtype=np.float64)
    return np.log1p(-np.exp2(-5.0 - h))


def _alibi_slopes():
    return np.exp2(-8.0 / SWA_HEADS * np.arange(1, SWA_HEADS + 1, dtype=np.float64))


def _layer_norm(z, g, b):
    mu = jnp.mean(z, axis=-1, keepdims=True)
    d = z - mu
    var = jnp.mean(d * d, axis=-1, keepdims=True)
    return d * lax.rsqrt(var + LN_EPS) * g + b


def _dot(a, b):
    return jnp.dot(a, b, preferred_element_type=F32)


def _dot_nt(a, b):
    return lax.dot_general(a, b, (((1,), (1,)), ((), ())), preferred_element_type=F32)


def _dot_tn(a, b):
    return lax.dot_general(a, b, (((0,), (0,)), ((), ())), preferred_element_type=F32)


def _ffn_ln_body(x_ref, wg_ref, wu_ref, wd_ref, g_ref, b_ref, o_ref, *, row_block):
    for r0 in range(0, x_ref.shape[0], row_block):
        x = x_ref[r0:r0 + row_block, :]
        xb = x.astype(BF16)
        acc = None
        for c0, cw in _FF_CHUNKS:
            gate = _dot(xb, wg_ref[:, c0:c0 + cw])
            up = _dot(xb, wu_ref[:, c0:c0 + cw])
            hid = (gate * jax.nn.sigmoid(gate) * up).astype(BF16)
            part = _dot(hid, wd_ref[c0:c0 + cw, :])
            acc = part if acc is None else acc + part
        z = DN_ALPHA * x + 0.5 * acc
        o_ref[r0:r0 + row_block, :] = _layer_norm(z, g_ref[...], b_ref[...])


def _const_spec(shape):
    nd = len(shape)
    return pl.BlockSpec(shape, lambda *_: (0,) * nd, pipeline_mode=pl.Buffered(1))


def _ffn_ln(x, wg, wu, wd, g, b, *, tm):
    rows = x.shape[0]
    assert rows % tm == 0 and tm % FFN_ROW_BLOCK == 0
    return pl.pallas_call(
        functools.partial(_ffn_ln_body, row_block=FFN_ROW_BLOCK),
        grid=(rows // tm,),
        in_specs=[
            pl.BlockSpec((tm, D_MODEL), lambda i: (i, 0)),
            _const_spec((D_MODEL, D_FF)),
            _const_spec((D_MODEL, D_FF)),
            _const_spec((D_FF, D_MODEL)),
            _const_spec((1, D_MODEL)),
            _const_spec((1, D_MODEL)),
        ],
        out_specs=pl.BlockSpec((tm, D_MODEL), lambda i: (i, 0)),
        out_shape=jax.ShapeDtypeStruct((rows, D_MODEL), F32),
        compiler_params=pltpu.CompilerParams(
            dimension_semantics=("arbitrary",), vmem_limit_bytes=VMEM_LIMIT_BYTES),
        name="ffn_ln",
    )(x, wg, wu, wd, g, b)


def _prompt_constants():
    lg = _log_gamma()
    idx = np.arange(CHUNK, dtype=np.float64)
    diff = idx[:, None] - idx[None, :]
    dm = np.where(diff >= 0, np.exp(np.where(diff >= 0, diff, 0.0)[None] * lg[:, None, None]), 0.0)
    dm_st = np.stack([np.block([[dm[4 * g], dm[4 * g + 2]], [dm[4 * g + 1], dm[4 * g + 3]]])
                      for g in range(GROUPS)])
    qw = np.repeat(np.exp((idx + 1.0)[:, None] * lg[None, :]), HEAD_DIM, axis=1)
    kw = np.repeat(np.exp((CHUNK - 1.0 - idx)[:, None] * lg[None, :]), HEAD_DIM, axis=1) * QK_SCALE
    gc = np.exp(CHUNK * lg)
    blk = np.kron(np.eye(4), np.ones((HEAD_DIM, HEAD_DIM)))
    gbd = np.stack([blk * np.repeat(gc[4 * g:4 * g + 4], HEAD_DIM)[:, None] for g in range(GROUPS)])
    slopes = _alibi_slopes()
    i = np.arange(CHUNK)
    j = np.arange(2 * CHUNK)
    dist = i[:, None] + CHUNK - j[None, :]
    within = (dist >= 0) & (dist <= WINDOW)
    valid = [within & (j >= CHUNK)[None, :], within]
    bias = np.stack([np.concatenate([np.where(valid[v], -slopes[h] * dist * LOG2E, NEG_BIG).T
                                     for h in _SWA_HEAD_ORDER], axis=1) for v in range(2)])
    f = lambda a: jnp.asarray(a, F32)
    return f(dm_st), f(qw), f(kw), f(gbd), f(blk), f(bias)


def _mixer_prompt_body(x_ref, xn_ref, win_ref, wout_ref, lng_ref, lnb_ref, gnw_ref, sink_ref,
                       dm_ref, qw_ref, kw_ref, gbd_ref, blk_ref, bias_ref,
                       y_ref, st_ref, ck_ref, cv_ref,
                       pb_ref, proj1_ref, oret_ref, yret_ref, swat_ref, s_ref, kprev_ref, vprev_ref,
                       *, tq):
    step = pl.program_id(1)
    nsteps = pl.num_programs(1)
    flat = pl.program_id(0) * nsteps + step
    slot = lax.rem(flat, 2)
    nchunks = tq // CHUNK
    nblocks = tq // PROJ_ROWS
    col_starts = range(0, IN_COLS, PROJ_COLS)

    @pl.when(step == 0)
    def _():
        s_ref[...] = jnp.zeros_like(s_ref)
        kprev_ref[...] = jnp.zeros_like(kprev_ref)
        vprev_ref[...] = jnp.zeros_like(vprev_ref)

    x = x_ref[0]

    def project_piece(dst, xrows, c0):
        c1 = min(c0 + PROJ_COLS, IN_COLS)
        dst[:, c0:c1] = _dot(xrows.astype(BF16), win_ref[:, c0:c1])

    @pl.when(flat == 0)
    def _():
        for c0 in col_starts:
            project_piece(pb_ref.at[0], x[:PROJ_ROWS], c0)

    def proj(r0, nrows, c0, c1):
        if r0 < PROJ_ROWS:
            return pb_ref[slot, r0:r0 + nrows, c0:c1]
        return proj1_ref[r0 - PROJ_ROWS:r0 - PROJ_ROWS + nrows, c0:c1]

    pending = []

    def emit_pieces(n):
        for _ in range(min(n, len(pending))):
            pending.pop(0)()

    pieces_per_chunk = pl.cdiv(len(col_starts), PROJ_ROWS // CHUNK)

    lane = lax.broadcasted_iota(jnp.int32, (CHUNK, LANES), 1)
    lo = lane < HEAD_DIM
    lane_g = lax.broadcasted_iota(jnp.int32, (CHUNK, GROUP_W), 1)
    lo_g = jnp.bitwise_and(lane_g, LANES - 1) < HEAD_DIM
    blk = blk_ref[...]
    zero_b = jnp.zeros((CHUNK, LANES), BF16)

    def pair_diag(a):
        return jnp.concatenate([jnp.concatenate([a[:, :LANES], zero_b], axis=1),
                                jnp.concatenate([zero_b, a[:, LANES:]], axis=1)], axis=0)

    def head_stack(a):
        zero = jnp.zeros_like(a)
        return jnp.concatenate([jnp.where(lo_g, a, zero), jnp.where(lo_g, zero, a)], axis=0)

    lane_t = lax.broadcasted_iota(jnp.int32, (PROJ_ROWS, LANES), 1)
    lo_t = lane_t < HEAD_DIM

    def finish_block(kb):
        rb = slice(kb * PROJ_ROWS, (kb + 1) * PROJ_ROWS)
        for p in range(PAIRS):
            cols = slice(p * LANES, (p + 1) * LANES)
            o = oret_ref[rb, cols]
            zero = jnp.zeros_like(o)
            s_lo = jnp.sum(jnp.where(lo_t, o, zero), axis=-1, keepdims=True)
            s_hi = jnp.sum(jnp.where(lo_t, zero, o), axis=-1, keepdims=True)
            d = o - jnp.where(lo_t, s_lo, s_hi) * (1.0 / HEAD_DIM)
            d2 = d * d
            v_lo = jnp.sum(jnp.where(lo_t, d2, zero), axis=-1, keepdims=True)
            v_hi = jnp.sum(jnp.where(lo_t, zero, d2), axis=-1, keepdims=True)
            var = jnp.where(lo_t, v_lo, v_hi) * (1.0 / HEAD_DIM)
            gn = d * lax.rsqrt(var + GN_EPS) * gnw_ref[:, cols]
            gate = proj(kb * PROJ_ROWS, PROJ_ROWS, _RG + p * LANES, _RG + (p + 1) * LANES)
            yret_ref[rb, cols] = (gate * jax.nn.sigmoid(gate) * gn).astype(BF16)
        mix = (_dot(yret_ref[rb, :], wout_ref[:RET_W, :])
               + _dot_tn(swat_ref[:, rb].astype(BF16), wout_ref[RET_W:, :]))
        y_ref[0, rb, :] = _layer_norm(DN_ALPHA * x[rb] + mix, lng_ref[...], lnb_ref[...])

    k_prev = kprev_ref[...]
    vt_prev = vprev_ref[...]
    for c in range(nchunks):
        r0 = c * CHUNK
        rows = slice(r0, r0 + CHUNK)
        if r0 % PROJ_ROWS == 0:
            nb = r0 // PROJ_ROWS + 1
            if nb < nblocks:
                dst = proj1_ref.at[(nb - 1) * PROJ_ROWS:nb * PROJ_ROWS]
                xrows = x[nb * PROJ_ROWS:(nb + 1) * PROJ_ROWS]
            else:
                dst, xrows = pb_ref.at[1 - slot], xn_ref[0]
            pending.extend(functools.partial(project_piece, dst, xrows, c0) for c0 in col_starts)
        budget = pieces_per_chunk
        for g in range(GROUPS):
            cols = slice(g * GROUP_W, (g + 1) * GROUP_W)
            q4 = proj(r0, CHUNK, _RQ + g * GROUP_W, _RQ + (g + 1) * GROUP_W)
            k4 = proj(r0, CHUNK, _RK + g * GROUP_W, _RK + (g + 1) * GROUP_W)
            v4 = proj(r0, CHUNK, _RV + g * GROUP_W, _RV + (g + 1) * GROUP_W).astype(BF16)
            k_bd = pair_diag((k4 * QK_SCALE).astype(BF16))
            sc = _dot_nt(head_stack(q4).astype(BF16), k_bd) * dm_ref[g]
            qd_st = head_stack(q4 * qw_ref[:, cols])
            lhs = jnp.concatenate([sc.astype(BF16), qd_st.astype(BF16)], axis=1)
            s_old = s_ref[g]
            rhs = jnp.concatenate([pair_diag(v4), s_old.astype(BF16)], axis=0)
            r = _dot(lhs, rhs)
            oret_ref[rows, cols] = jnp.where(lo_g, r[:CHUNK], r[CHUNK:])
            kd = (k4 * kw_ref[:, cols]).astype(BF16)
            s_ref[g] = gbd_ref[g] * s_old + blk * _dot_tn(kd, v4)
        k_cur = proj(r0, CHUNK, _SK, _SK + KV_W).astype(BF16)
        vt_cur = proj(r0, CHUNK, _SV, _SV + KV_W).T.astype(BF16)
        kk = jnp.concatenate([k_prev, k_cur], axis=0)
        vvt = jnp.concatenate([vt_prev, vt_cur], axis=1)
        q_parts = []
        for p in range(PAIRS):
            q2 = proj(r0, CHUNK, _SQ + p * LANES, _SQ + (p + 1) * LANES) * (QK_SCALE * LOG2E)
            zero = jnp.zeros_like(q2)
            q_parts += [jnp.where(lo, q2, zero).astype(BF16), jnp.where(lo, zero, q2).astype(BF16)]
        st = _dot_nt(kk, jnp.concatenate(q_parts, axis=0))
        if c == 0:
            st = st + bias_ref[jnp.minimum(step, 1)]
        else:
            st = st + bias_ref[1]
        sink = sink_ref[...] * LOG2E
        e_parts, den_parts = [], []
        for p in range(PAIRS):
            if budget > 0:
                emit_pieces(1)
                budget -= 1
            pc = slice(2 * p * CHUNK, 2 * (p + 1) * CHUNK)
            st_p, sink_p = st[:, pc], sink[:, pc]
            m = jnp.maximum(jnp.max(st_p, axis=0, keepdims=True), sink_p)
            e = jnp.exp2(st_p - m)
            den_parts.append(jnp.sum(e, axis=0, keepdims=True) + jnp.exp2(sink_p - m))
            e_parts.append(e.astype(BF16))
        if (r0 + CHUNK) % PROJ_ROWS == 0:
            emit_pieces(len(pending))
        den = jnp.concatenate(den_parts, axis=1)
        ot = _dot(vvt, jnp.concatenate(e_parts, axis=1)) * (1.0 / den)
        for p in range(PAIRS):
            c0 = 2 * p * CHUNK
            swat_ref[p * LANES:p * LANES + HEAD_DIM, rows] = ot[:HEAD_DIM, c0:c0 + CHUNK]
            swat_ref[p * LANES + HEAD_DIM:(p + 1) * LANES, rows] = ot[HEAD_DIM:, c0 + CHUNK:c0 + 2 * CHUNK]
        k_prev, vt_prev = k_cur, vt_cur
        if (r0 + CHUNK) % PROJ_ROWS == 0:
            finish_block(r0 // PROJ_ROWS)
    kprev_ref[...] = k_prev
    vprev_ref[...] = vt_prev

    @pl.when(step == nsteps - 1)
    def _():
        for h in range(RET_HEADS):
            g, i = divmod(h, 4)
            blk_h = s_ref[g, i * HEAD_DIM:(i + 1) * HEAD_DIM, (i // 2) * LANES:(i // 2 + 1) * LANES]
            if i % 2:
                blk_h = pltpu.roll(blk_h, HEAD_DIM, 1)
            st_ref[0, h] = blk_h[:, :HEAD_DIM]
        ck_ref[0] = proj(tq - WINDOW, WINDOW, _SK, _SK + KV_W)
        cv_ref[0] = proj(tq - WINDOW, WINDOW, _SV, _SV + KV_W)


def _mixer_prompt(x, w_in_p, w_out_p, lng, lnb, gnw, sink_st, consts, *, tq):
    bsz, seq, _ = x.shape
    assert seq % tq == 0 and tq % PROJ_ROWS == 0 and PROJ_ROWS % CHUNK == 0
    dm_st, qw, kw, gbd, blk, bias = consts
    body = functools.partial(_mixer_prompt_body, tq=tq)
    nsteps = seq // tq
    blocks_per_tile = tq // PROJ_ROWS

    def next_tile_head(b, s):
        nxt = jnp.minimum(b * nsteps + s + 1, bsz * nsteps - 1)
        return nxt // nsteps, (nxt % nsteps) * blocks_per_tile, 0

    return pl.pallas_call(
        body,
        grid=(bsz, nsteps),
        in_specs=[
            pl.BlockSpec((1, tq, D_MODEL), lambda b, s: (b, s, 0)),
            pl.BlockSpec((1, PROJ_ROWS, D_MODEL), next_tile_head),
            _const_spec((D_MODEL, IN_COLS)),
            _const_spec((2 * RET_W, D_MODEL)),
            _const_spec((1, D_MODEL)),
            _const_spec((1, D_MODEL)),
            _const_spec((1, RET_W)),
            _const_spec((1, SWA_HEADS * CHUNK)),
            _const_spec(dm_st.shape),
            _const_spec(qw.shape),
            _const_spec(kw.shape),
            _const_spec(gbd.shape),
            _const_spec(blk.shape),
            _const_spec(bias.shape),
        ],
        out_specs=[
            pl.BlockSpec((1, tq, D_MODEL), lambda b, s: (b, s, 0)),
            pl.BlockSpec((1, RET_HEADS, HEAD_DIM, HEAD_DIM), lambda b, s: (b, 0, 0, 0)),
            pl.BlockSpec((1, WINDOW, KV_W), lambda b, s: (b, 0, 0)),
            pl.BlockSpec((1, WINDOW, KV_W), lambda b, s: (b, 0, 0)),
        ],
        out_shape=[
            jax.ShapeDtypeStruct((bsz, seq, D_MODEL), F32),
            jax.ShapeDtypeStruct((bsz, RET_HEADS, HEAD_DIM, HEAD_DIM), F32),
            jax.ShapeDtypeStruct((bsz, WINDOW, KV_W), F32),
            jax.ShapeDtypeStruct((bsz, WINDOW, KV_W), F32),
        ],
        scratch_shapes=[
            pltpu.VMEM((2, PROJ_ROWS, IN_COLS), F32),
            pltpu.VMEM((tq - PROJ_ROWS, IN_COLS), F32),
            pltpu.VMEM((tq, RET_W), F32),
            pltpu.VMEM((tq, RET_W), BF16),
            pltpu.VMEM((SWA_W, tq), F32),
            pltpu.VMEM((GROUPS, GROUP_W, GROUP_W), F32),
            pltpu.VMEM((CHUNK, KV_W), BF16),
            pltpu.VMEM((KV_W, CHUNK), BF16),
        ],
        compiler_params=pltpu.CompilerParams(
            dimension_semantics=("arbitrary", "arbitrary"), vmem_limit_bytes=VMEM_LIMIT_BYTES),
        name="mixer_prompt",
    )(x, x, w_in_p, w_out_p, lng, lnb, gnw, sink_st, dm_st, qw, kw, gbd, blk, bias)


def _sample_constants(t):
    lg = _log_gamma()
    hh = np.arange(RET_HEADS)
    tt = np.arange(t, dtype=np.float64)
    row_h = np.repeat(hh, t)
    row_t = np.tile(tt, RET_HEADS)
    col_h = np.repeat(hh, HEAD_DIM)
    same = (row_h[:, None] == col_h[None, :]).astype(np.float64)
    mask_q1 = same * QK_SCALE
    mask_qw = same * np.exp((row_t + 1.0) * lg[row_h])[:, None]
    mask_kw = same * (np.exp((t - 1.0 - row_t) * lg[row_h]) * QK_SCALE)[:, None]
    diff = row_t[:, None] - row_t[None, :]
    same_h = row_h[:, None] == row_h[None, :]
    dm = np.where(same_h & (diff >= 0), np.exp(np.where(diff >= 0, diff, 0.0) * lg[row_h][:, None]), 0.0)
    g_t = np.repeat(np.exp(t * lg), HEAD_DIM)[:, None] * np.ones((1, HEAD_DIM))
    slopes = _alibi_slopes()
    j = np.arange(WINDOW + t)
    dist = row_t[:, None] + WINDOW - j[None, :]
    valid = (dist >= 0) & (dist <= WINDOW)
    bias = np.where(valid, -slopes[row_h][:, None] * dist, NEG_BIG)
    f = lambda a: jnp.asarray(a, F32)
    return f(mask_q1), f(mask_qw), f(mask_kw), f(dm), f(g_t), f(bias)


def _mixer_sample_body(x_ref, st_ref, ck_ref, cv_ref,
                       wqk_ref, wv_ref, wg_ref, wsq_ref, wskv_ref, wor_ref, wos_ref,
                       lng_ref, lnb_ref, gnw_ref, sink_ref,
                       mq1_ref, mqw_ref, mkw_ref, dm_ref, gt_ref, bias_ref,
                       y_ref, sto_ref, cko_ref, cvo_ref,
                       qk_ref, vh_ref, qs_ref, kvn_ref, oh_ref, os_ref, *, sb, t):
    ht = RET_HEADS * t
    x = x_ref[...]
    xb = x.astype(BF16)
    qk_ref[...] = _dot(xb, wqk_ref[...])
    kvn_ref[...] = _dot(xb, wskv_ref[...])
    for h in range(RET_HEADS):
        vh_ref[h] = _dot(xb, wv_ref[h])
        qs_ref[h] = _dot(xb, wsq_ref[h]) * QK_SCALE

    mq1 = mq1_ref[...]
    mqw = mqw_ref[...]
    mkw = mkw_ref[...]
    dm = dm_ref[...]
    g_t = gt_ref[...]
    bias = bias_ref[...]
    sink = sink_ref[...]

    def per_seq(b, carry):
        r0 = pl.multiple_of(b * t, t)
        rows = pl.ds(r0, t)
        q_b = qk_ref[rows, 0:RET_W]
        k_b = qk_ref[rows, RET_W:2 * RET_W]
        q_rep = jnp.concatenate([q_b] * RET_HEADS, axis=0)
        k_rep = jnp.concatenate([k_b] * RET_HEADS, axis=0)
        v_st = vh_ref[:, rows, :].reshape(ht, HEAD_DIM).astype(BF16)
        s0 = st_ref[b]
        sc = _dot_nt((q_rep * mq1).astype(BF16), k_rep.astype(BF16)) * dm
        o = _dot((q_rep * mqw).astype(BF16), s0.astype(BF16)) + _dot(sc.astype(BF16), v_st)
        oh_ref[:, rows, :] = o.reshape(RET_HEADS, t, HEAD_DIM)
        sto_ref[b] = g_t * s0 + _dot_tn((k_rep * mkw).astype(BF16), v_st)
        k_new = kvn_ref[rows, 0:KV_W]
        v_new = kvn_ref[rows, KV_W:2 * KV_W]
        k_old = ck_ref[b]
        v_old = cv_ref[b]
        kk = jnp.concatenate([k_old, k_new], axis=0)
        vv = jnp.concatenate([v_old, v_new], axis=0)
        q_s = qs_ref[:, rows, :].reshape(ht, LANES)
        s = _dot_nt(q_s.astype(BF16), kk.astype(BF16)) + bias
        m = jnp.maximum(jnp.max(s, axis=-1, keepdims=True), sink)
        e = jnp.exp(s - m)
        den = jnp.sum(e, axis=-1, keepdims=True) + jnp.exp(sink - m)
        o_s = _dot(e.astype(BF16), vv.astype(BF16)) / den
        os_ref[:, rows, :] = o_s.reshape(SWA_HEADS, t, LANES)
        cko_ref[b] = kk[t:]
        cvo_ref[b] = vv[t:]
        return carry

    lax.fori_loop(0, sb, per_seq, 0, unroll=8)

    mix = None
    for h in range(RET_HEADS):
        o = oh_ref[h]
        mu = jnp.mean(o, axis=-1, keepdims=True)
        d = o - mu
        var = jnp.mean(d * d, axis=-1, keepdims=True)
        gn = d * lax.rsqrt(var + GN_EPS) * gnw_ref[h]
        gate = _dot(xb, wg_ref[h])
        y_h = (gate * jax.nn.sigmoid(gate) * gn).astype(BF16)
        part = _dot(y_h, wor_ref[h]) + _dot(os_ref[h].astype(BF16), wos_ref[h])
        mix = part if mix is None else mix + part
    y_ref[...] = _layer_norm(DN_ALPHA * x + mix, lng_ref[...], lnb_ref[...])


def _mixer_sample(x, state, ck, cv, weights, lng, lnb, gnw_h, sink_rows, consts, *, sb, t):
    rows_total = x.shape[0]
    nseq = rows_total // t
    assert nseq % sb == 0
    rows = sb * t
    wqk, wv, wg, wsq, wskv, wor, wos = weights
    body = functools.partial(_mixer_sample_body, sb=sb, t=t)
    return pl.pallas_call(
        body,
        grid=(nseq // sb,),
        in_specs=[
            pl.BlockSpec((rows, D_MODEL), lambda i: (i, 0)),
            pl.BlockSpec((sb, RET_W, HEAD_DIM), lambda i: (i, 0, 0)),
            pl.BlockSpec((sb, WINDOW, KV_W), lambda i: (i, 0, 0)),
            pl.BlockSpec((sb, WINDOW, KV_W), lambda i: (i, 0, 0)),
        ] + [_const_spec(w.shape) for w in weights] + [
            _const_spec(lng.shape), _const_spec(lnb.shape), _const_spec(gnw_h.shape),
            _const_spec(sink_rows.shape),
        ] + [_const_spec(c.shape) for c in consts],
        out_specs=[
            pl.BlockSpec((rows, D_MODEL), lambda i: (i, 0)),
            pl.BlockSpec((sb, RET_W, HEAD_DIM), lambda i: (i, 0, 0)),
            pl.BlockSpec((sb, WINDOW, KV_W), lambda i: (i, 0, 0)),
            pl.BlockSpec((sb, WINDOW, KV_W), lambda i: (i, 0, 0)),
        ],
        out_shape=[
            jax.ShapeDtypeStruct((rows_total, D_MODEL), F32),
            jax.ShapeDtypeStruct((nseq, RET_W, HEAD_DIM), F32),
            jax.ShapeDtypeStruct((nseq, WINDOW, KV_W), F32),
            jax.ShapeDtypeStruct((nseq, WINDOW, KV_W), F32),
        ],
        scratch_shapes=[
            pltpu.VMEM((rows, 2 * RET_W), F32),
            pltpu.VMEM((RET_HEADS, rows, HEAD_DIM), F32),
            pltpu.VMEM((SWA_HEADS, rows, LANES), F32),
            pltpu.VMEM((rows, 2 * KV_W), F32),
            pltpu.VMEM((RET_HEADS, rows, HEAD_DIM), F32),
            pltpu.VMEM((SWA_HEADS, rows, LANES), F32),
        ],
        compiler_params=pltpu.CompilerParams(
            dimension_semantics=("arbitrary",), vmem_limit_bytes=VMEM_LIMIT_BYTES),
        name="mixer_sample",
    )(x, state, ck, cv, *weights, lng, lnb, gnw_h, sink_rows, *consts)


def _prep_prompt_weights(w_in, w_out, sinks):
    sq = w_in[:, _SQ:_SQ + SWA_W].reshape(D_MODEL, SWA_HEADS, HEAD_DIM)
    sq = sq[:, jnp.asarray(_SWA_HEAD_ORDER), :].reshape(D_MODEL, SWA_W)
    w_in_p = jnp.concatenate([w_in[:, :_SQ], sq, w_in[:, _SK:]], axis=1).astype(BF16)
    wo_s = w_out[RET_W:].reshape(SWA_HEADS, HEAD_DIM, D_MODEL)
    wo_s = wo_s[jnp.asarray(_SWA_HEAD_ORDER)].reshape(SWA_W, D_MODEL)
    w_out_p = jnp.concatenate([w_out[:RET_W], wo_s], axis=0).astype(BF16)
    sink_st = jnp.repeat(sinks.astype(F32)[jnp.asarray(_SWA_HEAD_ORDER)], CHUNK)[None, :]
    return w_in_p, w_out_p, sink_st


def _prep_sample_weights(w_in, w_out, sinks, gnw, t):
    wqk = w_in[:, _RQ:_RQ + 2 * RET_W].astype(BF16)
    per_head = lambda w: jnp.transpose(w.reshape(D_MODEL, RET_HEADS, HEAD_DIM), (1, 0, 2))
    wv = per_head(w_in[:, _RV:_RV + RET_W]).astype(BF16)
    wg = per_head(w_in[:, _RG:_RG + RET_W]).astype(BF16)
    wsq_h = per_head(w_in[:, _SQ:_SQ + SWA_W])
    group = (jnp.arange(SWA_HEADS) // SWA_GROUP)[:, None, None, None]
    slot = jnp.arange(SWA_KV_HEADS)[None, None, :, None]
    wsq = jnp.where(group == slot, wsq_h[:, :, None, :], 0.0).reshape(SWA_HEADS, D_MODEL, KV_W).astype(BF16)
    wskv = w_in[:, _SK:].astype(BF16)
    wor = w_out[:RET_W].reshape(RET_HEADS, HEAD_DIM, D_MODEL).astype(BF16)
    wos_h = w_out[RET_W:].reshape(SWA_HEADS, 1, HEAD_DIM, D_MODEL)
    slot_r = jnp.arange(SWA_KV_HEADS)[None, :, None, None]
    wos = jnp.where(group == slot_r, wos_h, 0.0).reshape(SWA_HEADS, KV_W, D_MODEL).astype(BF16)
    gnw_h = gnw.astype(F32).reshape(RET_HEADS, 1, HEAD_DIM)
    sink_rows = jnp.repeat(sinks.astype(F32), t)[:, None]
    return (wqk, wv, wg, wsq, wskv, wor, wos), gnw_h, sink_rows


def kernel(x_prompt, x_sample, state_ret, cache_swa_k, cache_swa_v, ln_gain, ln_bias, w_in,
           ret_gn_w, swa_sinks, w_out, ffn1_gate, ffn1_up, ffn1_down, ffn2_gate, ffn2_up, ffn2_down):
    assert ln_gain.shape[0] == DEPTH == 1
    bsz, seq, _ = x_prompt.shape
    nseq, t, _ = x_sample.shape
    lng = ln_gain[0].astype(F32).reshape(3, 1, D_MODEL)
    lnb = ln_bias[0].astype(F32).reshape(3, 1, D_MODEL)
    f1 = (ffn1_gate[0].astype(BF16), ffn1_up[0].astype(BF16), ffn1_down[0].astype(BF16))
    f2 = (ffn2_gate[0].astype(BF16), ffn2_up[0].astype(BF16), ffn2_down[0].astype(BF16))
    gnw = ret_gn_w[0].astype(F32)

    tm_p = min(1024, bsz * seq)
    tm_s = min(512, nseq * t)
    xp = x_prompt.reshape(bsz * seq, D_MODEL)
    xs = x_sample.reshape(nseq * t, D_MODEL)

    xp = _ffn_ln(xp, *f1, lng[0], lnb[0], tm=tm_p)
    w_in_p, w_out_p, sink_st = _prep_prompt_weights(w_in[0], w_out[0], swa_sinks[0])
    yp, st_p, ck_p, cv_p = _mixer_prompt(
        xp.reshape(bsz, seq, D_MODEL), w_in_p, w_out_p, lng[1], lnb[1], gnw.reshape(1, RET_W),
        sink_st, _prompt_constants(), tq=min(1024, seq))
    yp = _ffn_ln(yp.reshape(bsz * seq, D_MODEL), *f2, lng[2], lnb[2], tm=tm_p)

    xs = _ffn_ln(xs, *f1, lng[0], lnb[0], tm=tm_s)
    weights, gnw_h, sink_rows = _prep_sample_weights(w_in[0], w_out[0], swa_sinks[0], gnw, t)
    ys, st_s, ck_s, cv_s = _mixer_sample(
        xs, state_ret[0].astype(F32).reshape(nseq, RET_W, HEAD_DIM),
        cache_swa_k[0].reshape(nseq, WINDOW, KV_W), cache_swa_v[0].reshape(nseq, WINDOW, KV_W),
        weights, lng[1], lnb[1], gnw_h, sink_rows, _sample_constants(t), sb=min(16, nseq), t=t)
    ys = _ffn_ln(ys, *f2, lng[2], lnb[2], tm=tm_s)

    kv_shape = lambda n: (1, n, WINDOW, SWA_KV_HEADS, HEAD_DIM)
    return (yp.reshape(bsz, seq, D_MODEL), ys.reshape(nseq, t, D_MODEL),
            st_p[None], ck_p.reshape(kv_shape(bsz)), cv_p.reshape(kv_shape(bsz)),
            st_s.reshape(1, nseq, RET_HEADS, HEAD_DIM, HEAD_DIM),
            ck_s.reshape(kv_shape(nseq)), cv_s.reshape(kv_shape(nseq)))
```

```python
import functools

import numpy as np
import jax
import jax.numpy as jnp
from jax import lax
from jax.experimental import pallas as pl
from jax.experimental.pallas import tpu as pltpu

F32 = jnp.float32
BF16 = jnp.bfloat16

D_MODEL = 1024
HEAD_DIM = 64
RET_HEADS = 8
SWA_HEADS = 8
SWA_KV_HEADS = 2
SWA_GROUP = SWA_HEADS // SWA_KV_HEADS
WINDOW = 128
CHUNK = 128
D_FF = 2816
RET_W = RET_HEADS * HEAD_DIM
SWA_W = SWA_HEADS * HEAD_DIM
KV_W = SWA_KV_HEADS * HEAD_DIM
IN_COLS = 4 * RET_W + SWA_W + 2 * KV_W
LN_EPS = 1e-5
GN_EPS = 1e-5
DEPTH = 1
DN_ALPHA = (2.0 * DEPTH) ** 0.25
QK_SCALE = HEAD_DIM ** -0.5
NEG_BIG = -1e30
LOG2E = 1.4426950408889634

LANES = 128
PAIRS = RET_HEADS // 2
GROUPS = RET_HEADS // 4
GROUP_W = 4 * HEAD_DIM
PROJ_ROWS = 256
PROJ_COLS = 512
VMEM_LIMIT_BYTES = 56 * 1024 * 1024

_RQ, _RK, _RV, _RG, _SQ = 0, RET_W, 2 * RET_W, 3 * RET_W, 4 * RET_W
_SK = 4 * RET_W + SWA_W
_SV = _SK + KV_W

_SWA_PAIR_HEADS = [(p, SWA_GROUP + p) for p in range(PAIRS)]
_SWA_HEAD_ORDER = [h for pair in _SWA_PAIR_HEADS for h in pair]

_FF_CHUNKS = ((0, 2816),)
FFN_ROW_BLOCK = 256


def _log_gamma():
    h = np.arange(RET_HEADS, dtype=np.float64)
    return np.log1p(-np.exp2(-5.0 - h))


def _alibi_slopes():
    return np.exp2(-8.0 / SWA_HEADS * np.arange(1, SWA_HEADS + 1, dtype=np.float64))


def _layer_norm(z, g, b):
    mu = jnp.mean(z, axis=-1, keepdims=True)
    d = z - mu
    var = jnp.mean(d * d, axis=-1, keepdims=True)
    return d * lax.rsqrt(var + LN_EPS) * g + b


def _pair_group_norm(o, lo, gain):
    zero = jnp.zeros_like(o)
    s_lo = jnp.sum(jnp.where(lo, o, zero), axis=-1, keepdims=True)
    s_hi = jnp.sum(jnp.where(lo, zero, o), axis=-1, keepdims=True)
    d = o - jnp.where(lo, s_lo, s_hi) * (1.0 / HEAD_DIM)
    d2 = d * d
    v_lo = jnp.sum(jnp.where(lo, d2, zero), axis=-1, keepdims=True)
    v_hi = jnp.sum(jnp.where(lo, zero, d2), axis=-1, keepdims=True)
    var = jnp.where(lo, v_lo, v_hi) * (1.0 / HEAD_DIM)
    return d * lax.rsqrt(var + GN_EPS) * gain


def _dot(a, b):
    return jnp.dot(a, b, preferred_element_type=F32)


def _dot_nt(a, b):
    return lax.dot_general(a, b, (((1,), (1,)), ((), ())), preferred_element_type=F32)


def _dot_tn(a, b):
    return lax.dot_general(a, b, (((0,), (0,)), ((), ())), preferred_element_type=F32)


def _ffn_ln_body(x_ref, wg_ref, wu_ref, wd_ref, g_ref, b_ref, o_ref, *, row_block):
    for r0 in range(0, x_ref.shape[0], row_block):
        x = x_ref[r0:r0 + row_block, :]
        xb = x.astype(BF16)
        acc = None
        for c0, cw in _FF_CHUNKS:
            gate = _dot(xb, wg_ref[:, c0:c0 + cw])
            up = _dot(xb, wu_ref[:, c0:c0 + cw])
            hid = (gate * jax.nn.sigmoid(gate) * up).astype(BF16)
            part = _dot(hid, wd_ref[c0:c0 + cw, :])
            acc = part if acc is None else acc + part
        z = DN_ALPHA * x + 0.5 * acc
        o_ref[r0:r0 + row_block, :] = _layer_norm(z, g_ref[...], b_ref[...])


def _const_spec(shape):
    nd = len(shape)
    return pl.BlockSpec(shape, lambda *_: (0,) * nd, pipeline_mode=pl.Buffered(1))


def _ffn_ln(x, wg, wu, wd, g, b, *, tm):
    rows = x.shape[0]
    assert rows % tm == 0 and tm % FFN_ROW_BLOCK == 0
    return pl.pallas_call(
        functools.partial(_ffn_ln_body, row_block=FFN_ROW_BLOCK),
        grid=(rows // tm,),
        in_specs=[
            pl.BlockSpec((tm, D_MODEL), lambda i: (i, 0)),
            _const_spec((D_MODEL, D_FF)),
            _const_spec((D_MODEL, D_FF)),
            _const_spec((D_FF, D_MODEL)),
            _const_spec((1, D_MODEL)),
            _const_spec((1, D_MODEL)),
        ],
        out_specs=pl.BlockSpec((tm, D_MODEL), lambda i: (i, 0)),
        out_shape=jax.ShapeDtypeStruct((rows, D_MODEL), F32),
        compiler_params=pltpu.CompilerParams(
            dimension_semantics=("arbitrary",), vmem_limit_bytes=VMEM_LIMIT_BYTES),
        name="ffn_ln",
    )(x, wg, wu, wd, g, b)


def _prompt_constants():
    lg = _log_gamma()
    idx = np.arange(CHUNK, dtype=np.float64)
    diff = idx[:, None] - idx[None, :]
    dm = np.where(diff >= 0, np.exp(np.where(diff >= 0, diff, 0.0)[None] * lg[:, None, None]), 0.0)
    dm_st = np.stack([np.block([[dm[4 * g], dm[4 * g + 2]], [dm[4 * g + 1], dm[4 * g + 3]]])
                      for g in range(GROUPS)])
    qw = np.repeat(np.exp((idx + 1.0)[:, None] * lg[None, :]), HEAD_DIM, axis=1)
    kw = np.repeat(np.exp((CHUNK - 1.0 - idx)[:, None] * lg[None, :]), HEAD_DIM, axis=1) * QK_SCALE
    gc = np.exp(CHUNK * lg)
    blk = np.kron(np.eye(4), np.ones((HEAD_DIM, HEAD_DIM)))
    gbd = np.stack([blk * np.repeat(gc[4 * g:4 * g + 4], HEAD_DIM)[:, None] for g in range(GROUPS)])
    slopes = _alibi_slopes()
    i = np.arange(CHUNK)
    j = np.arange(2 * CHUNK)
    dist = i[:, None] + CHUNK - j[None, :]
    within = (dist >= 0) & (dist <= WINDOW)
    valid = [within & (j >= CHUNK)[None, :], within]
    bias = np.stack([np.concatenate([np.where(valid[v], -slopes[h] * dist * LOG2E, NEG_BIG).T
                                     for h in _SWA_HEAD_ORDER], axis=1) for v in range(2)])
    f = lambda a: jnp.asarray(a, F32)
    return f(dm_st), f(qw), f(kw), f(gbd), f(blk), f(bias)


def _mixer_prompt_body(x_ref, xn_ref, win_ref, wout_ref, lng_ref, lnb_ref, gnw_ref, sink_ref,
                       dm_ref, qw_ref, kw_ref, gbd_ref, blk_ref, bias_ref,
                       y_ref, st_ref, ck_ref, cv_ref,
                       pb_ref, proj1_ref, oret_ref, yret_ref, swat_ref, s_ref, kprev_ref, vprev_ref,
                       *, tq):
    step = pl.program_id(1)
    nsteps = pl.num_programs(1)
    flat = pl.program_id(0) * nsteps + step
    slot = lax.rem(flat, 2)
    nchunks = tq // CHUNK
    nblocks = tq // PROJ_ROWS
    col_starts = range(0, IN_COLS, PROJ_COLS)

    @pl.when(step == 0)
    def _():
        s_ref[...] = jnp.zeros_like(s_ref)
        kprev_ref[...] = jnp.zeros_like(kprev_ref)
        vprev_ref[...] = jnp.zeros_like(vprev_ref)

    x = x_ref[0]

    def project_piece(dst, xrows, c0):
        c1 = min(c0 + PROJ_COLS, IN_COLS)
        dst[:, c0:c1] = _dot(xrows.astype(BF16), win_ref[:, c0:c1])

    @pl.when(flat == 0)
    def _():
        for c0 in col_starts:
            project_piece(pb_ref.at[0], x[:PROJ_ROWS], c0)

    def proj(r0, nrows, c0, c1):
        if r0 < PROJ_ROWS:
            return pb_ref[slot, r0:r0 + nrows, c0:c1]
        return proj1_ref[r0 - PROJ_ROWS:r0 - PROJ_ROWS + nrows, c0:c1]

    pending = []

    def emit_pieces(n):
        for _ in range(min(n, len(pending))):
            pending.pop(0)()

    pieces_per_chunk = pl.cdiv(len(col_starts), PROJ_ROWS // CHUNK)

    lane = lax.broadcasted_iota(jnp.int32, (CHUNK, LANES), 1)
    lo = lane < HEAD_DIM
    lane_g = lax.broadcasted_iota(jnp.int32, (CHUNK, GROUP_W), 1)
    lo_g = jnp.bitwise_and(lane_g, LANES - 1) < HEAD_DIM
    blk = blk_ref[...]
    zero_b = jnp.zeros((CHUNK, LANES), BF16)

    def pair_diag(a):
        return jnp.concatenate([jnp.concatenate([a[:, :LANES], zero_b], axis=1),
                                jnp.concatenate([zero_b, a[:, LANES:]], axis=1)], axis=0)

    def head_stack(a):
        zero = jnp.zeros_like(a)
        return jnp.concatenate([jnp.where(lo_g, a, zero), jnp.where(lo_g, zero, a)], axis=0)

    lane_t = lax.broadcasted_iota(jnp.int32, (PROJ_ROWS, LANES), 1)
    lo_t = lane_t < HEAD_DIM

    def finish_block(kb):
        rb = slice(kb * PROJ_ROWS, (kb + 1) * PROJ_ROWS)
        for p in range(PAIRS):
            cols = slice(p * LANES, (p + 1) * LANES)
            gn = _pair_group_norm(oret_ref[rb, cols], lo_t, gnw_ref[:, cols])
            gate = proj(kb * PROJ_ROWS, PROJ_ROWS, _RG + p * LANES, _RG + (p + 1) * LANES)
            yret_ref[rb, cols] = (gate * jax.nn.sigmoid(gate) * gn).astype(BF16)
        mix = (_dot(yret_ref[rb, :], wout_ref[:RET_W, :])
               + _dot_tn(swat_ref[:, rb].astype(BF16), wout_ref[RET_W:, :]))
        y_ref[0, rb, :] = _layer_norm(DN_ALPHA * x[rb] + mix, lng_ref[...], lnb_ref[...])

    k_prev = kprev_ref[...]
    vt_prev = vprev_ref[...]
    for c in range(nchunks):
        r0 = c * CHUNK
        rows = slice(r0, r0 + CHUNK)
        if r0 % PROJ_ROWS == 0:
            nb = r0 // PROJ_ROWS + 1
            if nb < nblocks:
                dst = proj1_ref.at[(nb - 1) * PROJ_ROWS:nb * PROJ_ROWS]
                xrows = x[nb * PROJ_ROWS:(nb + 1) * PROJ_ROWS]
            else:
                dst, xrows = pb_ref.at[1 - slot], xn_ref[0]
            pending.extend(functools.partial(project_piece, dst, xrows, c0) for c0 in col_starts)
        budget = pieces_per_chunk
        for g in range(GROUPS):
            cols = slice(g * GROUP_W, (g + 1) * GROUP_W)
            q4 = proj(r0, CHUNK, _RQ + g * GROUP_W, _RQ + (g + 1) * GROUP_W)
            k4 = proj(r0, CHUNK, _RK + g * GROUP_W, _RK + (g + 1) * GROUP_W)
            v4 = proj(r0, CHUNK, _RV + g * GROUP_W, _RV + (g + 1) * GROUP_W).astype(BF16)
            k_bd = pair_diag((k4 * QK_SCALE).astype(BF16))
            sc = _dot_nt(head_stack(q4).astype(BF16), k_bd) * dm_ref[g]
            qd_st = head_stack(q4 * qw_ref[:, cols])
            lhs = jnp.concatenate([sc.astype(BF16), qd_st.astype(BF16)], axis=1)
            s_old = s_ref[g]
            rhs = jnp.concatenate([pair_diag(v4), s_old.astype(BF16)], axis=0)
            r = _dot(lhs, rhs)
            oret_ref[rows, cols] = jnp.where(lo_g, r[:CHUNK], r[CHUNK:])
            kd = (k4 * kw_ref[:, cols]).astype(BF16)
            s_ref[g] = gbd_ref[g] * s_old + blk * _dot_tn(kd, v4)
        k_cur = proj(r0, CHUNK, _SK, _SK + KV_W).astype(BF16)
        vt_cur = proj(r0, CHUNK, _SV, _SV + KV_W).T.astype(BF16)
        kk = jnp.concatenate([k_prev, k_cur], axis=0)
        vvt = jnp.concatenate([vt_prev, vt_cur], axis=1)
        q_parts = []
        for p in range(PAIRS):
            q2 = proj(r0, CHUNK, _SQ + p * LANES, _SQ + (p + 1) * LANES) * (QK_SCALE * LOG2E)
            zero = jnp.zeros_like(q2)
            q_parts += [jnp.where(lo, q2, zero).astype(BF16), jnp.where(lo, zero, q2).astype(BF16)]
        st = _dot_nt(kk, jnp.concatenate(q_parts, axis=0))
        if c == 0:
            st = st + bias_ref[jnp.minimum(step, 1)]
        else:
            st = st + bias_ref[1]
        sink = sink_ref[...] * LOG2E
        e_parts, den_parts = [], []
        for p in range(PAIRS):
            if budget > 0:
                emit_pieces(1)
                budget -= 1
            pc = slice(2 * p * CHUNK, 2 * (p + 1) * CHUNK)
            st_p, sink_p = st[:, pc], sink[:, pc]
            m = jnp.maximum(jnp.max(st_p, axis=0, keepdims=True), sink_p)
            e = jnp.exp2(st_p - m)
            den_parts.append(jnp.sum(e, axis=0, keepdims=True) + jnp.exp2(sink_p - m))
            e_parts.append(e.astype(BF16))
        if (r0 + CHUNK) % PROJ_ROWS == 0:
            emit_pieces(len(pending))
        den = jnp.concatenate(den_parts, axis=1)
        ot = _dot(vvt, jnp.concatenate(e_parts, axis=1)) * (1.0 / den)
        for p in range(PAIRS):
            c0 = 2 * p * CHUNK
            swat_ref[p * LANES:p * LANES + HEAD_DIM, rows] = ot[:HEAD_DIM, c0:c0 + CHUNK]
            swat_ref[p * LANES + HEAD_DIM:(p + 1) * LANES, rows] = ot[HEAD_DIM:, c0 + CHUNK:c0 + 2 * CHUNK]
        k_prev, vt_prev = k_cur, vt_cur
        if (r0 + CHUNK) % PROJ_ROWS == 0:
            finish_block(r0 // PROJ_ROWS)
    kprev_ref[...] = k_prev
    vprev_ref[...] = vt_prev

    @pl.when(step == nsteps - 1)
    def _():
        for h in range(RET_HEADS):
            g, i = divmod(h, 4)
            blk_h = s_ref[g, i * HEAD_DIM:(i + 1) * HEAD_DIM, (i // 2) * LANES:(i // 2 + 1) * LANES]
            if i % 2:
                blk_h = pltpu.roll(blk_h, HEAD_DIM, 1)
            st_ref[0, h] = blk_h[:, :HEAD_DIM]
        ck_ref[0] = proj(tq - WINDOW, WINDOW, _SK, _SK + KV_W)
        cv_ref[0] = proj(tq - WINDOW, WINDOW, _SV, _SV + KV_W)


def _mixer_prompt(x, w_in_p, w_out_p, lng, lnb, gnw, sink_st, consts, *, tq):
    bsz, seq, _ = x.shape
    assert seq % tq == 0 and tq % PROJ_ROWS == 0 and PROJ_ROWS % CHUNK == 0
    dm_st, qw, kw, gbd, blk, bias = consts
    body = functools.partial(_mixer_prompt_body, tq=tq)
    nsteps = seq // tq
    blocks_per_tile = tq // PROJ_ROWS

    def next_tile_head(b, s):
        nxt = jnp.minimum(b * nsteps + s + 1, bsz * nsteps - 1)
        return nxt // nsteps, (nxt % nsteps) * blocks_per_tile, 0

    return pl.pallas_call(
        body,
        grid=(bsz, nsteps),
        in_specs=[
            pl.BlockSpec((1, tq, D_MODEL), lambda b, s: (b, s, 0)),
            pl.BlockSpec((1, PROJ_ROWS, D_MODEL), next_tile_head),
            _const_spec((D_MODEL, IN_COLS)),
            _const_spec((2 * RET_W, D_MODEL)),
            _const_spec((1, D_MODEL)),
            _const_spec((1, D_MODEL)),
            _const_spec((1, RET_W)),
            _const_spec((1, SWA_HEADS * CHUNK)),
            _const_spec(dm_st.shape),
            _const_spec(qw.shape),
            _const_spec(kw.shape),
            _const_spec(gbd.shape),
            _const_spec(blk.shape),
            _const_spec(bias.shape),
        ],
        out_specs=[
            pl.BlockSpec((1, tq, D_MODEL), lambda b, s: (b, s, 0)),
            pl.BlockSpec((1, RET_HEADS, HEAD_DIM, HEAD_DIM), lambda b, s: (b, 0, 0, 0)),
            pl.BlockSpec((1, WINDOW, KV_W), lambda b, s: (b, 0, 0)),
            pl.BlockSpec((1, WINDOW, KV_W), lambda b, s: (b, 0, 0)),
        ],
        out_shape=[
            jax.ShapeDtypeStruct((bsz, seq, D_MODEL), F32),
            jax.ShapeDtypeStruct((bsz, RET_HEADS, HEAD_DIM, HEAD_DIM), F32),
            jax.ShapeDtypeStruct((bsz, WINDOW, KV_W), F32),
            jax.ShapeDtypeStruct((bsz, WINDOW, KV_W), F32),
        ],
        scratch_shapes=[
            pltpu.VMEM((2, PROJ_ROWS, IN_COLS), F32),
            pltpu.VMEM((tq - PROJ_ROWS, IN_COLS), F32),
            pltpu.VMEM((tq, RET_W), F32),
            pltpu.VMEM((tq, RET_W), BF16),
            pltpu.VMEM((SWA_W, tq), F32),
            pltpu.VMEM((GROUPS, GROUP_W, GROUP_W), F32),
            pltpu.VMEM((CHUNK, KV_W), BF16),
            pltpu.VMEM((KV_W, CHUNK), BF16),
        ],
        compiler_params=pltpu.CompilerParams(
            dimension_semantics=("arbitrary", "arbitrary"), vmem_limit_bytes=VMEM_LIMIT_BYTES),
        name="mixer_prompt",
    )(x, x, w_in_p, w_out_p, lng, lnb, gnw, sink_st, dm_st, qw, kw, gbd, blk, bias)


def _sample_constants(t):
    lg = _log_gamma()
    hh = np.arange(RET_HEADS)
    tt = np.arange(t, dtype=np.float64)
    row_h = np.repeat(hh, t)
    row_t = np.tile(tt, RET_HEADS)
    col_h = np.repeat(hh, HEAD_DIM)
    same = (row_h[:, None] == col_h[None, :]).astype(np.float64)
    mask_q1 = same * QK_SCALE
    mask_qw = same * np.exp((row_t + 1.0) * lg[row_h])[:, None]
    mask_kw = same * (np.exp((t - 1.0 - row_t) * lg[row_h]) * QK_SCALE)[:, None]
    diff = row_t[:, None] - row_t[None, :]
    same_h = row_h[:, None] == row_h[None, :]
    dm = np.where(same_h & (diff >= 0), np.exp(np.where(diff >= 0, diff, 0.0) * lg[row_h][:, None]), 0.0)
    g_t = np.repeat(np.exp(t * lg), HEAD_DIM)[:, None] * np.ones((1, HEAD_DIM))
    slopes = _alibi_slopes()
    j = np.arange(WINDOW + t)
    dist = row_t[:, None] + WINDOW - j[None, :]
    valid = (dist >= 0) & (dist <= WINDOW)
    bias = np.where(valid, -slopes[row_h][:, None] * dist, NEG_BIG)
    f = lambda a: jnp.asarray(a, F32)
    return f(mask_q1), f(mask_qw), f(mask_kw), f(dm), f(g_t), f(bias)


def _mixer_sample_body(x_ref, st_ref, ck_ref, cv_ref, win_ref, wout_ref,
                       lng_ref, lnb_ref, gnw_ref, sink_ref,
                       mq1_ref, mqw_ref, mkw_ref, dm_ref, gt_ref, bias_ref,
                       y_ref, sto_ref, cko_ref, cvo_ref,
                       proj_ref, vh_ref, qs_ref, oh_ref, os_ref, merged_ref, *, sb, t):
    ht = RET_HEADS * t
    nrows = sb * t
    x = x_ref[...]
    proj_ref[...] = _dot(x.astype(BF16), win_ref[...])
    lane = lax.broadcasted_iota(jnp.int32, (nrows, LANES), 1)
    lo = lane < HEAD_DIM
    for p in range(PAIRS):
        v2 = proj_ref[:, _RV + p * LANES:_RV + (p + 1) * LANES]
        vh_ref[2 * p] = v2[:, :HEAD_DIM]
        vh_ref[2 * p + 1] = pltpu.roll(v2, HEAD_DIM, 1)[:, :HEAD_DIM]
        q2 = proj_ref[:, _SQ + p * LANES:_SQ + (p + 1) * LANES] * QK_SCALE
        zero = jnp.zeros_like(q2)
        qs_ref[p] = jnp.where(lo, q2, zero)
        qs_ref[SWA_GROUP + p] = jnp.where(lo, zero, q2)
    oh_ref[...] = jnp.zeros_like(oh_ref)

    mq1 = mq1_ref[...]
    mqw = mqw_ref[...]
    mkw = mkw_ref[...]
    dm = dm_ref[...]
    g_t = gt_ref[...]
    bias = bias_ref[...]
    sink = sink_ref[...]

    def per_seq(b, carry):
        r0 = pl.multiple_of(b * t, t)
        rows = pl.ds(r0, t)
        q_b = proj_ref[rows, _RQ:_RQ + RET_W]
        k_b = proj_ref[rows, _RK:_RK + RET_W]
        q_rep = jnp.concatenate([q_b] * RET_HEADS, axis=0)
        k_rep = jnp.concatenate([k_b] * RET_HEADS, axis=0)
        v_st = vh_ref[:, rows, :].reshape(ht, HEAD_DIM).astype(BF16)
        s0 = st_ref[b]
        sc = _dot_nt((q_rep * mq1).astype(BF16), k_rep.astype(BF16)) * dm
        o = _dot((q_rep * mqw).astype(BF16), s0.astype(BF16)) + _dot(sc.astype(BF16), v_st)
        oh_ref[:, rows, 0:HEAD_DIM] = o.reshape(RET_HEADS, t, HEAD_DIM)
        sto_ref[b] = g_t * s0 + _dot_tn((k_rep * mkw).astype(BF16), v_st)
        k_new = proj_ref[rows, _SK:_SK + KV_W]
        v_new = proj_ref[rows, _SV:_SV + KV_W]
        k_old = ck_ref[b]
        v_old = cv_ref[b]
        kk = jnp.concatenate([k_old, k_new], axis=0)
        vv = jnp.concatenate([v_old, v_new], axis=0)
        q_s = qs_ref[:, rows, :].reshape(ht, LANES)
        s = _dot_nt(q_s.astype(BF16), kk.astype(BF16)) + bias
        m = jnp.maximum(jnp.max(s, axis=-1, keepdims=True), sink)
        e = jnp.exp(s - m)
        den = jnp.sum(e, axis=-1, keepdims=True) + jnp.exp(sink - m)
        o_s = _dot(e.astype(BF16), vv.astype(BF16)) / den
        os_ref[:, rows, :] = o_s.reshape(SWA_HEADS, t, LANES)
        cko_ref[b] = kk[t:]
        cvo_ref[b] = vv[t:]
        return carry

    lax.fori_loop(0, sb, per_seq, 0, unroll=8)

    for p in range(PAIRS):
        cols = slice(p * LANES, (p + 1) * LANES)
        o2 = oh_ref[2 * p] + pltpu.roll(oh_ref[2 * p + 1], HEAD_DIM, 1)
        gn = _pair_group_norm(o2, lo, gnw_ref[:, cols])
        gate = proj_ref[:, _RG + p * LANES:_RG + (p + 1) * LANES]
        merged_ref[:, cols] = (gate * jax.nn.sigmoid(gate) * gn).astype(BF16)
        merged_ref[:, RET_W + p * LANES:RET_W + (p + 1) * LANES] = jnp.where(
            lo, os_ref[p], os_ref[SWA_GROUP + p]).astype(BF16)
    mix = _dot(merged_ref[...], wout_ref[...])
    y_ref[...] = _layer_norm(DN_ALPHA * x + mix, lng_ref[...], lnb_ref[...])


def _mixer_sample(x, state, ck, cv, w_in_p, w_out_p, lng, lnb, gnw, sink_rows, consts, *, sb, t):
    rows_total = x.shape[0]
    nseq = rows_total // t
    assert nseq % sb == 0
    rows = sb * t
    weights = (w_in_p, w_out_p)
    body = functools.partial(_mixer_sample_body, sb=sb, t=t)
    return pl.pallas_call(
        body,
        grid=(nseq // sb,),
        in_specs=[
            pl.BlockSpec((rows, D_MODEL), lambda i: (i, 0)),
            pl.BlockSpec((sb, RET_W, HEAD_DIM), lambda i: (i, 0, 0)),
            pl.BlockSpec((sb, WINDOW, KV_W), lambda i: (i, 0, 0)),
            pl.BlockSpec((sb, WINDOW, KV_W), lambda i: (i, 0, 0)),
        ] + [_const_spec(w.shape) for w in weights] + [
            _const_spec(lng.shape), _const_spec(lnb.shape), _const_spec(gnw.shape),
            _const_spec(sink_rows.shape),
        ] + [_const_spec(c.shape) for c in consts],
        out_specs=[
            pl.BlockSpec((rows, D_MODEL), lambda i: (i, 0)),
            pl.BlockSpec((sb, RET_W, HEAD_DIM), lambda i: (i, 0, 0)),
            pl.BlockSpec((sb, WINDOW, KV_W), lambda i: (i, 0, 0)),
            pl.BlockSpec((sb, WINDOW, KV_W), lambda i: (i, 0, 0)),
        ],
        out_shape=[
            jax.ShapeDtypeStruct((rows_total, D_MODEL), F32),
            jax.ShapeDtypeStruct((nseq, RET_W, HEAD_DIM), F32),
            jax.ShapeDtypeStruct((nseq, WINDOW, KV_W), F32),
            jax.ShapeDtypeStruct((nseq, WINDOW, KV_W), F32),
        ],
        scratch_shapes=[
            pltpu.VMEM((rows, IN_COLS), F32),
            pltpu.VMEM((RET_HEADS, rows, HEAD_DIM), F32),
            pltpu.VMEM((SWA_HEADS, rows, LANES), F32),
            pltpu.VMEM((RET_HEADS, rows, LANES), F32),
            pltpu.VMEM((SWA_HEADS, rows, LANES), F32),
            pltpu.VMEM((rows, 2 * RET_W), BF16),
        ],
        compiler_params=pltpu.CompilerParams(
            dimension_semantics=("arbitrary",), vmem_limit_bytes=VMEM_LIMIT_BYTES),
        name="mixer_sample",
    )(x, state, ck, cv, *weights, lng, lnb, gnw, sink_rows, *consts)


def _prep_prompt_weights(w_in, w_out, sinks):
    sq = w_in[:, _SQ:_SQ + SWA_W].reshape(D_MODEL, SWA_HEADS, HEAD_DIM)
    sq = sq[:, jnp.asarray(_SWA_HEAD_ORDER), :].reshape(D_MODEL, SWA_W)
    w_in_p = jnp.concatenate([w_in[:, :_SQ], sq, w_in[:, _SK:]], axis=1).astype(BF16)
    wo_s = w_out[RET_W:].reshape(SWA_HEADS, HEAD_DIM, D_MODEL)
    wo_s = wo_s[jnp.asarray(_SWA_HEAD_ORDER)].reshape(SWA_W, D_MODEL)
    w_out_p = jnp.concatenate([w_out[:RET_W], wo_s], axis=0).astype(BF16)
    sink_st = jnp.repeat(sinks.astype(F32)[jnp.asarray(_SWA_HEAD_ORDER)], CHUNK)[None, :]
    return w_in_p, w_out_p, sink_st


def kernel(x_prompt, x_sample, state_ret, cache_swa_k, cache_swa_v, ln_gain, ln_bias, w_in,
           ret_gn_w, swa_sinks, w_out, ffn1_gate, ffn1_up, ffn1_down, ffn2_gate, ffn2_up, ffn2_down):
    assert ln_gain.shape[0] == DEPTH == 1
    bsz, seq, _ = x_prompt.shape
    nseq, t, _ = x_sample.shape
    lng = ln_gain[0].astype(F32).reshape(3, 1, D_MODEL)
    lnb = ln_bias[0].astype(F32).reshape(3, 1, D_MODEL)
    f1 = (ffn1_gate[0].astype(BF16), ffn1_up[0].astype(BF16), ffn1_down[0].astype(BF16))
    f2 = (ffn2_gate[0].astype(BF16), ffn2_up[0].astype(BF16), ffn2_down[0].astype(BF16))
    gnw = ret_gn_w[0].astype(F32)

    tm_p = min(1024, bsz * seq)
    tm_s = min(512, nseq * t)
    xp = x_prompt.reshape(bsz * seq, D_MODEL)
    xs = x_sample.reshape(nseq * t, D_MODEL)

    xp = _ffn_ln(xp, *f1, lng[0], lnb[0], tm=tm_p)
    w_in_p, w_out_p, sink_st = _prep_prompt_weights(w_in[0], w_out[0], swa_sinks[0])
    yp, st_p, ck_p, cv_p = _mixer_prompt(
        xp.reshape(bsz, seq, D_MODEL), w_in_p, w_out_p, lng[1], lnb[1], gnw.reshape(1, RET_W),
        sink_st, _prompt_constants(), tq=min(1024, seq))
    yp = _ffn_ln(yp.reshape(bsz * seq, D_MODEL), *f2, lng[2], lnb[2], tm=tm_p)

    xs = _ffn_ln(xs, *f1, lng[0], lnb[0], tm=tm_s)
    sink_rows = jnp.repeat(swa_sinks[0].astype(F32), t)[:, None]
    ys, st_s, ck_s, cv_s = _mixer_sample(
        xs, state_ret[0].astype(F32).reshape(nseq, RET_W, HEAD_DIM),
        cache_swa_k[0].reshape(nseq, WINDOW, KV_W), cache_swa_v[0].reshape(nseq, WINDOW, KV_W),
        w_in_p, w_out_p, lng[1], lnb[1], gnw.reshape(1, RET_W), sink_rows, _sample_constants(t),
        sb=min(16, nseq), t=t)
    ys = _ffn_ln(ys, *f2, lng[2], lnb[2], tm=tm_s)

    kv_shape = lambda n: (1, n, WINDOW, SWA_KV_HEADS, HEAD_DIM)
    return (yp.reshape(bsz, seq, D_MODEL), ys.reshape(nseq, t, D_MODEL),
            st_p[None], ck_p.reshape(kv_shape(bsz)), cv_p.reshape(kv_shape(bsz)),
            st_s.reshape(1, nseq, RET_HEADS, HEAD_DIM, HEAD_DIM),
            ck_s.reshape(kv_shape(nseq)), cv_s.reshape(kv_shape(nseq)))
```

```python
import functools

import numpy as np
import jax
import jax.numpy as jnp
from jax import lax
from jax.experimental import pallas as pl
from jax.experimental.pallas import tpu as pltpu

F32 = jnp.float32
BF16 = jnp.bfloat16

D_MODEL = 1024
HEAD_DIM = 64
RET_HEADS = 8
SWA_HEADS = 8
SWA_KV_HEADS = 2
SWA_GROUP = SWA_HEADS // SWA_KV_HEADS
WINDOW = 128
CHUNK = 128
D_FF = 2816
RET_W = RET_HEADS * HEAD_DIM
SWA_W = SWA_HEADS * HEAD_DIM
KV_W = SWA_KV_HEADS * HEAD_DIM
IN_COLS = 4 * RET_W + SWA_W + 2 * KV_W
LN_EPS = 1e-5
GN_EPS = 1e-5
DEPTH = 1
DN_ALPHA = (2.0 * DEPTH) ** 0.25
QK_SCALE = HEAD_DIM ** -0.5
NEG_BIG = -1e30
LOG2E = 1.4426950408889634

LANES = 128
PAIRS = RET_HEADS // 2
GROUPS = RET_HEADS // 4
GROUP_W = 4 * HEAD_DIM
PROJ_ROWS = 256
PROJ_COLS = 512
VMEM_LIMIT_BYTES = 56 * 1024 * 1024

_RQ, _RK, _RV, _RG, _SQ = 0, RET_W, 2 * RET_W, 3 * RET_W, 4 * RET_W
_SK = 4 * RET_W + SWA_W
_SV = _SK + KV_W

_SWA_PAIR_HEADS = [(p, SWA_GROUP + p) for p in range(PAIRS)]
_SWA_HEAD_ORDER = [h for pair in _SWA_PAIR_HEADS for h in pair]

_FF_CHUNKS = ((0, 2816),)
FFN_ROW_BLOCK = 256


def _log_gamma():
    h = np.arange(RET_HEADS, dtype=np.float64)
    return np.log1p(-np.exp2(-5.0 - h))


def _alibi_slopes():
    return np.exp2(-8.0 / SWA_HEADS * np.arange(1, SWA_HEADS + 1, dtype=np.float64))


def _layer_norm(z, g, b):
    mu = jnp.mean(z, axis=-1, keepdims=True)
    d = z - mu
    var = jnp.mean(d * d, axis=-1, keepdims=True)
    return d * lax.rsqrt(var + LN_EPS) * g + b


def _pair_group_norm(o, lo, gain):
    zero = jnp.zeros_like(o)
    s_lo = jnp.sum(jnp.where(lo, o, zero), axis=-1, keepdims=True)
    s_hi = jnp.sum(jnp.where(lo, zero, o), axis=-1, keepdims=True)
    d = o - jnp.where(lo, s_lo, s_hi) * (1.0 / HEAD_DIM)
    d2 = d * d
    v_lo = jnp.sum(jnp.where(lo, d2, zero), axis=-1, keepdims=True)
    v_hi = jnp.sum(jnp.where(lo, zero, d2), axis=-1, keepdims=True)
    var = jnp.where(lo, v_lo, v_hi) * (1.0 / HEAD_DIM)
    return d * lax.rsqrt(var + GN_EPS) * gain


def _dot(a, b):
    return jnp.dot(a, b, preferred_element_type=F32)


def _dot_nt(a, b):
    return lax.dot_general(a, b, (((1,), (1,)), ((), ())), preferred_element_type=F32)


def _dot_tn(a, b):
    return lax.dot_general(a, b, (((0,), (0,)), ((), ())), preferred_element_type=F32)


def _ffn_ln_body(x_ref, wg_ref, wu_ref, wd_ref, g_ref, b_ref, o_ref, *, row_block):
    for r0 in range(0, x_ref.shape[0], row_block):
        x = x_ref[r0:r0 + row_block, :]
        xb = x.astype(BF16)
        acc = None
        for c0, cw in _FF_CHUNKS:
            gate = _dot(xb, wg_ref[:, c0:c0 + cw])
            up = _dot(xb, wu_ref[:, c0:c0 + cw])
            hid = (gate * jax.nn.sigmoid(gate) * up).astype(BF16)
            part = _dot(hid, wd_ref[c0:c0 + cw, :])
            acc = part if acc is None else acc + part
        z = DN_ALPHA * x + 0.5 * acc
        o_ref[r0:r0 + row_block, :] = _layer_norm(z, g_ref[...], b_ref[...])


def _const_spec(shape):
    nd = len(shape)
    return pl.BlockSpec(shape, lambda *_: (0,) * nd, pipeline_mode=pl.Buffered(1))


def _ffn_ln(x, wg, wu, wd, g, b, *, tm):
    rows = x.shape[0]
    assert rows % tm == 0 and tm % FFN_ROW_BLOCK == 0
    return pl.pallas_call(
        functools.partial(_ffn_ln_body, row_block=FFN_ROW_BLOCK),
        grid=(rows // tm,),
        in_specs=[
            pl.BlockSpec((tm, D_MODEL), lambda i: (i, 0)),
            _const_spec((D_MODEL, D_FF)),
            _const_spec((D_MODEL, D_FF)),
            _const_spec((D_FF, D_MODEL)),
            _const_spec((1, D_MODEL)),
            _const_spec((1, D_MODEL)),
        ],
        out_specs=pl.BlockSpec((tm, D_MODEL), lambda i: (i, 0)),
        out_shape=jax.ShapeDtypeStruct((rows, D_MODEL), F32),
        compiler_params=pltpu.CompilerParams(
            dimension_semantics=("arbitrary",), vmem_limit_bytes=VMEM_LIMIT_BYTES),
        name="ffn_ln",
    )(x, wg, wu, wd, g, b)


def _prompt_constants():
    lg = _log_gamma()
    idx = np.arange(CHUNK, dtype=np.float64)
    diff = idx[:, None] - idx[None, :]
    dm = np.where(diff >= 0, np.exp(np.where(diff >= 0, diff, 0.0)[None] * lg[:, None, None]), 0.0)
    dm_st = np.stack([np.block([[dm[4 * g], dm[4 * g + 2]], [dm[4 * g + 1], dm[4 * g + 3]]])
                      for g in range(GROUPS)])
    qw = np.repeat(np.exp((idx + 1.0)[:, None] * lg[None, :]), HEAD_DIM, axis=1)
    kw = np.repeat(np.exp((CHUNK - 1.0 - idx)[:, None] * lg[None, :]), HEAD_DIM, axis=1) * QK_SCALE
    gc = np.exp(CHUNK * lg)
    blk = np.kron(np.eye(4), np.ones((HEAD_DIM, HEAD_DIM)))
    gbd = np.stack([blk * np.repeat(gc[4 * g:4 * g + 4], HEAD_DIM)[:, None] for g in range(GROUPS)])
    slopes = _alibi_slopes()
    i = np.arange(CHUNK)
    j = np.arange(2 * CHUNK)
    dist = i[:, None] + CHUNK - j[None, :]
    within = (dist >= 0) & (dist <= WINDOW)
    valid = [within & (j >= CHUNK)[None, :], within]
    bias = np.stack([np.concatenate([np.where(valid[v], -slopes[h] * dist * LOG2E, NEG_BIG).T
                                     for h in _SWA_HEAD_ORDER], axis=1) for v in range(2)])
    f = lambda a: jnp.asarray(a, F32)
    return f(dm_st), f(qw), f(kw), f(gbd), f(blk), f(bias)


def _mixer_prompt_body(x_ref, xn_ref, win_ref, wout_ref, lng_ref, lnb_ref, gnw_ref, sink_ref,
                       dm_ref, qw_ref, kw_ref, gbd_ref, blk_ref, bias_ref,
                       y_ref, st_ref, ck_ref, cv_ref,
                       pb_ref, proj1_ref, oret_ref, yret_ref, swat_ref, s_ref, kprev_ref, vprev_ref,
                       *, tq):
    step = pl.program_id(1)
    nsteps = pl.num_programs(1)
    flat = pl.program_id(0) * nsteps + step
    slot = lax.rem(flat, 2)
    nchunks = tq // CHUNK
    nblocks = tq // PROJ_ROWS
    col_starts = range(0, IN_COLS, PROJ_COLS)

    @pl.when(step == 0)
    def _():
        s_ref[...] = jnp.zeros_like(s_ref)
        kprev_ref[...] = jnp.zeros_like(kprev_ref)
        vprev_ref[...] = jnp.zeros_like(vprev_ref)

    x = x_ref[0]

    def project_piece(dst, xrows, c0):
        c1 = min(c0 + PROJ_COLS, IN_COLS)
        dst[:, c0:c1] = _dot(xrows.astype(BF16), win_ref[:, c0:c1])

    @pl.when(flat == 0)
    def _():
        for c0 in col_starts:
            project_piece(pb_ref.at[0], x[:PROJ_ROWS], c0)

    def proj(r0, nrows, c0, c1):
        if r0 < PROJ_ROWS:
            return pb_ref[slot, r0:r0 + nrows, c0:c1]
        return proj1_ref[r0 - PROJ_ROWS:r0 - PROJ_ROWS + nrows, c0:c1]

    pending = []

    def emit_pieces(n):
        for _ in range(min(n, len(pending))):
            pending.pop(0)()

    pieces_per_chunk = pl.cdiv(len(col_starts), PROJ_ROWS // CHUNK)

    lane = lax.broadcasted_iota(jnp.int32, (CHUNK, LANES), 1)
    lo = lane < HEAD_DIM
    lane_g = lax.broadcasted_iota(jnp.int32, (CHUNK, GROUP_W), 1)
    lo_g = jnp.bitwise_and(lane_g, LANES - 1) < HEAD_DIM
    blk = blk_ref[...]
    zero_b = jnp.zeros((CHUNK, LANES), BF16)

    def pair_diag(a):
        return jnp.concatenate([jnp.concatenate([a[:, :LANES], zero_b], axis=1),
                                jnp.concatenate([zero_b, a[:, LANES:]], axis=1)], axis=0)

    def head_stack(a):
        zero = jnp.zeros_like(a)
        return jnp.concatenate([jnp.where(lo_g, a, zero), jnp.where(lo_g, zero, a)], axis=0)

    lane_t = lax.broadcasted_iota(jnp.int32, (PROJ_ROWS, LANES), 1)
    lo_t = lane_t < HEAD_DIM

    def finish_block(kb):
        rb = slice(kb * PROJ_ROWS, (kb + 1) * PROJ_ROWS)
        for p in range(PAIRS):
            cols = slice(p * LANES, (p + 1) * LANES)
            gn = _pair_group_norm(oret_ref[rb, cols], lo_t, gnw_ref[:, cols])
            gate = proj(kb * PROJ_ROWS, PROJ_ROWS, _RG + p * LANES, _RG + (p + 1) * LANES)
            yret_ref[rb, cols] = (gate * jax.nn.sigmoid(gate) * gn).astype(BF16)
        mix = (_dot(yret_ref[rb, :], wout_ref[:RET_W, :])
               + _dot_tn(swat_ref[:, rb].astype(BF16), wout_ref[RET_W:, :]))
        y_ref[0, rb, :] = _layer_norm(DN_ALPHA * x[rb] + mix, lng_ref[...], lnb_ref[...])

    k_prev = kprev_ref[...]
    vt_prev = vprev_ref[...]
    for c in range(nchunks):
        r0 = c * CHUNK
        rows = slice(r0, r0 + CHUNK)
        if r0 % PROJ_ROWS == 0:
            nb = r0 // PROJ_ROWS + 1
            if nb < nblocks:
                dst = proj1_ref.at[(nb - 1) * PROJ_ROWS:nb * PROJ_ROWS]
                xrows = x[nb * PROJ_ROWS:(nb + 1) * PROJ_ROWS]
            else:
                dst, xrows = pb_ref.at[1 - slot], xn_ref[0]
            pending.extend(functools.partial(project_piece, dst, xrows, c0) for c0 in col_starts)
        budget = pieces_per_chunk
        for g in range(GROUPS):
            if budget > 0:
                emit_pieces(1)
                budget -= 1
            cols = slice(g * GROUP_W, (g + 1) * GROUP_W)
            q4 = proj(r0, CHUNK, _RQ + g * GROUP_W, _RQ + (g + 1) * GROUP_W)
            k4 = proj(r0, CHUNK, _RK + g * GROUP_W, _RK + (g + 1) * GROUP_W)
            v4 = proj(r0, CHUNK, _RV + g * GROUP_W, _RV + (g + 1) * GROUP_W).astype(BF16)
            k_bd = pair_diag((k4 * QK_SCALE).astype(BF16))
            sc = _dot_nt(head_stack(q4).astype(BF16), k_bd) * dm_ref[g]
            qd_st = head_stack(q4 * qw_ref[:, cols])
            lhs = jnp.concatenate([sc.astype(BF16), qd_st.astype(BF16)], axis=1)
            s_old = s_ref[g]
            rhs = jnp.concatenate([pair_diag(v4), s_old.astype(BF16)], axis=0)
            r = _dot(lhs, rhs)
            oret_ref[rows, cols] = jnp.where(lo_g, r[:CHUNK], r[CHUNK:])
            kd = (k4 * kw_ref[:, cols]).astype(BF16)
            s_ref[g] = gbd_ref[g] * s_old + blk * _dot_tn(kd, v4)
        k_cur = proj(r0, CHUNK, _SK, _SK + KV_W).astype(BF16)
        vt_cur = proj(r0, CHUNK, _SV, _SV + KV_W).T.astype(BF16)
        kk = jnp.concatenate([k_prev, k_cur], axis=0)
        vvt = jnp.concatenate([vt_prev, vt_cur], axis=1)
        q_parts = []
        for p in range(PAIRS):
            q2 = proj(r0, CHUNK, _SQ + p * LANES, _SQ + (p + 1) * LANES) * (QK_SCALE * LOG2E)
            zero = jnp.zeros_like(q2)
            q_parts += [jnp.where(lo, q2, zero).astype(BF16), jnp.where(lo, zero, q2).astype(BF16)]
        st = _dot_nt(kk, jnp.concatenate(q_parts, axis=0))
        if c == 0:
            st = st + bias_ref[jnp.minimum(step, 1)]
        else:
            st = st + bias_ref[1]
        sink = sink_ref[...] * LOG2E
        e_parts, den_parts = [], []
        for p in range(PAIRS):
            if budget > 0:
                emit_pieces(1)
                budget -= 1
            pc = slice(2 * p * CHUNK, 2 * (p + 1) * CHUNK)
            st_p, sink_p = st[:, pc], sink[:, pc]
            m = jnp.maximum(jnp.max(st_p, axis=0, keepdims=True), sink_p)
            e = jnp.exp2(st_p - m)
            den_parts.append(jnp.sum(e, axis=0, keepdims=True) + jnp.exp2(sink_p - m))
            e_parts.append(e.astype(BF16))
        if (r0 + CHUNK) % PROJ_ROWS == 0:
            emit_pieces(len(pending))
        den = jnp.concatenate(den_parts, axis=1)
        ot = _dot(vvt, jnp.concatenate(e_parts, axis=1)) * (1.0 / den)
        for p in range(PAIRS):
            c0 = 2 * p * CHUNK
            swat_ref[p * LANES:p * LANES + HEAD_DIM, rows] = ot[:HEAD_DIM, c0:c0 + CHUNK]
            swat_ref[p * LANES + HEAD_DIM:(p + 1) * LANES, rows] = ot[HEAD_DIM:, c0 + CHUNK:c0 + 2 * CHUNK]
        k_prev, vt_prev = k_cur, vt_cur
        if (r0 + CHUNK) % PROJ_ROWS == 0:
            finish_block(r0 // PROJ_ROWS)
    kprev_ref[...] = k_prev
    vprev_ref[...] = vt_prev

    @pl.when(step == nsteps - 1)
    def _():
        for h in range(RET_HEADS):
            g, i = divmod(h, 4)
            blk_h = s_ref[g, i * HEAD_DIM:(i + 1) * HEAD_DIM, (i // 2) * LANES:(i // 2 + 1) * LANES]
            if i % 2:
                blk_h = pltpu.roll(blk_h, HEAD_DIM, 1)
            st_ref[0, h] = blk_h[:, :HEAD_DIM]
        ck_ref[0] = proj(tq - WINDOW, WINDOW, _SK, _SK + KV_W)
        cv_ref[0] = proj(tq - WINDOW, WINDOW, _SV, _SV + KV_W)


def _mixer_prompt(x, w_in_p, w_out_p, lng, lnb, gnw, sink_st, consts, *, tq):
    bsz, seq, _ = x.shape
    assert seq % tq == 0 and tq % PROJ_ROWS == 0 and PROJ_ROWS % CHUNK == 0
    dm_st, qw, kw, gbd, blk, bias = consts
    body = functools.partial(_mixer_prompt_body, tq=tq)
    nsteps = seq // tq
    blocks_per_tile = tq // PROJ_ROWS

    def next_tile_head(b, s):
        nxt = jnp.minimum(b * nsteps + s + 1, bsz * nsteps - 1)
        return nxt // nsteps, (nxt % nsteps) * blocks_per_tile, 0

    return pl.pallas_call(
        body,
        grid=(bsz, nsteps),
        in_specs=[
            pl.BlockSpec((1, tq, D_MODEL), lambda b, s: (b, s, 0)),
            pl.BlockSpec((1, PROJ_ROWS, D_MODEL), next_tile_head),
            _const_spec((D_MODEL, IN_COLS)),
            _const_spec((2 * RET_W, D_MODEL)),
            _const_spec((1, D_MODEL)),
            _const_spec((1, D_MODEL)),
            _const_spec((1, RET_W)),
            _const_spec((1, SWA_HEADS * CHUNK)),
            _const_spec(dm_st.shape),
            _const_spec(qw.shape),
            _const_spec(kw.shape),
            _const_spec(gbd.shape),
            _const_spec(blk.shape),
            _const_spec(bias.shape),
        ],
        out_specs=[
            pl.BlockSpec((1, tq, D_MODEL), lambda b, s: (b, s, 0)),
            pl.BlockSpec((1, RET_HEADS, HEAD_DIM, HEAD_DIM), lambda b, s: (b, 0, 0, 0)),
            pl.BlockSpec((1, WINDOW, KV_W), lambda b, s: (b, 0, 0)),
            pl.BlockSpec((1, WINDOW, KV_W), lambda b, s: (b, 0, 0)),
        ],
        out_shape=[
            jax.ShapeDtypeStruct((bsz, seq, D_MODEL), F32),
            jax.ShapeDtypeStruct((bsz, RET_HEADS, HEAD_DIM, HEAD_DIM), F32),
            jax.ShapeDtypeStruct((bsz, WINDOW, KV_W), F32),
            jax.ShapeDtypeStruct((bsz, WINDOW, KV_W), F32),
        ],
        scratch_shapes=[
            pltpu.VMEM((2, PROJ_ROWS, IN_COLS), F32),
            pltpu.VMEM((tq - PROJ_ROWS, IN_COLS), F32),
            pltpu.VMEM((tq, RET_W), F32),
            pltpu.VMEM((tq, RET_W), BF16),
            pltpu.VMEM((SWA_W, tq), F32),
            pltpu.VMEM((GROUPS, GROUP_W, GROUP_W), F32),
            pltpu.VMEM((CHUNK, KV_W), BF16),
            pltpu.VMEM((KV_W, CHUNK), BF16),
        ],
        compiler_params=pltpu.CompilerParams(
            dimension_semantics=("arbitrary", "arbitrary"), vmem_limit_bytes=VMEM_LIMIT_BYTES),
        name="mixer_prompt",
    )(x, x, w_in_p, w_out_p, lng, lnb, gnw, sink_st, dm_st, qw, kw, gbd, blk, bias)


def _sample_constants(t):
    lg = _log_gamma()
    hh = np.arange(RET_HEADS)
    tt = np.arange(t, dtype=np.float64)
    row_h = np.repeat(hh, t)
    row_t = np.tile(tt, RET_HEADS)
    col_h = np.repeat(hh, HEAD_DIM)
    same = (row_h[:, None] == col_h[None, :]).astype(np.float64)
    mask_q1 = same * QK_SCALE
    mask_qw = same * np.exp((row_t + 1.0) * lg[row_h])[:, None]
    mask_kw = same * (np.exp((t - 1.0 - row_t) * lg[row_h]) * QK_SCALE)[:, None]
    diff = row_t[:, None] - row_t[None, :]
    same_h = row_h[:, None] == row_h[None, :]
    dm = np.where(same_h & (diff >= 0), np.exp(np.where(diff >= 0, diff, 0.0) * lg[row_h][:, None]), 0.0)
    g_t = np.repeat(np.exp(t * lg), HEAD_DIM)[:, None] * np.ones((1, HEAD_DIM))
    slopes = _alibi_slopes()
    j = np.arange(WINDOW + t)
    dist = row_t[:, None] + WINDOW - j[None, :]
    valid = (dist >= 0) & (dist <= WINDOW)
    bias = np.where(valid, -slopes[row_h][:, None] * dist, NEG_BIG)
    f = lambda a: jnp.asarray(a, F32)
    return f(mask_q1), f(mask_qw), f(mask_kw), f(dm), f(g_t), f(bias)


def _mixer_sample_body(x_ref, st_ref, ck_ref, cv_ref, win_ref, wout_ref,
                       lng_ref, lnb_ref, gnw_ref, sink_ref,
                       mq1_ref, mqw_ref, mkw_ref, dm_ref, gt_ref, bias_ref,
                       y_ref, sto_ref, cko_ref, cvo_ref,
                       proj_ref, vh_ref, qs_ref, oh_ref, os_ref, merged_ref, *, sb, t):
    ht = RET_HEADS * t
    nrows = sb * t
    x = x_ref[...]
    proj_ref[...] = _dot(x.astype(BF16), win_ref[...])
    lane = lax.broadcasted_iota(jnp.int32, (nrows, LANES), 1)
    lo = lane < HEAD_DIM
    for p in range(PAIRS):
        v2 = proj_ref[:, _RV + p * LANES:_RV + (p + 1) * LANES]
        vh_ref[2 * p] = v2[:, :HEAD_DIM]
        vh_ref[2 * p + 1] = pltpu.roll(v2, HEAD_DIM, 1)[:, :HEAD_DIM]
        q2 = proj_ref[:, _SQ + p * LANES:_SQ + (p + 1) * LANES] * QK_SCALE
        zero = jnp.zeros_like(q2)
        qs_ref[p] = jnp.where(lo, q2, zero)
        qs_ref[SWA_GROUP + p] = jnp.where(lo, zero, q2)
    oh_ref[...] = jnp.zeros_like(oh_ref)

    mq1 = mq1_ref[...]
    mqw = mqw_ref[...]
    mkw = mkw_ref[...]
    dm = dm_ref[...]
    g_t = gt_ref[...]
    bias = bias_ref[...]
    sink = sink_ref[...]

    def per_seq(b, carry):
        r0 = pl.multiple_of(b * t, t)
        rows = pl.ds(r0, t)
        q_b = proj_ref[rows, _RQ:_RQ + RET_W]
        k_b = proj_ref[rows, _RK:_RK + RET_W]
        q_rep = jnp.concatenate([q_b] * RET_HEADS, axis=0)
        k_rep = jnp.concatenate([k_b] * RET_HEADS, axis=0)
        v_st = vh_ref[:, rows, :].reshape(ht, HEAD_DIM).astype(BF16)
        s0 = st_ref[b]
        sc = _dot_nt((q_rep * mq1).astype(BF16), k_rep.astype(BF16)) * dm
        o = _dot((q_rep * mqw).astype(BF16), s0.astype(BF16)) + _dot(sc.astype(BF16), v_st)
        oh_ref[:, rows, 0:HEAD_DIM] = o.reshape(RET_HEADS, t, HEAD_DIM)
        sto_ref[b] = g_t * s0 + _dot_tn((k_rep * mkw).astype(BF16), v_st)
        k_new = proj_ref[rows, _SK:_SK + KV_W]
        v_new = proj_ref[rows, _SV:_SV + KV_W]
        k_old = ck_ref[b]
        v_old = cv_ref[b]
        kk = jnp.concatenate([k_old, k_new], axis=0)
        vv = jnp.concatenate([v_old, v_new], axis=0)
        q_s = qs_ref[:, rows, :].reshape(ht, LANES)
        s = _dot_nt(q_s.astype(BF16), kk.astype(BF16)) + bias
        m = jnp.maximum(jnp.max(s, axis=-1, keepdims=True), sink)
        e = jnp.exp(s - m)
        den = jnp.sum(e, axis=-1, keepdims=True) + jnp.exp(sink - m)
        o_s = _dot(e.astype(BF16), vv.astype(BF16)) / den
        os_ref[:, rows, :] = o_s.reshape(SWA_HEADS, t, LANES)
        cko_ref[b] = kk[t:]
        cvo_ref[b] = vv[t:]
        return carry

    lax.fori_loop(0, sb, per_seq, 0, unroll=16)

    for p in range(PAIRS):
        cols = slice(p * LANES, (p + 1) * LANES)
        o2 = oh_ref[2 * p] + pltpu.roll(oh_ref[2 * p + 1], HEAD_DIM, 1)
        gn = _pair_group_norm(o2, lo, gnw_ref[:, cols])
        gate = proj_ref[:, _RG + p * LANES:_RG + (p + 1) * LANES]
        merged_ref[:, cols] = (gate * jax.nn.sigmoid(gate) * gn).astype(BF16)
        merged_ref[:, RET_W + p * LANES:RET_W + (p + 1) * LANES] = jnp.where(
            lo, os_ref[p], os_ref[SWA_GROUP + p]).astype(BF16)
    mix = _dot(merged_ref[...], wout_ref[...])
    y_ref[...] = _layer_norm(DN_ALPHA * x + mix, lng_ref[...], lnb_ref[...])


def _mixer_sample(x, state, ck, cv, w_in_p, w_out_p, lng, lnb, gnw, sink_rows, consts, *, sb, t):
    rows_total = x.shape[0]
    nseq = rows_total // t
    assert nseq % sb == 0
    rows = sb * t
    weights = (w_in_p, w_out_p)
    body = functools.partial(_mixer_sample_body, sb=sb, t=t)
    return pl.pallas_call(
        body,
        grid=(nseq // sb,),
        in_specs=[
            pl.BlockSpec((rows, D_MODEL), lambda i: (i, 0)),
            pl.BlockSpec((sb, RET_W, HEAD_DIM), lambda i: (i, 0, 0)),
            pl.BlockSpec((sb, WINDOW, KV_W), lambda i: (i, 0, 0)),
            pl.BlockSpec((sb, WINDOW, KV_W), lambda i: (i, 0, 0)),
        ] + [_const_spec(w.shape) for w in weights] + [
            _const_spec(lng.shape), _const_spec(lnb.shape), _const_spec(gnw.shape),
            _const_spec(sink_rows.shape),
        ] + [_const_spec(c.shape) for c in consts],
        out_specs=[
            pl.BlockSpec((rows, D_MODEL), lambda i: (i, 0)),
            pl.BlockSpec((sb, RET_W, HEAD_DIM), lambda i: (i, 0, 0)),
            pl.BlockSpec((sb, WINDOW, KV_W), lambda i: (i, 0, 0)),
            pl.BlockSpec((sb, WINDOW, KV_W), lambda i: (i, 0, 0)),
        ],
        out_shape=[
            jax.ShapeDtypeStruct((rows_total, D_MODEL), F32),
            jax.ShapeDtypeStruct((nseq, RET_W, HEAD_DIM), F32),
            jax.ShapeDtypeStruct((nseq, WINDOW, KV_W), F32),
            jax.ShapeDtypeStruct((nseq, WINDOW, KV_W), F32),
        ],
        scratch_shapes=[
            pltpu.VMEM((rows, IN_COLS), F32),
            pltpu.VMEM((RET_HEADS, rows, HEAD_DIM), F32),
            pltpu.VMEM((SWA_HEADS, rows, LANES), F32),
            pltpu.VMEM((RET_HEADS, rows, LANES), F32),
            pltpu.VMEM((SWA_HEADS, rows, LANES), F32),
            pltpu.VMEM((rows, 2 * RET_W), BF16),
        ],
        compiler_params=pltpu.CompilerParams(
            dimension_semantics=("arbitrary",), vmem_limit_bytes=VMEM_LIMIT_BYTES),
        name="mixer_sample",
    )(x, state, ck, cv, *weights, lng, lnb, gnw, sink_rows, *consts)


def _prep_prompt_weights(w_in, w_out, sinks):
    sq = w_in[:, _SQ:_SQ + SWA_W].reshape(D_MODEL, SWA_HEADS, HEAD_DIM)
    sq = sq[:, jnp.asarray(_SWA_HEAD_ORDER), :].reshape(D_MODEL, SWA_W)
    w_in_p = jnp.concatenate([w_in[:, :_SQ], sq, w_in[:, _SK:]], axis=1).astype(BF16)
    wo_s = w_out[RET_W:].reshape(SWA_HEADS, HEAD_DIM, D_MODEL)
    wo_s = wo_s[jnp.asarray(_SWA_HEAD_ORDER)].reshape(SWA_W, D_MODEL)
    w_out_p = jnp.concatenate([w_out[:RET_W], wo_s], axis=0).astype(BF16)
    sink_st = jnp.repeat(sinks.astype(F32)[jnp.asarray(_SWA_HEAD_ORDER)], CHUNK)[None, :]
    return w_in_p, w_out_p, sink_st


def kernel(x_prompt, x_sample, state_ret, cache_swa_k, cache_swa_v, ln_gain, ln_bias, w_in,
           ret_gn_w, swa_sinks, w_out, ffn1_gate, ffn1_up, ffn1_down, ffn2_gate, ffn2_up, ffn2_down):
    assert ln_gain.shape[0] == DEPTH == 1
    bsz, seq, _ = x_prompt.shape
    nseq, t, _ = x_sample.shape
    lng = ln_gain[0].astype(F32).reshape(3, 1, D_MODEL)
    lnb = ln_bias[0].astype(F32).reshape(3, 1, D_MODEL)
    f1 = (ffn1_gate[0].astype(BF16), ffn1_up[0].astype(BF16), ffn1_down[0].astype(BF16))
    f2 = (ffn2_gate[0].astype(BF16), ffn2_up[0].astype(BF16), ffn2_down[0].astype(BF16))
    gnw = ret_gn_w[0].astype(F32)

    tm_p = min(1024, bsz * seq)
    tm_s = min(1024, nseq * t)
    xp = x_prompt.reshape(bsz * seq, D_MODEL)
    xs = x_sample.reshape(nseq * t, D_MODEL)

    xp = _ffn_ln(xp, *f1, lng[0], lnb[0], tm=tm_p)
    w_in_p, w_out_p, sink_st = _prep_prompt_weights(w_in[0], w_out[0], swa_sinks[0])
    yp, st_p, ck_p, cv_p = _mixer_prompt(
        xp.reshape(bsz, seq, D_MODEL), w_in_p, w_out_p, lng[1], lnb[1], gnw.reshape(1, RET_W),
        sink_st, _prompt_constants(), tq=min(1024, seq))
    yp = _ffn_ln(yp.reshape(bsz * seq, D_MODEL), *f2, lng[2], lnb[2], tm=tm_p)

    xs = _ffn_ln(xs, *f1, lng[0], lnb[0], tm=tm_s)
    sink_rows = jnp.repeat(swa_sinks[0].astype(F32), t)[:, None]
    ys, st_s, ck_s, cv_s = _mixer_sample(
        xs, state_ret[0].astype(F32).reshape(nseq, RET_W, HEAD_DIM),
        cache_swa_k[0].reshape(nseq, WINDOW, KV_W), cache_swa_v[0].reshape(nseq, WINDOW, KV_W),
        w_in_p, w_out_p, lng[1], lnb[1], gnw.reshape(1, RET_W), sink_rows, _sample_constants(t),
        sb=min(16, nseq), t=t)
    ys = _ffn_ln(ys, *f2, lng[2], lnb[2], tm=tm_s)

    kv_shape = lambda n: (1, n, WINDOW, SWA_KV_HEADS, HEAD_DIM)
    return (yp.reshape(bsz, seq, D_MODEL), ys.reshape(nseq, t, D_MODEL),
            st_p[None], ck_p.reshape(kv_shape(bsz)), cv_p.reshape(kv_shape(bsz)),
            st_s.reshape(1, nseq, RET_HEADS, HEAD_DIM, HEAD_DIM),
            ck_s.reshape(kv_shape(nseq)), cv_s.reshape(kv_shape(nseq)))
```

```python
import functools

import numpy as np
import jax
import jax.numpy as jnp
from jax import lax
from jax.experimental import pallas as pl
from jax.experimental.pallas import tpu as pltpu

F32 = jnp.float32
BF16 = jnp.bfloat16

D_MODEL = 1024
HEAD_DIM = 64
RET_HEADS = 8
SWA_HEADS = 8
SWA_KV_HEADS = 2
SWA_GROUP = SWA_HEADS // SWA_KV_HEADS
WINDOW = 128
CHUNK = 128
D_FF = 2816
RET_W = RET_HEADS * HEAD_DIM
SWA_W = SWA_HEADS * HEAD_DIM
KV_W = SWA_KV_HEADS * HEAD_DIM
IN_COLS = 4 * RET_W + SWA_W + 2 * KV_W
LN_EPS = 1e-5
GN_EPS = 1e-5
DEPTH = 1
DN_ALPHA = (2.0 * DEPTH) ** 0.25
QK_SCALE = HEAD_DIM ** -0.5
NEG_BIG = -1e30
LOG2E = 1.4426950408889634

LANES = 128
PAIRS = RET_HEADS // 2
GROUPS = RET_HEADS // 4
GROUP_W = 4 * HEAD_DIM
PROJ_ROWS = 256
PROJ_COLS = 512
VMEM_LIMIT_BYTES = 56 * 1024 * 1024

_RQ, _RK, _RV, _RG, _SQ = 0, RET_W, 2 * RET_W, 3 * RET_W, 4 * RET_W
_SK = 4 * RET_W + SWA_W
_SV = _SK + KV_W

_SWA_PAIR_HEADS = [(p, SWA_GROUP + p) for p in range(PAIRS)]
_SWA_HEAD_ORDER = [h for pair in _SWA_PAIR_HEADS for h in pair]

_FF_CHUNKS = ((0, 2816),)
FFN_ROW_BLOCK = 256


def _log_gamma():
    h = np.arange(RET_HEADS, dtype=np.float64)
    return np.log1p(-np.exp2(-5.0 - h))


def _alibi_slopes():
    return np.exp2(-8.0 / SWA_HEADS * np.arange(1, SWA_HEADS + 1, dtype=np.float64))


def _layer_norm(z, g, b):
    mu = jnp.mean(z, axis=-1, keepdims=True)
    d = z - mu
    var = jnp.mean(d * d, axis=-1, keepdims=True)
    return d * lax.rsqrt(var + LN_EPS) * g + b


def _pair_group_norm(o, lo, gain):
    zero = jnp.zeros_like(o)
    s_lo = jnp.sum(jnp.where(lo, o, zero), axis=-1, keepdims=True)
    s_hi = jnp.sum(jnp.where(lo, zero, o), axis=-1, keepdims=True)
    d = o - jnp.where(lo, s_lo, s_hi) * (1.0 / HEAD_DIM)
    d2 = d * d
    v_lo = jnp.sum(jnp.where(lo, d2, zero), axis=-1, keepdims=True)
    v_hi = jnp.sum(jnp.where(lo, zero, d2), axis=-1, keepdims=True)
    var = jnp.where(lo, v_lo, v_hi) * (1.0 / HEAD_DIM)
    return d * lax.rsqrt(var + GN_EPS) * gain


def _dot(a, b):
    return jnp.dot(a, b, preferred_element_type=F32)


def _dot_nt(a, b):
    return lax.dot_general(a, b, (((1,), (1,)), ((), ())), preferred_element_type=F32)


def _dot_tn(a, b):
    return lax.dot_general(a, b, (((0,), (0,)), ((), ())), preferred_element_type=F32)


def _ffn_ln_body(x_ref, wg_ref, wu_ref, wd_ref, g_ref, b_ref, o_ref, *, row_block):
    for r0 in range(0, x_ref.shape[0], row_block):
        x = x_ref[r0:r0 + row_block, :]
        xb = x.astype(BF16)
        acc = None
        for c0, cw in _FF_CHUNKS:
            gate = _dot(xb, wg_ref[:, c0:c0 + cw])
            up = _dot(xb, wu_ref[:, c0:c0 + cw])
            hid = (gate * jax.nn.sigmoid(gate) * up).astype(BF16)
            part = _dot(hid, wd_ref[c0:c0 + cw, :])
            acc = part if acc is None else acc + part
        z = DN_ALPHA * x + 0.5 * acc
        o_ref[r0:r0 + row_block, :] = _layer_norm(z, g_ref[...], b_ref[...])


def _const_spec(shape):
    nd = len(shape)
    return pl.BlockSpec(shape, lambda *_: (0,) * nd, pipeline_mode=pl.Buffered(1))


def _ffn_ln(x, wg, wu, wd, g, b, *, tm):
    rows = x.shape[0]
    assert rows % tm == 0 and tm % FFN_ROW_BLOCK == 0
    return pl.pallas_call(
        functools.partial(_ffn_ln_body, row_block=FFN_ROW_BLOCK),
        grid=(rows // tm,),
        in_specs=[
            pl.BlockSpec((tm, D_MODEL), lambda i: (i, 0)),
            _const_spec((D_MODEL, D_FF)),
            _const_spec((D_MODEL, D_FF)),
            _const_spec((D_FF, D_MODEL)),
            _const_spec((1, D_MODEL)),
            _const_spec((1, D_MODEL)),
        ],
        out_specs=pl.BlockSpec((tm, D_MODEL), lambda i: (i, 0)),
        out_shape=jax.ShapeDtypeStruct((rows, D_MODEL), F32),
        compiler_params=pltpu.CompilerParams(
            dimension_semantics=("arbitrary",), vmem_limit_bytes=VMEM_LIMIT_BYTES),
        name="ffn_ln",
    )(x, wg, wu, wd, g, b)


def _prompt_constants():
    lg = _log_gamma()
    idx = np.arange(CHUNK, dtype=np.float64)
    diff = idx[:, None] - idx[None, :]
    dm = np.where(diff >= 0, np.exp(np.where(diff >= 0, diff, 0.0)[None] * lg[:, None, None]), 0.0)
    dm_st = np.stack([np.block([[dm[4 * g], dm[4 * g + 2]], [dm[4 * g + 1], dm[4 * g + 3]]])
                      for g in range(GROUPS)])
    qw = np.repeat(np.exp((idx + 1.0)[:, None] * lg[None, :]), HEAD_DIM, axis=1)
    kw = np.repeat(np.exp((CHUNK - 1.0 - idx)[:, None] * lg[None, :]), HEAD_DIM, axis=1) * QK_SCALE
    gc = np.exp(CHUNK * lg)
    blk = np.kron(np.eye(4), np.ones((HEAD_DIM, HEAD_DIM)))
    gbd = np.stack([blk * np.repeat(gc[4 * g:4 * g + 4], HEAD_DIM)[:, None] for g in range(GROUPS)])
    slopes = _alibi_slopes()
    i = np.arange(CHUNK)
    j = np.arange(2 * CHUNK)
    dist = i[:, None] + CHUNK - j[None, :]
    within = (dist >= 0) & (dist <= WINDOW)
    valid = [within & (j >= CHUNK)[None, :], within]
    bias = np.stack([np.concatenate([np.where(valid[v], -slopes[h] * dist * LOG2E, NEG_BIG).T
                                     for h in _SWA_HEAD_ORDER], axis=1) for v in range(2)])
    f = lambda a: jnp.asarray(a, F32)
    return f(dm_st), f(qw), f(kw), f(gbd), f(blk), f(bias)


def _mixer_prompt_body(x_ref, xn_ref, win_ref, wout_ref, lng_ref, lnb_ref, gnw_ref, sink_ref,
                       dm_ref, qw_ref, kw_ref, gbd_ref, blk_ref, bias_ref,
                       y_ref, st_ref, ck_ref, cv_ref,
                       pb_ref, proj1_ref, oret_ref, yret_ref, swat_ref, s_ref, kprev_ref, vprev_ref,
                       *, tq):
    step = pl.program_id(1)
    nsteps = pl.num_programs(1)
    flat = pl.program_id(0) * nsteps + step
    slot = lax.rem(flat, 2)
    nchunks = tq // CHUNK
    nblocks = tq // PROJ_ROWS
    col_starts = range(0, IN_COLS, PROJ_COLS)

    @pl.when(step == 0)
    def _():
        s_ref[...] = jnp.zeros_like(s_ref)
        kprev_ref[...] = jnp.zeros_like(kprev_ref)
        vprev_ref[...] = jnp.zeros_like(vprev_ref)

    x = x_ref[0]

    def project_piece(dst, xrows, c0):
        c1 = min(c0 + PROJ_COLS, IN_COLS)
        dst[:, c0:c1] = _dot(xrows.astype(BF16), win_ref[:, c0:c1])

    @pl.when(flat == 0)
    def _():
        for c0 in col_starts:
            project_piece(pb_ref.at[0], x[:PROJ_ROWS], c0)

    def proj(r0, nrows, c0, c1):
        if r0 < PROJ_ROWS:
            return pb_ref[slot, r0:r0 + nrows, c0:c1]
        return proj1_ref[r0 - PROJ_ROWS:r0 - PROJ_ROWS + nrows, c0:c1]

    pending = []

    def emit_pieces(n):
        for _ in range(min(n, len(pending))):
            pending.pop(0)()

    pieces_per_chunk = pl.cdiv(len(col_starts), PROJ_ROWS // CHUNK)

    lane = lax.broadcasted_iota(jnp.int32, (CHUNK, LANES), 1)
    lo = lane < HEAD_DIM
    lane_g = lax.broadcasted_iota(jnp.int32, (CHUNK, GROUP_W), 1)
    lo_g = jnp.bitwise_and(lane_g, LANES - 1) < HEAD_DIM
    blk = blk_ref[...]
    zero_b = jnp.zeros((CHUNK, LANES), BF16)

    def pair_diag(a):
        return jnp.concatenate([jnp.concatenate([a[:, :LANES], zero_b], axis=1),
                                jnp.concatenate([zero_b, a[:, LANES:]], axis=1)], axis=0)

    def head_stack(a):
        zero = jnp.zeros_like(a)
        return jnp.concatenate([jnp.where(lo_g, a, zero), jnp.where(lo_g, zero, a)], axis=0)

    lane_t = lax.broadcasted_iota(jnp.int32, (PROJ_ROWS, LANES), 1)
    lo_t = lane_t < HEAD_DIM

    def finish_block(kb):
        rb = slice(kb * PROJ_ROWS, (kb + 1) * PROJ_ROWS)
        for p in range(PAIRS):
            cols = slice(p * LANES, (p + 1) * LANES)
            gn = _pair_group_norm(oret_ref[rb, cols], lo_t, gnw_ref[:, cols])
            gate = proj(kb * PROJ_ROWS, PROJ_ROWS, _RG + p * LANES, _RG + (p + 1) * LANES)
            yret_ref[rb, cols] = (gate * jax.nn.sigmoid(gate) * gn).astype(BF16)
        mix = (_dot(yret_ref[rb, :], wout_ref[:RET_W, :])
               + _dot_tn(swat_ref[:, rb].astype(BF16), wout_ref[RET_W:, :]))
        y_ref[0, rb, :] = _layer_norm(DN_ALPHA * x[rb] + mix, lng_ref[...], lnb_ref[...])

    k_prev = kprev_ref[...]
    vt_prev = vprev_ref[...]
    for c in range(nchunks):
        r0 = c * CHUNK
        rows = slice(r0, r0 + CHUNK)
        if r0 % PROJ_ROWS == 0:
            nb = r0 // PROJ_ROWS + 1
            if nb < nblocks:
                dst = proj1_ref.at[(nb - 1) * PROJ_ROWS:nb * PROJ_ROWS]
                xrows = x[nb * PROJ_ROWS:(nb + 1) * PROJ_ROWS]
            else:
                dst, xrows = pb_ref.at[1 - slot], xn_ref[0]
            pending.extend(functools.partial(project_piece, dst, xrows, c0) for c0 in col_starts)
        budget = pieces_per_chunk
        for g in range(GROUPS):
            if budget > 0:
                emit_pieces(1)
                budget -= 1
            cols = slice(g * GROUP_W, (g + 1) * GROUP_W)
            q4 = proj(r0, CHUNK, _RQ + g * GROUP_W, _RQ + (g + 1) * GROUP_W)
            k4 = proj(r0, CHUNK, _RK + g * GROUP_W, _RK + (g + 1) * GROUP_W)
            v4 = proj(r0, CHUNK, _RV + g * GROUP_W, _RV + (g + 1) * GROUP_W).astype(BF16)
            k_bd = pair_diag((k4 * QK_SCALE).astype(BF16))
            sc = _dot_nt(head_stack(q4).astype(BF16), k_bd) * dm_ref[g]
            qd_st = head_stack(q4 * qw_ref[:, cols])
            lhs = jnp.concatenate([sc.astype(BF16), qd_st.astype(BF16)], axis=1)
            s_old = s_ref[g]
            rhs = jnp.concatenate([pair_diag(v4), s_old.astype(BF16)], axis=0)
            r = _dot(lhs, rhs)
            oret_ref[rows, cols] = jnp.where(lo_g, r[:CHUNK], r[CHUNK:])
            kd = (k4 * kw_ref[:, cols]).astype(BF16)
            s_ref[g] = gbd_ref[g] * s_old + blk * _dot_tn(kd, v4)
        k_cur = proj(r0, CHUNK, _SK, _SK + KV_W).astype(BF16)
        vt_cur = proj(r0, CHUNK, _SV, _SV + KV_W).T.astype(BF16)
        kk = jnp.concatenate([k_prev, k_cur], axis=0)
        vvt = jnp.concatenate([vt_prev, vt_cur], axis=1)
        q_parts = []
        for p in range(PAIRS):
            q2 = proj(r0, CHUNK, _SQ + p * LANES, _SQ + (p + 1) * LANES) * (QK_SCALE * LOG2E)
            zero = jnp.zeros_like(q2)
            q_parts += [jnp.where(lo, q2, zero).astype(BF16), jnp.where(lo, zero, q2).astype(BF16)]
        st = _dot_nt(kk, jnp.concatenate(q_parts, axis=0))
        if c == 0:
            st = st + bias_ref[jnp.minimum(step, 1)]
        else:
            st = st + bias_ref[1]
        sink = sink_ref[...] * LOG2E
        e_parts, den_parts = [], []
        for p in range(PAIRS):
            if budget > 0:
                emit_pieces(1)
                budget -= 1
            pc = slice(2 * p * CHUNK, 2 * (p + 1) * CHUNK)
            st_p, sink_p = st[:, pc], sink[:, pc]
            m = jnp.maximum(jnp.max(st_p, axis=0, keepdims=True), sink_p)
            e = jnp.exp2(st_p - m)
            den_parts.append(jnp.sum(e, axis=0, keepdims=True) + jnp.exp2(sink_p - m))
            e_parts.append(e.astype(BF16))
        if (r0 + CHUNK) % PROJ_ROWS == 0:
            emit_pieces(len(pending))
        den = jnp.concatenate(den_parts, axis=1)
        ot = _dot(vvt, jnp.concatenate(e_parts, axis=1)) * (1.0 / den)
        for p in range(PAIRS):
            c0 = 2 * p * CHUNK
            swat_ref[p * LANES:p * LANES + HEAD_DIM, rows] = ot[:HEAD_DIM, c0:c0 + CHUNK]
            swat_ref[p * LANES + HEAD_DIM:(p + 1) * LANES, rows] = ot[HEAD_DIM:, c0 + CHUNK:c0 + 2 * CHUNK]
        k_prev, vt_prev = k_cur, vt_cur
        if (r0 + CHUNK) % PROJ_ROWS == 0:
            finish_block(r0 // PROJ_ROWS)
    kprev_ref[...] = k_prev
    vprev_ref[...] = vt_prev

    @pl.when(step == nsteps - 1)
    def _():
        for h in range(RET_HEADS):
            g, i = divmod(h, 4)
            blk_h = s_ref[g, i * HEAD_DIM:(i + 1) * HEAD_DIM, (i // 2) * LANES:(i // 2 + 1) * LANES]
            if i % 2:
                blk_h = pltpu.roll(blk_h, HEAD_DIM, 1)
            st_ref[0, h] = blk_h[:, :HEAD_DIM]
        ck_ref[0] = proj(tq - WINDOW, WINDOW, _SK, _SK + KV_W).T
        cv_ref[0] = proj(tq - WINDOW, WINDOW, _SV, _SV + KV_W).T


def _mixer_prompt(x, w_in_p, w_out_p, lng, lnb, gnw, sink_st, consts, *, tq):
    bsz, seq, _ = x.shape
    assert seq % tq == 0 and tq % PROJ_ROWS == 0 and PROJ_ROWS % CHUNK == 0
    dm_st, qw, kw, gbd, blk, bias = consts
    body = functools.partial(_mixer_prompt_body, tq=tq)
    nsteps = seq // tq
    blocks_per_tile = tq // PROJ_ROWS

    def next_tile_head(b, s):
        nxt = jnp.minimum(b * nsteps + s + 1, bsz * nsteps - 1)
        return nxt // nsteps, (nxt % nsteps) * blocks_per_tile, 0

    return pl.pallas_call(
        body,
        grid=(bsz, nsteps),
        in_specs=[
            pl.BlockSpec((1, tq, D_MODEL), lambda b, s: (b, s, 0)),
            pl.BlockSpec((1, PROJ_ROWS, D_MODEL), next_tile_head),
            _const_spec((D_MODEL, IN_COLS)),
            _const_spec((2 * RET_W, D_MODEL)),
            _const_spec((1, D_MODEL)),
            _const_spec((1, D_MODEL)),
            _const_spec((1, RET_W)),
            _const_spec((1, SWA_HEADS * CHUNK)),
            _const_spec(dm_st.shape),
            _const_spec(qw.shape),
            _const_spec(kw.shape),
            _const_spec(gbd.shape),
            _const_spec(blk.shape),
            _const_spec(bias.shape),
        ],
        out_specs=[
            pl.BlockSpec((1, tq, D_MODEL), lambda b, s: (b, s, 0)),
            pl.BlockSpec((1, RET_HEADS, HEAD_DIM, HEAD_DIM), lambda b, s: (b, 0, 0, 0)),
            pl.BlockSpec((1, KV_W, WINDOW), lambda b, s: (b, 0, 0)),
            pl.BlockSpec((1, KV_W, WINDOW), lambda b, s: (b, 0, 0)),
        ],
        out_shape=[
            jax.ShapeDtypeStruct((bsz, seq, D_MODEL), F32),
            jax.ShapeDtypeStruct((bsz, RET_HEADS, HEAD_DIM, HEAD_DIM), F32),
            jax.ShapeDtypeStruct((bsz, KV_W, WINDOW), F32),
            jax.ShapeDtypeStruct((bsz, KV_W, WINDOW), F32),
        ],
        scratch_shapes=[
            pltpu.VMEM((2, PROJ_ROWS, IN_COLS), F32),
            pltpu.VMEM((tq - PROJ_ROWS, IN_COLS), F32),
            pltpu.VMEM((tq, RET_W), F32),
            pltpu.VMEM((tq, RET_W), BF16),
            pltpu.VMEM((SWA_W, tq), F32),
            pltpu.VMEM((GROUPS, GROUP_W, GROUP_W), F32),
            pltpu.VMEM((CHUNK, KV_W), BF16),
            pltpu.VMEM((KV_W, CHUNK), BF16),
        ],
        compiler_params=pltpu.CompilerParams(
            dimension_semantics=("arbitrary", "arbitrary"), vmem_limit_bytes=VMEM_LIMIT_BYTES),
        name="mixer_prompt",
    )(x, x, w_in_p, w_out_p, lng, lnb, gnw, sink_st, dm_st, qw, kw, gbd, blk, bias)


def _sample_constants(t):
    lg = _log_gamma()
    hh = np.arange(RET_HEADS)
    tt = np.arange(t, dtype=np.float64)
    row_h = np.repeat(hh, t)
    row_t = np.tile(tt, RET_HEADS)
    col_h = np.repeat(hh, HEAD_DIM)
    same = (row_h[:, None] == col_h[None, :]).astype(np.float64)
    mask_q1 = same * QK_SCALE
    mask_qw = same * np.exp((row_t + 1.0) * lg[row_h])[:, None]
    mask_kw = same * (np.exp((t - 1.0 - row_t) * lg[row_h]) * QK_SCALE)[:, None]
    diff = row_t[:, None] - row_t[None, :]
    same_h = row_h[:, None] == row_h[None, :]
    dm = np.where(same_h & (diff >= 0), np.exp(np.where(diff >= 0, diff, 0.0) * lg[row_h][:, None]), 0.0)
    g_t = np.repeat(np.exp(t * lg), HEAD_DIM)[:, None] * np.ones((1, HEAD_DIM))
    slopes = _alibi_slopes()
    j = np.arange(WINDOW + t)
    dist = row_t[:, None] + WINDOW - j[None, :]
    valid = (dist >= 0) & (dist <= WINDOW)
    bias = np.where(valid, -slopes[row_h][:, None] * dist, NEG_BIG)
    f = lambda a: jnp.asarray(a, F32)
    return f(mask_q1), f(mask_qw), f(mask_kw), f(dm), f(g_t), f(bias)


def _mixer_sample_body(x_ref, st_ref, ck_ref, cv_ref, win_ref, wout_ref,
                       lng_ref, lnb_ref, gnw_ref, sink_ref,
                       mq1_ref, mqw_ref, mkw_ref, dm_ref, gt_ref, bias_ref,
                       y_ref, sto_ref, cko_ref, cvo_ref,
                       proj_ref, vh_ref, qs_ref, oh_ref, os_ref, merged_ref, *, sb, t):
    ht = RET_HEADS * t
    nrows = sb * t
    x = x_ref[...]
    proj_ref[...] = _dot(x.astype(BF16), win_ref[...])
    lane = lax.broadcasted_iota(jnp.int32, (nrows, LANES), 1)
    lo = lane < HEAD_DIM
    for p in range(PAIRS):
        v2 = proj_ref[:, _RV + p * LANES:_RV + (p + 1) * LANES]
        vh_ref[2 * p] = v2[:, :HEAD_DIM]
        vh_ref[2 * p + 1] = pltpu.roll(v2, HEAD_DIM, 1)[:, :HEAD_DIM]
        q2 = proj_ref[:, _SQ + p * LANES:_SQ + (p + 1) * LANES] * QK_SCALE
        zero = jnp.zeros_like(q2)
        qs_ref[p] = jnp.where(lo, q2, zero)
        qs_ref[SWA_GROUP + p] = jnp.where(lo, zero, q2)
    oh_ref[...] = jnp.zeros_like(oh_ref)
    knt = proj_ref[:, _SK:_SK + KV_W].T
    vnt = proj_ref[:, _SV:_SV + KV_W].T
    keep = lax.broadcasted_iota(jnp.int32, (KV_W, WINDOW), 1) < WINDOW - t

    mq1 = mq1_ref[...]
    mqw = mqw_ref[...]
    mkw = mkw_ref[...]
    dm = dm_ref[...]
    g_t = gt_ref[...]
    bias = bias_ref[...]
    sink = sink_ref[...]

    def per_seq(b, carry):
        r0 = pl.multiple_of(b * t, t)
        rows = pl.ds(r0, t)
        q_b = proj_ref[rows, _RQ:_RQ + RET_W]
        k_b = proj_ref[rows, _RK:_RK + RET_W]
        q_rep = jnp.concatenate([q_b] * RET_HEADS, axis=0)
        k_rep = jnp.concatenate([k_b] * RET_HEADS, axis=0)
        v_st = vh_ref[:, rows, :].reshape(ht, HEAD_DIM).astype(BF16)
        s0 = st_ref[b]
        sc = _dot_nt((q_rep * mq1).astype(BF16), k_rep.astype(BF16)) * dm
        o = _dot((q_rep * mqw).astype(BF16), s0.astype(BF16)) + _dot(sc.astype(BF16), v_st)
        oh_ref[:, rows, 0:HEAD_DIM] = o.reshape(RET_HEADS, t, HEAD_DIM)
        sto_ref[b] = g_t * s0 + _dot_tn((k_rep * mkw).astype(BF16), v_st)
        k_new = proj_ref[rows, _SK:_SK + KV_W]
        v_new = proj_ref[rows, _SV:_SV + KV_W]
        kt_old = ck_ref[b]
        vt_old = cv_ref[b]
        q_s = qs_ref[:, rows, :].reshape(ht, LANES).astype(BF16)
        s = jnp.concatenate([_dot(q_s, kt_old.astype(BF16)),
                             _dot_nt(q_s, k_new.astype(BF16))], axis=1) + bias
        m = jnp.maximum(jnp.max(s, axis=-1, keepdims=True), sink)
        e = jnp.exp(s - m)
        den = jnp.sum(e, axis=-1, keepdims=True) + jnp.exp(sink - m)
        eb = e.astype(BF16)
        o_s = (_dot_nt(eb[:, :WINDOW], vt_old.astype(BF16))
               + _dot(eb[:, WINDOW:], v_new.astype(BF16))) / den
        os_ref[:, rows, :] = o_s.reshape(SWA_HEADS, t, LANES)
        shift = WINDOW - t - r0
        cko_ref[b] = jnp.where(keep, pltpu.roll(kt_old, WINDOW - t, 1), pltpu.roll(knt, shift, 1))
        cvo_ref[b] = jnp.where(keep, pltpu.roll(vt_old, WINDOW - t, 1), pltpu.roll(vnt, shift, 1))
        return carry

    lax.fori_loop(0, sb, per_seq, 0, unroll=16)

    for p in range(PAIRS):
        cols = slice(p * LANES, (p + 1) * LANES)
        o2 = oh_ref[2 * p] + pltpu.roll(oh_ref[2 * p + 1], HEAD_DIM, 1)
        gn = _pair_group_norm(o2, lo, gnw_ref[:, cols])
        gate = proj_ref[:, _RG + p * LANES:_RG + (p + 1) * LANES]
        merged_ref[:, cols] = (gate * jax.nn.sigmoid(gate) * gn).astype(BF16)
        merged_ref[:, RET_W + p * LANES:RET_W + (p + 1) * LANES] = jnp.where(
            lo, os_ref[p], os_ref[SWA_GROUP + p]).astype(BF16)
    mix = _dot(merged_ref[...], wout_ref[...])
    y_ref[...] = _layer_norm(DN_ALPHA * x + mix, lng_ref[...], lnb_ref[...])


def _mixer_sample(x, state, ck, cv, w_in_p, w_out_p, lng, lnb, gnw, sink_rows, consts, *, sb, t):
    rows_total = x.shape[0]
    nseq = rows_total // t
    assert nseq % sb == 0 and sb * t == LANES and t % 8 == 0
    rows = sb * t
    weights = (w_in_p, w_out_p)
    body = functools.partial(_mixer_sample_body, sb=sb, t=t)
    return pl.pallas_call(
        body,
        grid=(nseq // sb,),
        in_specs=[
            pl.BlockSpec((rows, D_MODEL), lambda i: (i, 0)),
            pl.BlockSpec((sb, RET_W, HEAD_DIM), lambda i: (i, 0, 0)),
            pl.BlockSpec((sb, KV_W, WINDOW), lambda i: (i, 0, 0)),
            pl.BlockSpec((sb, KV_W, WINDOW), lambda i: (i, 0, 0)),
        ] + [_const_spec(w.shape) for w in weights] + [
            _const_spec(lng.shape), _const_spec(lnb.shape), _const_spec(gnw.shape),
            _const_spec(sink_rows.shape),
        ] + [_const_spec(c.shape) for c in consts],
        out_specs=[
            pl.BlockSpec((rows, D_MODEL), lambda i: (i, 0)),
            pl.BlockSpec((sb, RET_W, HEAD_DIM), lambda i: (i, 0, 0)),
            pl.BlockSpec((sb, KV_W, WINDOW), lambda i: (i, 0, 0)),
            pl.BlockSpec((sb, KV_W, WINDOW), lambda i: (i, 0, 0)),
        ],
        out_shape=[
            jax.ShapeDtypeStruct((rows_total, D_MODEL), F32),
            jax.ShapeDtypeStruct((nseq, RET_W, HEAD_DIM), F32),
            jax.ShapeDtypeStruct((nseq, KV_W, WINDOW), F32),
            jax.ShapeDtypeStruct((nseq, KV_W, WINDOW), F32),
        ],
        scratch_shapes=[
            pltpu.VMEM((rows, IN_COLS), F32),
            pltpu.VMEM((RET_HEADS, rows, HEAD_DIM), F32),
            pltpu.VMEM((SWA_HEADS, rows, LANES), F32),
            pltpu.VMEM((RET_HEADS, rows, LANES), F32),
            pltpu.VMEM((SWA_HEADS, rows, LANES), F32),
            pltpu.VMEM((rows, 2 * RET_W), BF16),
        ],
        compiler_params=pltpu.CompilerParams(
            dimension_semantics=("arbitrary",), vmem_limit_bytes=VMEM_LIMIT_BYTES),
        name="mixer_sample",
    )(x, state, ck, cv, *weights, lng, lnb, gnw, sink_rows, *consts)


def _prep_prompt_weights(w_in, w_out, sinks):
    sq = w_in[:, _SQ:_SQ + SWA_W].reshape(D_MODEL, SWA_HEADS, HEAD_DIM)
    sq = sq[:, jnp.asarray(_SWA_HEAD_ORDER), :].reshape(D_MODEL, SWA_W)
    w_in_p = jnp.concatenate([w_in[:, :_SQ], sq, w_in[:, _SK:]], axis=1).astype(BF16)
    wo_s = w_out[RET_W:].reshape(SWA_HEADS, HEAD_DIM, D_MODEL)
    wo_s = wo_s[jnp.asarray(_SWA_HEAD_ORDER)].reshape(SWA_W, D_MODEL)
    w_out_p = jnp.concatenate([w_out[:RET_W], wo_s], axis=0).astype(BF16)
    sink_st = jnp.repeat(sinks.astype(F32)[jnp.asarray(_SWA_HEAD_ORDER)], CHUNK)[None, :]
    return w_in_p, w_out_p, sink_st


def kernel(x_prompt, x_sample, state_ret, cache_swa_k, cache_swa_v, ln_gain, ln_bias, w_in,
           ret_gn_w, swa_sinks, w_out, ffn1_gate, ffn1_up, ffn1_down, ffn2_gate, ffn2_up, ffn2_down):
    assert ln_gain.shape[0] == DEPTH == 1
    bsz, seq, _ = x_prompt.shape
    nseq, t, _ = x_sample.shape
    lng = ln_gain[0].astype(F32).reshape(3, 1, D_MODEL)
    lnb = ln_bias[0].astype(F32).reshape(3, 1, D_MODEL)
    f1 = (ffn1_gate[0].astype(BF16), ffn1_up[0].astype(BF16), ffn1_down[0].astype(BF16))
    f2 = (ffn2_gate[0].astype(BF16), ffn2_up[0].astype(BF16), ffn2_down[0].astype(BF16))
    gnw = ret_gn_w[0].astype(F32)

    to_dims_major = lambda c: jnp.transpose(c, (0, 2, 3, 1)).reshape(c.shape[0], KV_W, WINDOW)
    to_pos_major = lambda c: jnp.transpose(
        c.reshape(c.shape[0], SWA_KV_HEADS, HEAD_DIM, WINDOW), (0, 3, 1, 2))[None]

    tm_p = min(2048, bsz * seq)
    tm_s = min(512, nseq * t)
    xp = x_prompt.reshape(bsz * seq, D_MODEL)
    xs = x_sample.reshape(nseq * t, D_MODEL)

    xp = _ffn_ln(xp, *f1, lng[0], lnb[0], tm=tm_p)
    w_in_p, w_out_p, sink_st = _prep_prompt_weights(w_in[0], w_out[0], swa_sinks[0])
    yp, st_p, ck_p, cv_p = _mixer_prompt(
        xp.reshape(bsz, seq, D_MODEL), w_in_p, w_out_p, lng[1], lnb[1], gnw.reshape(1, RET_W),
        sink_st, _prompt_constants(), tq=min(1024, seq))
    yp = _ffn_ln(yp.reshape(bsz * seq, D_MODEL), *f2, lng[2], lnb[2], tm=tm_p)

    xs = _ffn_ln(xs, *f1, lng[0], lnb[0], tm=tm_s)
    sink_rows = jnp.repeat(swa_sinks[0].astype(F32), t)[:, None]
    ys, st_s, ck_s, cv_s = _mixer_sample(
        xs, state_ret[0].astype(F32).reshape(nseq, RET_W, HEAD_DIM),
        to_dims_major(cache_swa_k[0]), to_dims_major(cache_swa_v[0]),
        w_in_p, w_out_p, lng[1], lnb[1], gnw.reshape(1, RET_W), sink_rows, _sample_constants(t),
        sb=LANES // t, t=t)
    ys = _ffn_ln(ys, *f2, lng[2], lnb[2], tm=tm_s)

    return (yp.reshape(bsz, seq, D_MODEL), ys.reshape(nseq, t, D_MODEL),
            st_p[None], to_pos_major(ck_p), to_pos_major(cv_p),
            st_s.reshape(1, nseq, RET_HEADS, HEAD_DIM, HEAD_DIM),
            to_pos_major(ck_s), to_pos_major(cv_s))
```

```python
import functools

import numpy as np
import jax
import jax.numpy as jnp
from jax import lax
from jax.experimental import pallas as pl
from jax.experimental.pallas import tpu as pltpu

F32 = jnp.float32
BF16 = jnp.bfloat16

D_MODEL = 1024
HEAD_DIM = 64
RET_HEADS = 8
SWA_HEADS = 8
SWA_KV_HEADS = 2
SWA_GROUP = SWA_HEADS // SWA_KV_HEADS
WINDOW = 128
CHUNK = 128
D_FF = 2816
RET_W = RET_HEADS * HEAD_DIM
SWA_W = SWA_HEADS * HEAD_DIM
KV_W = SWA_KV_HEADS * HEAD_DIM
IN_COLS = 4 * RET_W + SWA_W + 2 * KV_W
LN_EPS = 1e-5
GN_EPS = 1e-5
DEPTH = 1
DN_ALPHA = (2.0 * DEPTH) ** 0.25
QK_SCALE = HEAD_DIM ** -0.5
NEG_BIG = -1e30
LOG2E = 1.4426950408889634

LANES = 128
PAIRS = RET_HEADS // 2
GROUPS = RET_HEADS // 4
GROUP_W = 4 * HEAD_DIM
PROJ_ROWS = 256
PROJ_COLS = 512
VMEM_LIMIT_BYTES = 56 * 1024 * 1024

_RQ, _RK, _RV, _RG, _SQ = 0, RET_W, 2 * RET_W, 3 * RET_W, 4 * RET_W
_SK = 4 * RET_W + SWA_W
_SV = _SK + KV_W

_SWA_PAIR_HEADS = [(p, SWA_GROUP + p) for p in range(PAIRS)]
_SWA_HEAD_ORDER = [h for pair in _SWA_PAIR_HEADS for h in pair]

_FF_CHUNKS = ((0, 2816),)
FFN_ROW_BLOCK = 256


def _log_gamma():
    h = np.arange(RET_HEADS, dtype=np.float64)
    return np.log1p(-np.exp2(-5.0 - h))


def _alibi_slopes():
    return np.exp2(-8.0 / SWA_HEADS * np.arange(1, SWA_HEADS + 1, dtype=np.float64))


def _layer_norm(z, g, b):
    mu = jnp.mean(z, axis=-1, keepdims=True)
    d = z - mu
    var = jnp.mean(d * d, axis=-1, keepdims=True)
    return d * lax.rsqrt(var + LN_EPS) * g + b


def _pair_group_norm(o, lo, gain):
    zero = jnp.zeros_like(o)
    s_lo = jnp.sum(jnp.where(lo, o, zero), axis=-1, keepdims=True)
    s_hi = jnp.sum(jnp.where(lo, zero, o), axis=-1, keepdims=True)
    d = o - jnp.where(lo, s_lo, s_hi) * (1.0 / HEAD_DIM)
    d2 = d * d
    v_lo = jnp.sum(jnp.where(lo, d2, zero), axis=-1, keepdims=True)
    v_hi = jnp.sum(jnp.where(lo, zero, d2), axis=-1, keepdims=True)
    var = jnp.where(lo, v_lo, v_hi) * (1.0 / HEAD_DIM)
    return d * lax.rsqrt(var + GN_EPS) * gain


def _dot(a, b):
    return jnp.dot(a, b, preferred_element_type=F32)


def _dot_nt(a, b):
    return lax.dot_general(a, b, (((1,), (1,)), ((), ())), preferred_element_type=F32)


def _dot_tn(a, b):
    return lax.dot_general(a, b, (((0,), (0,)), ((), ())), preferred_element_type=F32)


def _ffn_ln_body(x_ref, wg_ref, wu_ref, wd_ref, g_ref, b_ref, o_ref, *, row_block):
    for r0 in range(0, x_ref.shape[0], row_block):
        x = x_ref[r0:r0 + row_block, :]
        xb = x.astype(BF16)
        acc = None
        for c0, cw in _FF_CHUNKS:
            gate = _dot(xb, wg_ref[:, c0:c0 + cw])
            up = _dot(xb, wu_ref[:, c0:c0 + cw])
            hid = (gate * jax.nn.sigmoid(gate) * up).astype(BF16)
            part = _dot(hid, wd_ref[c0:c0 + cw, :])
            acc = part if acc is None else acc + part
        z = DN_ALPHA * x + 0.5 * acc
        o_ref[r0:r0 + row_block, :] = _layer_norm(z, g_ref[...], b_ref[...])


def _const_spec(shape):
    nd = len(shape)
    return pl.BlockSpec(shape, lambda *_: (0,) * nd, pipeline_mode=pl.Buffered(1))


def _ffn_ln(x, wg, wu, wd, g, b, *, tm):
    rows = x.shape[0]
    assert rows % tm == 0 and tm % FFN_ROW_BLOCK == 0
    return pl.pallas_call(
        functools.partial(_ffn_ln_body, row_block=FFN_ROW_BLOCK),
        grid=(rows // tm,),
        in_specs=[
            pl.BlockSpec((tm, D_MODEL), lambda i: (i, 0)),
            _const_spec((D_MODEL, D_FF)),
            _const_spec((D_MODEL, D_FF)),
            _const_spec((D_FF, D_MODEL)),
            _const_spec((1, D_MODEL)),
            _const_spec((1, D_MODEL)),
        ],
        out_specs=pl.BlockSpec((tm, D_MODEL), lambda i: (i, 0)),
        out_shape=jax.ShapeDtypeStruct((rows, D_MODEL), F32),
        compiler_params=pltpu.CompilerParams(
            dimension_semantics=("arbitrary",), vmem_limit_bytes=VMEM_LIMIT_BYTES),
        name="ffn_ln",
    )(x, wg, wu, wd, g, b)


def _prompt_constants():
    lg = _log_gamma()
    idx = np.arange(CHUNK, dtype=np.float64)
    diff = idx[:, None] - idx[None, :]
    dm = np.where(diff >= 0, np.exp(np.where(diff >= 0, diff, 0.0)[None] * lg[:, None, None]), 0.0)
    dm_st = np.stack([np.block([[dm[4 * g], dm[4 * g + 2]], [dm[4 * g + 1], dm[4 * g + 3]]])
                      for g in range(GROUPS)])
    qw = np.repeat(np.exp((idx + 1.0)[:, None] * lg[None, :]), HEAD_DIM, axis=1)
    kw = np.repeat(np.exp((CHUNK - 1.0 - idx)[:, None] * lg[None, :]), HEAD_DIM, axis=1) * QK_SCALE
    gc = np.exp(CHUNK * lg)
    blk = np.kron(np.eye(4), np.ones((HEAD_DIM, HEAD_DIM)))
    gbd = np.stack([blk * np.repeat(gc[4 * g:4 * g + 4], HEAD_DIM)[:, None] for g in range(GROUPS)])
    slopes = _alibi_slopes()
    i = np.arange(CHUNK)
    j = np.arange(2 * CHUNK)
    dist = i[:, None] + CHUNK - j[None, :]
    within = (dist >= 0) & (dist <= WINDOW)
    valid = [within & (j >= CHUNK)[None, :], within]
    bias = np.stack([np.concatenate([np.where(valid[v], -slopes[h] * dist * LOG2E, NEG_BIG).T
                                     for h in _SWA_HEAD_ORDER], axis=1) for v in range(2)])
    f = lambda a: jnp.asarray(a, F32)
    return f(dm_st), f(qw), f(kw), f(gbd), f(blk), f(bias)


def _mixer_prompt_body(x_ref, xn_ref, win_ref, wout_ref, lng_ref, lnb_ref, gnw_ref, sink_ref,
                       dm_ref, qw_ref, kw_ref, gbd_ref, blk_ref, bias_ref,
                       y_ref, st_ref, ck_ref, cv_ref,
                       pb_ref, proj1_ref, oret_ref, yret_ref, swat_ref, s_ref, kprev_ref, vprev_ref,
                       *, tq):
    step = pl.program_id(1)
    nsteps = pl.num_programs(1)
    flat = pl.program_id(0) * nsteps + step
    slot = lax.rem(flat, 2)
    nchunks = tq // CHUNK
    nblocks = tq // PROJ_ROWS
    col_starts = range(0, IN_COLS, PROJ_COLS)

    @pl.when(step == 0)
    def _():
        s_ref[...] = jnp.zeros_like(s_ref)
        kprev_ref[...] = jnp.zeros_like(kprev_ref)
        vprev_ref[...] = jnp.zeros_like(vprev_ref)

    x = x_ref[0]

    def project_piece(dst, xrows, c0):
        c1 = min(c0 + PROJ_COLS, IN_COLS)
        dst[:, c0:c1] = _dot(xrows.astype(BF16), win_ref[:, c0:c1])

    @pl.when(flat == 0)
    def _():
        for c0 in col_starts:
            project_piece(pb_ref.at[0], x[:PROJ_ROWS], c0)

    def proj(r0, nrows, c0, c1):
        if r0 < PROJ_ROWS:
            return pb_ref[slot, r0:r0 + nrows, c0:c1]
        return proj1_ref[r0 - PROJ_ROWS:r0 - PROJ_ROWS + nrows, c0:c1]

    pending = []

    def emit_pieces(n):
        for _ in range(min(n, len(pending))):
            pending.pop(0)()

    pieces_per_chunk = pl.cdiv(len(col_starts), PROJ_ROWS // CHUNK)

    lane = lax.broadcasted_iota(jnp.int32, (CHUNK, LANES), 1)
    lo = lane < HEAD_DIM
    lane_g = lax.broadcasted_iota(jnp.int32, (CHUNK, GROUP_W), 1)
    lo_g = jnp.bitwise_and(lane_g, LANES - 1) < HEAD_DIM
    blk = blk_ref[...]
    zero_b = jnp.zeros((CHUNK, LANES), BF16)

    def pair_diag(a):
        return jnp.concatenate([jnp.concatenate([a[:, :LANES], zero_b], axis=1),
                                jnp.concatenate([zero_b, a[:, LANES:]], axis=1)], axis=0)

    def head_stack(a):
        zero = jnp.zeros_like(a)
        return jnp.concatenate([jnp.where(lo_g, a, zero), jnp.where(lo_g, zero, a)], axis=0)

    lane_t = lax.broadcasted_iota(jnp.int32, (PROJ_ROWS, LANES), 1)
    lo_t = lane_t < HEAD_DIM

    def finish_block(kb):
        rb = slice(kb * PROJ_ROWS, (kb + 1) * PROJ_ROWS)
        for p in range(PAIRS):
            cols = slice(p * LANES, (p + 1) * LANES)
            gn = _pair_group_norm(oret_ref[rb, cols], lo_t, gnw_ref[:, cols])
            gate = proj(kb * PROJ_ROWS, PROJ_ROWS, _RG + p * LANES, _RG + (p + 1) * LANES)
            yret_ref[rb, cols] = (gate * jax.nn.sigmoid(gate) * gn).astype(BF16)
        mix = (_dot(yret_ref[rb, :], wout_ref[:RET_W, :])
               + _dot_tn(swat_ref[:, rb].astype(BF16), wout_ref[RET_W:, :]))
        y_ref[0, rb, :] = _layer_norm(DN_ALPHA * x[rb] + mix, lng_ref[...], lnb_ref[...])

    k_prev = kprev_ref[...]
    vt_prev = vprev_ref[...]
    for c in range(nchunks):
        r0 = c * CHUNK
        rows = slice(r0, r0 + CHUNK)
        if r0 % PROJ_ROWS == 0:
            nb = r0 // PROJ_ROWS + 1
            if nb < nblocks:
                dst = proj1_ref.at[(nb - 1) * PROJ_ROWS:nb * PROJ_ROWS]
                xrows = x[nb * PROJ_ROWS:(nb + 1) * PROJ_ROWS]
            else:
                dst, xrows = pb_ref.at[1 - slot], xn_ref[0]
            pending.extend(functools.partial(project_piece, dst, xrows, c0) for c0 in col_starts)
        budget = pieces_per_chunk
        for g in range(GROUPS):
            if budget > 0:
                emit_pieces(1)
                budget -= 1
            cols = slice(g * GROUP_W, (g + 1) * GROUP_W)
            q4 = proj(r0, CHUNK, _RQ + g * GROUP_W, _RQ + (g + 1) * GROUP_W)
            k4 = proj(r0, CHUNK, _RK + g * GROUP_W, _RK + (g + 1) * GROUP_W)
            v4 = proj(r0, CHUNK, _RV + g * GROUP_W, _RV + (g + 1) * GROUP_W).astype(BF16)
            k_bd = pair_diag((k4 * QK_SCALE).astype(BF16))
            sc = _dot_nt(head_stack(q4).astype(BF16), k_bd) * dm_ref[g]
            qd_st = head_stack(q4 * qw_ref[:, cols])
            lhs = jnp.concatenate([sc.astype(BF16), qd_st.astype(BF16)], axis=1)
            s_old = s_ref[g]
            rhs = jnp.concatenate([pair_diag(v4), s_old.astype(BF16)], axis=0)
            r = _dot(lhs, rhs)
            oret_ref[rows, cols] = jnp.where(lo_g, r[:CHUNK], r[CHUNK:])
            kd = (k4 * kw_ref[:, cols]).astype(BF16)
            s_ref[g] = gbd_ref[g] * s_old + blk * _dot_tn(kd, v4)
        k_cur = proj(r0, CHUNK, _SK, _SK + KV_W).astype(BF16)
        vt_cur = proj(r0, CHUNK, _SV, _SV + KV_W).T.astype(BF16)
        kk = jnp.concatenate([k_prev, k_cur], axis=0)
        vvt = jnp.concatenate([vt_prev, vt_cur], axis=1)
        q_parts = []
        for p in range(PAIRS):
            q2 = proj(r0, CHUNK, _SQ + p * LANES, _SQ + (p + 1) * LANES) * (QK_SCALE * LOG2E)
            zero = jnp.zeros_like(q2)
            q_parts += [jnp.where(lo, q2, zero).astype(BF16), jnp.where(lo, zero, q2).astype(BF16)]
        st = _dot_nt(kk, jnp.concatenate(q_parts, axis=0))
        if c == 0:
            st = st + bias_ref[jnp.minimum(step, 1)]
        else:
            st = st + bias_ref[1]
        sink = sink_ref[...] * LOG2E
        e_parts, den_parts = [], []
        for p in range(PAIRS):
            if budget > 0:
                emit_pieces(1)
                budget -= 1
            pc = slice(2 * p * CHUNK, 2 * (p + 1) * CHUNK)
            st_p, sink_p = st[:, pc], sink[:, pc]
            m = jnp.maximum(jnp.max(st_p, axis=0, keepdims=True), sink_p)
            e = jnp.exp2(st_p - m)
            den_parts.append(jnp.sum(e, axis=0, keepdims=True) + jnp.exp2(sink_p - m))
            e_parts.append(e.astype(BF16))
        if (r0 + CHUNK) % PROJ_ROWS == 0:
            emit_pieces(len(pending))
        den = jnp.concatenate(den_parts, axis=1)
        ot = _dot(vvt, jnp.concatenate(e_parts, axis=1)) * (1.0 / den)
        for p in range(PAIRS):
            c0 = 2 * p * CHUNK
            swat_ref[p * LANES:p * LANES + HEAD_DIM, rows] = ot[:HEAD_DIM, c0:c0 + CHUNK]
            swat_ref[p * LANES + HEAD_DIM:(p + 1) * LANES, rows] = ot[HEAD_DIM:, c0 + CHUNK:c0 + 2 * CHUNK]
        k_prev, vt_prev = k_cur, vt_cur
        if (r0 + CHUNK) % PROJ_ROWS == 0:
            finish_block(r0 // PROJ_ROWS)
    kprev_ref[...] = k_prev
    vprev_ref[...] = vt_prev

    @pl.when(step == nsteps - 1)
    def _():
        for h in range(RET_HEADS):
            g, i = divmod(h, 4)
            blk_h = s_ref[g, i * HEAD_DIM:(i + 1) * HEAD_DIM, (i // 2) * LANES:(i // 2 + 1) * LANES]
            if i % 2:
                blk_h = pltpu.roll(blk_h, HEAD_DIM, 1)
            st_ref[0, h] = blk_h[:, :HEAD_DIM]
        ck_ref[0] = proj(tq - WINDOW, WINDOW, _SK, _SK + KV_W).T
        cv_ref[0] = proj(tq - WINDOW, WINDOW, _SV, _SV + KV_W).T


def _mixer_prompt(x, w_in_p, w_out_p, lng, lnb, gnw, sink_st, consts, *, tq):
    bsz, seq, _ = x.shape
    assert seq % tq == 0 and tq % PROJ_ROWS == 0 and PROJ_ROWS % CHUNK == 0
    dm_st, qw, kw, gbd, blk, bias = consts
    body = functools.partial(_mixer_prompt_body, tq=tq)
    nsteps = seq // tq
    blocks_per_tile = tq // PROJ_ROWS

    def next_tile_head(b, s):
        nxt = jnp.minimum(b * nsteps + s + 1, bsz * nsteps - 1)
        return nxt // nsteps, (nxt % nsteps) * blocks_per_tile, 0

    return pl.pallas_call(
        body,
        grid=(bsz, nsteps),
        in_specs=[
            pl.BlockSpec((1, tq, D_MODEL), lambda b, s: (b, s, 0)),
            pl.BlockSpec((1, PROJ_ROWS, D_MODEL), next_tile_head),
            _const_spec((D_MODEL, IN_COLS)),
            _const_spec((2 * RET_W, D_MODEL)),
            _const_spec((1, D_MODEL)),
            _const_spec((1, D_MODEL)),
            _const_spec((1, RET_W)),
            _const_spec((1, SWA_HEADS * CHUNK)),
            _const_spec(dm_st.shape),
            _const_spec(qw.shape),
            _const_spec(kw.shape),
            _const_spec(gbd.shape),
            _const_spec(blk.shape),
            _const_spec(bias.shape),
        ],
        out_specs=[
            pl.BlockSpec((1, tq, D_MODEL), lambda b, s: (b, s, 0)),
            pl.BlockSpec((1, RET_HEADS, HEAD_DIM, HEAD_DIM), lambda b, s: (b, 0, 0, 0)),
            pl.BlockSpec((1, KV_W, WINDOW), lambda b, s: (b, 0, 0)),
            pl.BlockSpec((1, KV_W, WINDOW), lambda b, s: (b, 0, 0)),
        ],
        out_shape=[
            jax.ShapeDtypeStruct((bsz, seq, D_MODEL), F32),
            jax.ShapeDtypeStruct((bsz, RET_HEADS, HEAD_DIM, HEAD_DIM), F32),
            jax.ShapeDtypeStruct((bsz, KV_W, WINDOW), F32),
            jax.ShapeDtypeStruct((bsz, KV_W, WINDOW), F32),
        ],
        scratch_shapes=[
            pltpu.VMEM((2, PROJ_ROWS, IN_COLS), F32),
            pltpu.VMEM((tq - PROJ_ROWS, IN_COLS), F32),
            pltpu.VMEM((tq, RET_W), F32),
            pltpu.VMEM((tq, RET_W), BF16),
            pltpu.VMEM((SWA_W, tq), F32),
            pltpu.VMEM((GROUPS, GROUP_W, GROUP_W), F32),
            pltpu.VMEM((CHUNK, KV_W), BF16),
            pltpu.VMEM((KV_W, CHUNK), BF16),
        ],
        compiler_params=pltpu.CompilerParams(
            dimension_semantics=("arbitrary", "arbitrary"), vmem_limit_bytes=VMEM_LIMIT_BYTES),
        name="mixer_prompt",
    )(x, x, w_in_p, w_out_p, lng, lnb, gnw, sink_st, dm_st, qw, kw, gbd, blk, bias)


def _sample_constants(t):
    lg = _log_gamma()
    hh = np.arange(RET_HEADS)
    tt = np.arange(t, dtype=np.float64)
    row_h = np.repeat(hh, t)
    row_t = np.tile(tt, RET_HEADS)
    col_h = np.repeat(hh, HEAD_DIM)
    same = (row_h[:, None] == col_h[None, :]).astype(np.float64)
    mask_q1 = same * QK_SCALE
    mask_qw = same * np.exp((row_t + 1.0) * lg[row_h])[:, None]
    mask_kw = same * (np.exp((t - 1.0 - row_t) * lg[row_h]) * QK_SCALE)[:, None]
    diff = row_t[:, None] - row_t[None, :]
    same_h = row_h[:, None] == row_h[None, :]
    dm = np.where(same_h & (diff >= 0), np.exp(np.where(diff >= 0, diff, 0.0) * lg[row_h][:, None]), 0.0)
    g_t = np.repeat(np.exp(t * lg), HEAD_DIM)[:, None] * np.ones((1, HEAD_DIM))
    slopes = _alibi_slopes()
    j = np.arange(WINDOW + t)
    dist = row_t[:, None] + WINDOW - j[None, :]
    valid = (dist >= 0) & (dist <= WINDOW)
    bias = np.where(valid, -slopes[row_h][:, None] * dist, NEG_BIG)
    f = lambda a: jnp.asarray(a, F32)
    return f(mask_q1), f(mask_qw), f(mask_kw), f(dm), f(g_t), f(bias)


def _mixer_sample_body(x_ref, st_ref, ck_ref, cv_ref, win_ref, wout_ref,
                       lng_ref, lnb_ref, gnw_ref, sink_ref,
                       mq1_ref, mqw_ref, mkw_ref, dm_ref, gt_ref, bias_ref,
                       y_ref, sto_ref, cko_ref, cvo_ref,
                       proj_ref, vh_ref, qs_ref, oh_ref, os_ref, merged_ref, *, sb, t):
    ht = RET_HEADS * t
    nrows = sb * t
    x = x_ref[...]
    proj_ref[...] = _dot(x.astype(BF16), win_ref[...])
    lane = lax.broadcasted_iota(jnp.int32, (nrows, LANES), 1)
    lo = lane < HEAD_DIM
    for p in range(PAIRS):
        v2 = proj_ref[:, _RV + p * LANES:_RV + (p + 1) * LANES]
        vh_ref[2 * p] = v2[:, :HEAD_DIM]
        vh_ref[2 * p + 1] = pltpu.roll(v2, HEAD_DIM, 1)[:, :HEAD_DIM]
        q2 = proj_ref[:, _SQ + p * LANES:_SQ + (p + 1) * LANES] * QK_SCALE
        zero = jnp.zeros_like(q2)
        qs_ref[p] = jnp.where(lo, q2, zero)
        qs_ref[SWA_GROUP + p] = jnp.where(lo, zero, q2)
    oh_ref[...] = jnp.zeros_like(oh_ref)
    knt = proj_ref[:, _SK:_SK + KV_W].T
    vnt = proj_ref[:, _SV:_SV + KV_W].T
    keep = lax.broadcasted_iota(jnp.int32, (KV_W, WINDOW), 1) < WINDOW - t

    mq1 = mq1_ref[...]
    mqw = mqw_ref[...]
    mkw = mkw_ref[...]
    dm = dm_ref[...]
    g_t = gt_ref[...]
    bias = bias_ref[...]
    sink = sink_ref[...]

    def per_seq(b, carry):
        r0 = pl.multiple_of(b * t, t)
        rows = pl.ds(r0, t)
        q_b = proj_ref[rows, _RQ:_RQ + RET_W]
        k_b = proj_ref[rows, _RK:_RK + RET_W]
        q_rep = jnp.concatenate([q_b] * RET_HEADS, axis=0)
        k_rep = jnp.concatenate([k_b] * RET_HEADS, axis=0)
        v_st = vh_ref[:, rows, :].reshape(ht, HEAD_DIM).astype(BF16)
        s0 = st_ref[b]
        sc = _dot_nt((q_rep * mq1).astype(BF16), k_rep.astype(BF16)) * dm
        o = _dot((q_rep * mqw).astype(BF16), s0.astype(BF16)) + _dot(sc.astype(BF16), v_st)
        oh_ref[:, rows, 0:HEAD_DIM] = o.reshape(RET_HEADS, t, HEAD_DIM)
        sto_ref[b] = g_t * s0 + _dot_tn((k_rep * mkw).astype(BF16), v_st)
        k_new = proj_ref[rows, _SK:_SK + KV_W]
        v_new = proj_ref[rows, _SV:_SV + KV_W]
        kt_old = ck_ref[b]
        vt_old = cv_ref[b]
        q_s = qs_ref[:, rows, :].reshape(ht, LANES).astype(BF16)
        s = jnp.concatenate([_dot(q_s, kt_old.astype(BF16)),
                             _dot_nt(q_s, k_new.astype(BF16))], axis=1) + bias
        m = jnp.maximum(jnp.max(s, axis=-1, keepdims=True), sink)
        e = jnp.exp(s - m)
        den = jnp.sum(e, axis=-1, keepdims=True) + jnp.exp(sink - m)
        eb = e.astype(BF16)
        o_s = (_dot_nt(eb[:, :WINDOW], vt_old.astype(BF16))
               + _dot(eb[:, WINDOW:], v_new.astype(BF16))) / den
        os_ref[:, rows, :] = o_s.reshape(SWA_HEADS, t, LANES)
        shift = WINDOW - t - r0
        cko_ref[b] = jnp.where(keep, pltpu.roll(kt_old, WINDOW - t, 1), pltpu.roll(knt, shift, 1))
        cvo_ref[b] = jnp.where(keep, pltpu.roll(vt_old, WINDOW - t, 1), pltpu.roll(vnt, shift, 1))
        return carry

    lax.fori_loop(0, sb, per_seq, 0, unroll=16)

    for p in range(PAIRS):
        cols = slice(p * LANES, (p + 1) * LANES)
        o2 = oh_ref[2 * p] + pltpu.roll(oh_ref[2 * p + 1], HEAD_DIM, 1)
        gn = _pair_group_norm(o2, lo, gnw_ref[:, cols])
        gate = proj_ref[:, _RG + p * LANES:_RG + (p + 1) * LANES]
        merged_ref[:, cols] = (gate * jax.nn.sigmoid(gate) * gn).astype(BF16)
        merged_ref[:, RET_W + p * LANES:RET_W + (p + 1) * LANES] = jnp.where(
            lo, os_ref[p], os_ref[SWA_GROUP + p]).astype(BF16)
    mix = _dot(merged_ref[...], wout_ref[...])
    y_ref[...] = _layer_norm(DN_ALPHA * x + mix, lng_ref[...], lnb_ref[...])


def _mixer_sample(x, state, ck, cv, w_in_p, w_out_p, lng, lnb, gnw, sink_rows, consts, *, sb, t):
    rows_total = x.shape[0]
    nseq = rows_total // t
    assert nseq % sb == 0 and sb * t == LANES and t % 8 == 0
    rows = sb * t
    weights = (w_in_p, w_out_p)
    body = functools.partial(_mixer_sample_body, sb=sb, t=t)
    return pl.pallas_call(
        body,
        grid=(nseq // sb,),
        in_specs=[
            pl.BlockSpec((rows, D_MODEL), lambda i: (i, 0)),
            pl.BlockSpec((sb, RET_W, HEAD_DIM), lambda i: (i, 0, 0)),
            pl.BlockSpec((sb, KV_W, WINDOW), lambda i: (i, 0, 0)),
            pl.BlockSpec((sb, KV_W, WINDOW), lambda i: (i, 0, 0)),
        ] + [_const_spec(w.shape) for w in weights] + [
            _const_spec(lng.shape), _const_spec(lnb.shape), _const_spec(gnw.shape),
            _const_spec(sink_rows.shape),
        ] + [_const_spec(c.shape) for c in consts],
        out_specs=[
            pl.BlockSpec((rows, D_MODEL), lambda i: (i, 0)),
            pl.BlockSpec((sb, RET_W, HEAD_DIM), lambda i: (i, 0, 0)),
            pl.BlockSpec((sb, KV_W, WINDOW), lambda i: (i, 0, 0)),
            pl.BlockSpec((sb, KV_W, WINDOW), lambda i: (i, 0, 0)),
        ],
        out_shape=[
            jax.ShapeDtypeStruct((rows_total, D_MODEL), F32),
            jax.ShapeDtypeStruct((nseq, RET_W, HEAD_DIM), F32),
            jax.ShapeDtypeStruct((nseq, KV_W, WINDOW), F32),
            jax.ShapeDtypeStruct((nseq, KV_W, WINDOW), F32),
        ],
        scratch_shapes=[
            pltpu.VMEM((rows, IN_COLS), F32),
            pltpu.VMEM((RET_HEADS, rows, HEAD_DIM), F32),
            pltpu.VMEM((SWA_HEADS, rows, LANES), F32),
            pltpu.VMEM((RET_HEADS, rows, LANES), F32),
            pltpu.VMEM((SWA_HEADS, rows, LANES), F32),
            pltpu.VMEM((rows, 2 * RET_W), BF16),
        ],
        compiler_params=pltpu.CompilerParams(
            dimension_semantics=("arbitrary",), vmem_limit_bytes=VMEM_LIMIT_BYTES),
        name="mixer_sample",
    )(x, state, ck, cv, *weights, lng, lnb, gnw, sink_rows, *consts)


def _prep_prompt_weights(w_in, w_out, sinks):
    sq = w_in[:, _SQ:_SQ + SWA_W].reshape(D_MODEL, SWA_HEADS, HEAD_DIM)
    sq = sq[:, jnp.asarray(_SWA_HEAD_ORDER), :].reshape(D_MODEL, SWA_W)
    w_in_p = jnp.concatenate([w_in[:, :_SQ], sq, w_in[:, _SK:]], axis=1).astype(BF16)
    wo_s = w_out[RET_W:].reshape(SWA_HEADS, HEAD_DIM, D_MODEL)
    wo_s = wo_s[jnp.asarray(_SWA_HEAD_ORDER)].reshape(SWA_W, D_MODEL)
    w_out_p = jnp.concatenate([w_out[:RET_W], wo_s], axis=0).astype(BF16)
    sink_st = jnp.repeat(sinks.astype(F32)[jnp.asarray(_SWA_HEAD_ORDER)], CHUNK)[None, :]
    return w_in_p, w_out_p, sink_st


def kernel(x_prompt, x_sample, state_ret, cache_swa_k, cache_swa_v, ln_gain, ln_bias, w_in,
           ret_gn_w, swa_sinks, w_out, ffn1_gate, ffn1_up, ffn1_down, ffn2_gate, ffn2_up, ffn2_down):
    assert ln_gain.shape[0] == DEPTH == 1
    bsz, seq, _ = x_prompt.shape
    nseq, t, _ = x_sample.shape
    lng = ln_gain[0].astype(F32).reshape(3, 1, D_MODEL)
    lnb = ln_bias[0].astype(F32).reshape(3, 1, D_MODEL)
    f1 = (ffn1_gate[0].astype(BF16), ffn1_up[0].astype(BF16), ffn1_down[0].astype(BF16))
    f2 = (ffn2_gate[0].astype(BF16), ffn2_up[0].astype(BF16), ffn2_down[0].astype(BF16))
    gnw = ret_gn_w[0].astype(F32)

    to_dims_major = lambda c: jnp.transpose(c, (0, 2, 3, 1)).reshape(c.shape[0], KV_W, WINDOW)
    to_pos_major = lambda c: jnp.transpose(
        c.reshape(c.shape[0], SWA_KV_HEADS, HEAD_DIM, WINDOW), (0, 3, 1, 2))[None]

    tm_p = min(1024, bsz * seq)
    tm_s = min(512, nseq * t)
    xp = x_prompt.reshape(bsz * seq, D_MODEL)
    xs = x_sample.reshape(nseq * t, D_MODEL)

    xp = _ffn_ln(xp, *f1, lng[0], lnb[0], tm=tm_p)
    w_in_p, w_out_p, sink_st = _prep_prompt_weights(w_in[0], w_out[0], swa_sinks[0])
    yp, st_p, ck_p, cv_p = _mixer_prompt(
        xp.reshape(bsz, seq, D_MODEL), w_in_p, w_out_p, lng[1], lnb[1], gnw.reshape(1, RET_W),
        sink_st, _prompt_constants(), tq=min(1024, seq))
    yp = _ffn_ln(yp.reshape(bsz * seq, D_MODEL), *f2, lng[2], lnb[2], tm=tm_p)

    xs = _ffn_ln(xs, *f1, lng[0], lnb[0], tm=tm_s)
    sink_rows = jnp.repeat(swa_sinks[0].astype(F32), t)[:, None]
    ys, st_s, ck_s, cv_s = _mixer_sample(
        xs, state_ret[0].astype(F32).reshape(nseq, RET_W, HEAD_DIM),
        to_dims_major(cache_swa_k[0]), to_dims_major(cache_swa_v[0]),
        w_in_p, w_out_p, lng[1], lnb[1], gnw.reshape(1, RET_W), sink_rows, _sample_constants(t),
        sb=LANES // t, t=t)
    ys = _ffn_ln(ys, *f2, lng[2], lnb[2], tm=tm_s)

    return (yp.reshape(bsz, seq, D_MODEL), ys.reshape(nseq, t, D_MODEL),
            st_p[None], to_pos_major(ck_p), to_pos_major(cv_p),
            st_s.reshape(1, nseq, RET_HEADS, HEAD_DIM, HEAD_DIM),
            to_pos_major(ck_s), to_pos_major(cv_s))
```

```python
import functools

import numpy as np
import jax
import jax.numpy as jnp
from jax import lax
from jax.experimental import pallas as pl
from jax.experimental.pallas import tpu as pltpu

F32 = jnp.float32
BF16 = jnp.bfloat16

D_MODEL = 1024
HEAD_DIM = 64
RET_HEADS = 8
SWA_HEADS = 8
SWA_KV_HEADS = 2
SWA_GROUP = SWA_HEADS // SWA_KV_HEADS
WINDOW = 128
CHUNK = 128
D_FF = 2816
RET_W = RET_HEADS * HEAD_DIM
SWA_W = SWA_HEADS * HEAD_DIM
KV_W = SWA_KV_HEADS * HEAD_DIM
IN_COLS = 4 * RET_W + SWA_W + 2 * KV_W
LN_EPS = 1e-5
GN_EPS = 1e-5
DEPTH = 1
DN_ALPHA = (2.0 * DEPTH) ** 0.25
QK_SCALE = HEAD_DIM ** -0.5
NEG_BIG = -1e30
LOG2E = 1.4426950408889634

LANES = 128
PAIRS = RET_HEADS // 2
GROUPS = RET_HEADS // 4
GROUP_W = 4 * HEAD_DIM
PROJ_ROWS = 256
PROJ_COLS = 512
VMEM_LIMIT_BYTES = 56 * 1024 * 1024

_RQ, _RK, _RV, _RG, _SQ = 0, RET_W, 2 * RET_W, 3 * RET_W, 4 * RET_W
_SK = 4 * RET_W + SWA_W
_SV = _SK + KV_W

_SWA_PAIR_HEADS = [(p, SWA_GROUP + p) for p in range(PAIRS)]
_SWA_HEAD_ORDER = [h for pair in _SWA_PAIR_HEADS for h in pair]

_FF_CHUNKS = ((0, 2816),)
FFN_ROW_BLOCK = 256


def _log_gamma():
    h = np.arange(RET_HEADS, dtype=np.float64)
    return np.log1p(-np.exp2(-5.0 - h))


def _alibi_slopes():
    return np.exp2(-8.0 / SWA_HEADS * np.arange(1, SWA_HEADS + 1, dtype=np.float64))


def _layer_norm(z, g, b):
    mu = jnp.mean(z, axis=-1, keepdims=True)
    d = z - mu
    var = jnp.mean(d * d, axis=-1, keepdims=True)
    return d * lax.rsqrt(var + LN_EPS) * g + b


def _pair_group_norm(o, lo, gain):
    zero = jnp.zeros_like(o)
    s_lo = jnp.sum(jnp.where(lo, o, zero), axis=-1, keepdims=True)
    s_hi = jnp.sum(jnp.where(lo, zero, o), axis=-1, keepdims=True)
    d = o - jnp.where(lo, s_lo, s_hi) * (1.0 / HEAD_DIM)
    d2 = d * d
    v_lo = jnp.sum(jnp.where(lo, d2, zero), axis=-1, keepdims=True)
    v_hi = jnp.sum(jnp.where(lo, zero, d2), axis=-1, keepdims=True)
    var = jnp.where(lo, v_lo, v_hi) * (1.0 / HEAD_DIM)
    return d * lax.rsqrt(var + GN_EPS) * gain


def _dot(a, b):
    return jnp.dot(a, b, preferred_element_type=F32)


def _dot_nt(a, b):
    return lax.dot_general(a, b, (((1,), (1,)), ((), ())), preferred_element_type=F32)


def _dot_tn(a, b):
    return lax.dot_general(a, b, (((0,), (0,)), ((), ())), preferred_element_type=F32)


def _ffn_ln_body(x_ref, wg_ref, wu_ref, wd_ref, g_ref, b_ref, o_ref, *, row_block):
    for r0 in range(0, x_ref.shape[0], row_block):
        x = x_ref[r0:r0 + row_block, :]
        xb = x.astype(BF16)
        acc = None
        for c0, cw in _FF_CHUNKS:
            gate = _dot(xb, wg_ref[:, c0:c0 + cw])
            up = _dot(xb, wu_ref[:, c0:c0 + cw])
            hid = (gate * jax.nn.sigmoid(gate) * up).astype(BF16)
            part = _dot(hid, wd_ref[c0:c0 + cw, :])
            acc = part if acc is None else acc + part
        z = DN_ALPHA * x + 0.5 * acc
        o_ref[r0:r0 + row_block, :] = _layer_norm(z, g_ref[...], b_ref[...])


def _const_spec(shape):
    nd = len(shape)
    return pl.BlockSpec(shape, lambda *_: (0,) * nd, pipeline_mode=pl.Buffered(1))


def _ffn_ln(x, wg, wu, wd, g, b, *, tm):
    rows = x.shape[0]
    assert rows % tm == 0 and tm % FFN_ROW_BLOCK == 0
    return pl.pallas_call(
        functools.partial(_ffn_ln_body, row_block=FFN_ROW_BLOCK),
        grid=(rows // tm,),
        in_specs=[
            pl.BlockSpec((tm, D_MODEL), lambda i: (i, 0)),
            _const_spec((D_MODEL, D_FF)),
            _const_spec((D_MODEL, D_FF)),
            _const_spec((D_FF, D_MODEL)),
            _const_spec((1, D_MODEL)),
            _const_spec((1, D_MODEL)),
        ],
        out_specs=pl.BlockSpec((tm, D_MODEL), lambda i: (i, 0)),
        out_shape=jax.ShapeDtypeStruct((rows, D_MODEL), F32),
        compiler_params=pltpu.CompilerParams(
            dimension_semantics=("arbitrary",), vmem_limit_bytes=VMEM_LIMIT_BYTES),
        name="ffn_ln",
    )(x, wg, wu, wd, g, b)


def _prompt_constants():
    lg = _log_gamma()
    idx = np.arange(CHUNK, dtype=np.float64)
    diff = idx[:, None] - idx[None, :]
    dm = np.where(diff >= 0, np.exp(np.where(diff >= 0, diff, 0.0)[None] * lg[:, None, None]), 0.0)
    dm_st = np.stack([np.block([[dm[4 * g], dm[4 * g + 2]], [dm[4 * g + 1], dm[4 * g + 3]]])
                      for g in range(GROUPS)])
    qw = np.repeat(np.exp((idx + 1.0)[:, None] * lg[None, :]), HEAD_DIM, axis=1)
    kw = np.repeat(np.exp((CHUNK - 1.0 - idx)[:, None] * lg[None, :]), HEAD_DIM, axis=1) * QK_SCALE
    gc = np.exp(CHUNK * lg)
    blk = np.kron(np.eye(4), np.ones((HEAD_DIM, HEAD_DIM)))
    gbd = np.stack([blk * np.repeat(gc[4 * g:4 * g + 4], HEAD_DIM)[:, None] for g in range(GROUPS)])
    slopes = _alibi_slopes()
    i = np.arange(CHUNK)
    j = np.arange(2 * CHUNK)
    dist = i[:, None] + CHUNK - j[None, :]
    within = (dist >= 0) & (dist <= WINDOW)
    valid = [within & (j >= CHUNK)[None, :], within]
    bias = np.stack([np.concatenate([np.where(valid[v], -slopes[h] * dist * LOG2E, NEG_BIG).T
                                     for h in _SWA_HEAD_ORDER], axis=1) for v in range(2)])
    f = lambda a: jnp.asarray(a, F32)
    return f(dm_st), f(qw), f(kw), f(gbd), f(blk), f(bias)


def _mixer_prompt_body(x_ref, xn_ref, win_ref, wout_ref, lng_ref, lnb_ref, gnw_ref, sink_ref,
                       dm_ref, qw_ref, kw_ref, gbd_ref, blk_ref, bias_ref,
                       y_ref, st_ref, ck_ref, cv_ref,
                       pb_ref, proj1_ref, oret_ref, yret_ref, swat_ref, s_ref, kprev_ref, vprev_ref,
                       *, tq):
    step = pl.program_id(1)
    nsteps = pl.num_programs(1)
    flat = pl.program_id(0) * nsteps + step
    slot = lax.rem(flat, 2)
    nchunks = tq // CHUNK
    nblocks = tq // PROJ_ROWS
    col_starts = range(0, IN_COLS, PROJ_COLS)

    @pl.when(step == 0)
    def _():
        s_ref[...] = jnp.zeros_like(s_ref)
        kprev_ref[...] = jnp.zeros_like(kprev_ref)
        vprev_ref[...] = jnp.zeros_like(vprev_ref)

    x = x_ref[0]

    def project_piece(dst, xrows, c0):
        c1 = min(c0 + PROJ_COLS, IN_COLS)
        dst[:, c0:c1] = _dot(xrows.astype(BF16), win_ref[:, c0:c1])

    @pl.when(flat == 0)
    def _():
        for c0 in col_starts:
            project_piece(pb_ref.at[0], x[:PROJ_ROWS], c0)

    def proj(r0, nrows, c0, c1):
        if r0 < PROJ_ROWS:
            return pb_ref[slot, r0:r0 + nrows, c0:c1]
        return proj1_ref[r0 - PROJ_ROWS:r0 - PROJ_ROWS + nrows, c0:c1]

    pending = []

    def emit_pieces(n):
        for _ in range(min(n, len(pending))):
            pending.pop(0)()

    pieces_per_chunk = pl.cdiv(len(col_starts), PROJ_ROWS // CHUNK)

    lane = lax.broadcasted_iota(jnp.int32, (CHUNK, LANES), 1)
    lo = lane < HEAD_DIM
    lane_g = lax.broadcasted_iota(jnp.int32, (CHUNK, GROUP_W), 1)
    lo_g = jnp.bitwise_and(lane_g, LANES - 1) < HEAD_DIM
    blk = blk_ref[...]
    zero_b = jnp.zeros((CHUNK, LANES), BF16)

    def pair_diag(a):
        return jnp.concatenate([jnp.concatenate([a[:, :LANES], zero_b], axis=1),
                                jnp.concatenate([zero_b, a[:, LANES:]], axis=1)], axis=0)

    def head_stack(a):
        zero = jnp.zeros_like(a)
        return jnp.concatenate([jnp.where(lo_g, a, zero), jnp.where(lo_g, zero, a)], axis=0)

    lane_t = lax.broadcasted_iota(jnp.int32, (PROJ_ROWS, LANES), 1)
    lo_t = lane_t < HEAD_DIM

    def finish_block(kb):
        rb = slice(kb * PROJ_ROWS, (kb + 1) * PROJ_ROWS)
        for p in range(PAIRS):
            cols = slice(p * LANES, (p + 1) * LANES)
            gn = _pair_group_norm(oret_ref[rb, cols], lo_t, gnw_ref[:, cols])
            gate = proj(kb * PROJ_ROWS, PROJ_ROWS, _RG + p * LANES, _RG + (p + 1) * LANES)
            yret_ref[rb, cols] = (gate * jax.nn.sigmoid(gate) * gn).astype(BF16)
        mix = (_dot(yret_ref[rb, :], wout_ref[:RET_W, :])
               + _dot_tn(swat_ref[:, rb].astype(BF16), wout_ref[RET_W:, :]))
        y_ref[0, rb, :] = _layer_norm(DN_ALPHA * x[rb] + mix, lng_ref[...], lnb_ref[...])

    k_prev = kprev_ref[...]
    vt_prev = vprev_ref[...]
    for c in range(nchunks):
        r0 = c * CHUNK
        rows = slice(r0, r0 + CHUNK)
        if r0 % PROJ_ROWS == 0:
            nb = r0 // PROJ_ROWS + 1
            if nb < nblocks:
                dst = proj1_ref.at[(nb - 1) * PROJ_ROWS:nb * PROJ_ROWS]
                xrows = x[nb * PROJ_ROWS:(nb + 1) * PROJ_ROWS]
            else:
                dst, xrows = pb_ref.at[1 - slot], xn_ref[0]
            pending.extend(functools.partial(project_piece, dst, xrows, c0) for c0 in col_starts)
        budget = pieces_per_chunk
        for g in range(GROUPS):
            if budget > 0:
                emit_pieces(1)
                budget -= 1
            cols = slice(g * GROUP_W, (g + 1) * GROUP_W)
            q4 = proj(r0, CHUNK, _RQ + g * GROUP_W, _RQ + (g + 1) * GROUP_W)
            k4 = proj(r0, CHUNK, _RK + g * GROUP_W, _RK + (g + 1) * GROUP_W)
            v4 = proj(r0, CHUNK, _RV + g * GROUP_W, _RV + (g + 1) * GROUP_W).astype(BF16)
            k_bd = pair_diag((k4 * QK_SCALE).astype(BF16))
            sc = _dot_nt(head_stack(q4).astype(BF16), k_bd) * dm_ref[g]
            qd_st = head_stack(q4 * qw_ref[:, cols])
            lhs = jnp.concatenate([sc.astype(BF16), qd_st.astype(BF16)], axis=1)
            s_old = s_ref[g]
            rhs = jnp.concatenate([pair_diag(v4), s_old.astype(BF16)], axis=0)
            r = _dot(lhs, rhs)
            oret_ref[rows, cols] = jnp.where(lo_g, r[:CHUNK], r[CHUNK:])
            kd = (k4 * kw_ref[:, cols]).astype(BF16)
            s_ref[g] = gbd_ref[g] * s_old + blk * _dot_tn(kd, v4)
        k_cur = proj(r0, CHUNK, _SK, _SK + KV_W).astype(BF16)
        vt_cur = proj(r0, CHUNK, _SV, _SV + KV_W).T.astype(BF16)
        kk = jnp.concatenate([k_prev, k_cur], axis=0)
        vvt = jnp.concatenate([vt_prev, vt_cur], axis=1)
        q_parts = []
        for p in range(PAIRS):
            q2 = proj(r0, CHUNK, _SQ + p * LANES, _SQ + (p + 1) * LANES) * (QK_SCALE * LOG2E)
            zero = jnp.zeros_like(q2)
            q_parts += [jnp.where(lo, q2, zero).astype(BF16), jnp.where(lo, zero, q2).astype(BF16)]
        st = _dot_nt(kk, jnp.concatenate(q_parts, axis=0))
        if c == 0:
            st = st + bias_ref[jnp.minimum(step, 1)]
        else:
            st = st + bias_ref[1]
        sink = sink_ref[...] * LOG2E
        e_parts, den_parts = [], []
        for p in range(PAIRS):
            if budget > 0:
                emit_pieces(1)
                budget -= 1
            pc = slice(2 * p * CHUNK, 2 * (p + 1) * CHUNK)
            st_p, sink_p = st[:, pc], sink[:, pc]
            m = jnp.maximum(jnp.max(st_p, axis=0, keepdims=True), sink_p)
            e = jnp.exp2(st_p - m)
            den_parts.append(jnp.sum(e, axis=0, keepdims=True) + jnp.exp2(sink_p - m))
            e_parts.append(e.astype(BF16))
        if (r0 + CHUNK) % PROJ_ROWS == 0:
            emit_pieces(len(pending))
        den = jnp.concatenate(den_parts, axis=1)
        ot = _dot(vvt, jnp.concatenate(e_parts, axis=1)) * (1.0 / den)
        for p in range(PAIRS):
            c0 = 2 * p * CHUNK
            swat_ref[p * LANES:p * LANES + HEAD_DIM, rows] = ot[:HEAD_DIM, c0:c0 + CHUNK]
            swat_ref[p * LANES + HEAD_DIM:(p + 1) * LANES, rows] = ot[HEAD_DIM:, c0 + CHUNK:c0 + 2 * CHUNK]
        k_prev, vt_prev = k_cur, vt_cur
        if (r0 + CHUNK) % PROJ_ROWS == 0:
            finish_block(r0 // PROJ_ROWS)
    kprev_ref[...] = k_prev
    vprev_ref[...] = vt_prev

    @pl.when(step == nsteps - 1)
    def _():
        for h in range(RET_HEADS):
            g, i = divmod(h, 4)
            blk_h = s_ref[g, i * HEAD_DIM:(i + 1) * HEAD_DIM, (i // 2) * LANES:(i // 2 + 1) * LANES]
            if i % 2:
                blk_h = pltpu.roll(blk_h, HEAD_DIM, 1)
            st_ref[0, h] = blk_h[:, :HEAD_DIM]
        ck_ref[0] = proj(tq - WINDOW, WINDOW, _SK, _SK + KV_W).T
        cv_ref[0] = proj(tq - WINDOW, WINDOW, _SV, _SV + KV_W).T


def _mixer_prompt(x, w_in_p, w_out_p, lng, lnb, gnw, sink_st, consts, *, tq):
    bsz, seq, _ = x.shape
    assert seq % tq == 0 and tq % PROJ_ROWS == 0 and PROJ_ROWS % CHUNK == 0
    dm_st, qw, kw, gbd, blk, bias = consts
    body = functools.partial(_mixer_prompt_body, tq=tq)
    nsteps = seq // tq
    blocks_per_tile = tq // PROJ_ROWS

    def next_tile_head(b, s):
        nxt = jnp.minimum(b * nsteps + s + 1, bsz * nsteps - 1)
        return nxt // nsteps, (nxt % nsteps) * blocks_per_tile, 0

    return pl.pallas_call(
        body,
        grid=(bsz, nsteps),
        in_specs=[
            pl.BlockSpec((1, tq, D_MODEL), lambda b, s: (b, s, 0)),
            pl.BlockSpec((1, PROJ_ROWS, D_MODEL), next_tile_head),
            _const_spec((D_MODEL, IN_COLS)),
            _const_spec((2 * RET_W, D_MODEL)),
            _const_spec((1, D_MODEL)),
            _const_spec((1, D_MODEL)),
            _const_spec((1, RET_W)),
            _const_spec((1, SWA_HEADS * CHUNK)),
            _const_spec(dm_st.shape),
            _const_spec(qw.shape),
            _const_spec(kw.shape),
            _const_spec(gbd.shape),
            _const_spec(blk.shape),
            _const_spec(bias.shape),
        ],
        out_specs=[
            pl.BlockSpec((1, tq, D_MODEL), lambda b, s: (b, s, 0)),
            pl.BlockSpec((1, RET_HEADS, HEAD_DIM, HEAD_DIM), lambda b, s: (b, 0, 0, 0)),
            pl.BlockSpec((1, KV_W, WINDOW), lambda b, s: (b, 0, 0)),
            pl.BlockSpec((1, KV_W, WINDOW), lambda b, s: (b, 0, 0)),
        ],
        out_shape=[
            jax.ShapeDtypeStruct((bsz, seq, D_MODEL), F32),
            jax.ShapeDtypeStruct((bsz, RET_HEADS, HEAD_DIM, HEAD_DIM), F32),
            jax.ShapeDtypeStruct((bsz, KV_W, WINDOW), F32),
            jax.ShapeDtypeStruct((bsz, KV_W, WINDOW), F32),
        ],
        scratch_shapes=[
            pltpu.VMEM((2, PROJ_ROWS, IN_COLS), F32),
            pltpu.VMEM((tq - PROJ_ROWS, IN_COLS), F32),
            pltpu.VMEM((tq, RET_W), F32),
            pltpu.VMEM((tq, RET_W), BF16),
            pltpu.VMEM((SWA_W, tq), F32),
            pltpu.VMEM((GROUPS, GROUP_W, GROUP_W), F32),
            pltpu.VMEM((CHUNK, KV_W), BF16),
            pltpu.VMEM((KV_W, CHUNK), BF16),
        ],
        compiler_params=pltpu.CompilerParams(
            dimension_semantics=("arbitrary", "arbitrary"), vmem_limit_bytes=VMEM_LIMIT_BYTES),
        name="mixer_prompt",
    )(x, x, w_in_p, w_out_p, lng, lnb, gnw, sink_st, dm_st, qw, kw, gbd, blk, bias)


def _sample_constants(t):
    lg = _log_gamma()
    hh = np.arange(RET_HEADS)
    tt = np.arange(t, dtype=np.float64)
    row_h = np.repeat(hh, t)
    row_t = np.tile(tt, RET_HEADS)
    col_h = np.repeat(hh, HEAD_DIM)
    same = (row_h[:, None] == col_h[None, :]).astype(np.float64)
    mask_q1 = same * QK_SCALE
    mask_qw = same * np.exp((row_t + 1.0) * lg[row_h])[:, None]
    mask_kw = same * (np.exp((t - 1.0 - row_t) * lg[row_h]) * QK_SCALE)[:, None]
    diff = row_t[:, None] - row_t[None, :]
    same_h = row_h[:, None] == row_h[None, :]
    dm = np.where(same_h & (diff >= 0), np.exp(np.where(diff >= 0, diff, 0.0) * lg[row_h][:, None]), 0.0)
    g_t = np.repeat(np.exp(t * lg), HEAD_DIM)[:, None] * np.ones((1, HEAD_DIM))
    slopes = _alibi_slopes()
    j = np.arange(WINDOW + t)
    dist = row_t[:, None] + WINDOW - j[None, :]
    valid = (dist >= 0) & (dist <= WINDOW)
    bias = np.where(valid, -slopes[row_h][:, None] * dist, NEG_BIG)
    f = lambda a: jnp.asarray(a, F32)
    return f(mask_q1), f(mask_qw), f(mask_kw), f(dm), f(g_t), f(bias)


def _mixer_sample_body(x_ref, st_ref, ck_ref, cv_ref, win_ref, wout_ref,
                       lng_ref, lnb_ref, gnw_ref, sink_ref,
                       mq1_ref, mqw_ref, mkw_ref, dm_ref, gt_ref, bias_ref,
                       y_ref, sto_ref, cko_ref, cvo_ref,
                       proj_ref, vh_ref, qs_ref, oh_ref, os_ref, merged_ref, *, sb, t):
    ht = RET_HEADS * t
    nrows = sb * t
    x = x_ref[...]
    proj_ref[...] = _dot(x.astype(BF16), win_ref[...])
    lane = lax.broadcasted_iota(jnp.int32, (nrows, LANES), 1)
    lo = lane < HEAD_DIM
    for p in range(PAIRS):
        v2 = proj_ref[:, _RV + p * LANES:_RV + (p + 1) * LANES]
        vh_ref[2 * p] = v2[:, :HEAD_DIM]
        vh_ref[2 * p + 1] = pltpu.roll(v2, HEAD_DIM, 1)[:, :HEAD_DIM]
        q2 = proj_ref[:, _SQ + p * LANES:_SQ + (p + 1) * LANES] * QK_SCALE
        zero = jnp.zeros_like(q2)
        qs_ref[p] = jnp.where(lo, q2, zero)
        qs_ref[SWA_GROUP + p] = jnp.where(lo, zero, q2)
    oh_ref[...] = jnp.zeros_like(oh_ref)
    knt = proj_ref[:, _SK:_SK + KV_W].T
    vnt = proj_ref[:, _SV:_SV + KV_W].T
    keep = lax.broadcasted_iota(jnp.int32, (KV_W, WINDOW), 1) < WINDOW - t

    mq1 = mq1_ref[...]
    mqw = mqw_ref[...]
    mkw = mkw_ref[...]
    dm = dm_ref[...]
    g_t = gt_ref[...]
    bias = bias_ref[...]
    sink = sink_ref[...]

    def per_seq(b, carry):
        r0 = pl.multiple_of(b * t, t)
        rows = pl.ds(r0, t)
        q_b = proj_ref[rows, _RQ:_RQ + RET_W]
        k_b = proj_ref[rows, _RK:_RK + RET_W]
        q_rep = jnp.concatenate([q_b] * RET_HEADS, axis=0)
        k_rep = jnp.concatenate([k_b] * RET_HEADS, axis=0)
        v_st = vh_ref[:, rows, :].reshape(ht, HEAD_DIM).astype(BF16)
        s0 = st_ref[b]
        sc = _dot_nt((q_rep * mq1).astype(BF16), k_rep.astype(BF16)) * dm
        o = _dot((q_rep * mqw).astype(BF16), s0.astype(BF16)) + _dot(sc.astype(BF16), v_st)
        oh_ref[:, rows, 0:HEAD_DIM] = o.reshape(RET_HEADS, t, HEAD_DIM)
        sto_ref[b] = g_t * s0 + _dot_tn((k_rep * mkw).astype(BF16), v_st)
        k_new = proj_ref[rows, _SK:_SK + KV_W]
        v_new = proj_ref[rows, _SV:_SV + KV_W]
        kt_old = ck_ref[b]
        vt_old = cv_ref[b]
        q_s = qs_ref[:, rows, :].reshape(ht, LANES).astype(BF16)
        s = jnp.concatenate([_dot(q_s, kt_old.astype(BF16)),
                             _dot_nt(q_s, k_new.astype(BF16))], axis=1) + bias
        m = jnp.maximum(jnp.max(s, axis=-1, keepdims=True), sink)
        e = jnp.exp(s - m)
        den = jnp.sum(e, axis=-1, keepdims=True) + jnp.exp(sink - m)
        eb = e.astype(BF16)
        o_s = (_dot_nt(eb[:, :WINDOW], vt_old.astype(BF16))
               + _dot(eb[:, WINDOW:], v_new.astype(BF16))) / den
        os_ref[:, rows, :] = o_s.reshape(SWA_HEADS, t, LANES)
        shift = WINDOW - t - r0
        cko_ref[b] = jnp.where(keep, pltpu.roll(kt_old, WINDOW - t, 1), pltpu.roll(knt, shift, 1))
        cvo_ref[b] = jnp.where(keep, pltpu.roll(vt_old, WINDOW - t, 1), pltpu.roll(vnt, shift, 1))
        return carry

    lax.fori_loop(0, sb, per_seq, 0, unroll=16)

    for p in range(PAIRS):
        cols = slice(p * LANES, (p + 1) * LANES)
        o2 = oh_ref[2 * p] + pltpu.roll(oh_ref[2 * p + 1], HEAD_DIM, 1)
        gn = _pair_group_norm(o2, lo, gnw_ref[:, cols])
        gate = proj_ref[:, _RG + p * LANES:_RG + (p + 1) * LANES]
        merged_ref[:, cols] = (gate * jax.nn.sigmoid(gate) * gn).astype(BF16)
        merged_ref[:, RET_W + p * LANES:RET_W + (p + 1) * LANES] = jnp.where(
            lo, os_ref[p], os_ref[SWA_GROUP + p]).astype(BF16)
    mix = _dot(merged_ref[...], wout_ref[...])
    y_ref[...] = _layer_norm(DN_ALPHA * x + mix, lng_ref[...], lnb_ref[...])


def _mixer_sample(x, state, ck, cv, w_in_p, w_out_p, lng, lnb, gnw, sink_rows, consts, *, sb, t):
    rows_total = x.shape[0]
    nseq = rows_total // t
    assert nseq % sb == 0 and sb * t == LANES and t % 8 == 0
    rows = sb * t
    weights = (w_in_p, w_out_p)
    body = functools.partial(_mixer_sample_body, sb=sb, t=t)
    return pl.pallas_call(
        body,
        grid=(nseq // sb,),
        in_specs=[
            pl.BlockSpec((rows, D_MODEL), lambda i: (i, 0)),
            pl.BlockSpec((sb, RET_W, HEAD_DIM), lambda i: (i, 0, 0)),
            pl.BlockSpec((sb, KV_W, WINDOW), lambda i: (i, 0, 0)),
            pl.BlockSpec((sb, KV_W, WINDOW), lambda i: (i, 0, 0)),
        ] + [_const_spec(w.shape) for w in weights] + [
            _const_spec(lng.shape), _const_spec(lnb.shape), _const_spec(gnw.shape),
            _const_spec(sink_rows.shape),
        ] + [_const_spec(c.shape) for c in consts],
        out_specs=[
            pl.BlockSpec((rows, D_MODEL), lambda i: (i, 0)),
            pl.BlockSpec((sb, RET_W, HEAD_DIM), lambda i: (i, 0, 0)),
            pl.BlockSpec((sb, KV_W, WINDOW), lambda i: (i, 0, 0)),
            pl.BlockSpec((sb, KV_W, WINDOW), lambda i: (i, 0, 0)),
        ],
        out_shape=[
            jax.ShapeDtypeStruct((rows_total, D_MODEL), F32),
            jax.ShapeDtypeStruct((nseq, RET_W, HEAD_DIM), F32),
            jax.ShapeDtypeStruct((nseq, KV_W, WINDOW), F32),
            jax.ShapeDtypeStruct((nseq, KV_W, WINDOW), F32),
        ],
        scratch_shapes=[
            pltpu.VMEM((rows, IN_COLS), F32),
            pltpu.VMEM((RET_HEADS, rows, HEAD_DIM), F32),
            pltpu.VMEM((SWA_HEADS, rows, LANES), F32),
            pltpu.VMEM((RET_HEADS, rows, LANES), F32),
            pltpu.VMEM((SWA_HEADS, rows, LANES), F32),
            pltpu.VMEM((rows, 2 * RET_W), BF16),
        ],
        compiler_params=pltpu.CompilerParams(
            dimension_semantics=("arbitrary",), vmem_limit_bytes=VMEM_LIMIT_BYTES),
        name="mixer_sample",
    )(x, state, ck, cv, *weights, lng, lnb, gnw, sink_rows, *consts)


def _prep_prompt_weights(w_in, w_out, sinks):
    sq = w_in[:, _SQ:_SQ + SWA_W].reshape(D_MODEL, SWA_HEADS, HEAD_DIM)
    sq = sq[:, jnp.asarray(_SWA_HEAD_ORDER), :].reshape(D_MODEL, SWA_W)
    w_in_p = jnp.concatenate([w_in[:, :_SQ], sq, w_in[:, _SK:]], axis=1).astype(BF16)
    wo_s = w_out[RET_W:].reshape(SWA_HEADS, HEAD_DIM, D_MODEL)
    wo_s = wo_s[jnp.asarray(_SWA_HEAD_ORDER)].reshape(SWA_W, D_MODEL)
    w_out_p = jnp.concatenate([w_out[:RET_W], wo_s], axis=0).astype(BF16)
    sink_st = jnp.repeat(sinks.astype(F32)[jnp.asarray(_SWA_HEAD_ORDER)], CHUNK)[None, :]
    return w_in_p, w_out_p, sink_st


def kernel(x_prompt, x_sample, state_ret, cache_swa_k, cache_swa_v, ln_gain, ln_bias, w_in,
           ret_gn_w, swa_sinks, w_out, ffn1_gate, ffn1_up, ffn1_down, ffn2_gate, ffn2_up, ffn2_down):
    assert ln_gain.shape[0] == DEPTH == 1
    bsz, seq, _ = x_prompt.shape
    nseq, t, _ = x_sample.shape
    lng = ln_gain[0].astype(F32).reshape(3, 1, D_MODEL)
    lnb = ln_bias[0].astype(F32).reshape(3, 1, D_MODEL)
    f1 = (ffn1_gate[0].astype(BF16), ffn1_up[0].astype(BF16), ffn1_down[0].astype(BF16))
    f2 = (ffn2_gate[0].astype(BF16), ffn2_up[0].astype(BF16), ffn2_down[0].astype(BF16))
    gnw = ret_gn_w[0].astype(F32)

    to_dims_major = lambda c: jnp.transpose(c, (0, 2, 3, 1)).reshape(c.shape[0], KV_W, WINDOW)
    to_pos_major = lambda c: jnp.transpose(
        c.reshape(c.shape[0], SWA_KV_HEADS, HEAD_DIM, WINDOW), (0, 3, 1, 2))[None]

    tm_p = min(512, bsz * seq)
    tm_s = min(512, nseq * t)
    xp = x_prompt.reshape(bsz * seq, D_MODEL)
    xs = x_sample.reshape(nseq * t, D_MODEL)

    xp = _ffn_ln(xp, *f1, lng[0], lnb[0], tm=tm_p)
    w_in_p, w_out_p, sink_st = _prep_prompt_weights(w_in[0], w_out[0], swa_sinks[0])
    yp, st_p, ck_p, cv_p = _mixer_prompt(
        xp.reshape(bsz, seq, D_MODEL), w_in_p, w_out_p, lng[1], lnb[1], gnw.reshape(1, RET_W),
        sink_st, _prompt_constants(), tq=min(1024, seq))
    yp = _ffn_ln(yp.reshape(bsz * seq, D_MODEL), *f2, lng[2], lnb[2], tm=tm_p)

    xs = _ffn_ln(xs, *f1, lng[0], lnb[0], tm=tm_s)
    sink_rows = jnp.repeat(swa_sinks[0].astype(F32), t)[:, None]
    ys, st_s, ck_s, cv_s = _mixer_sample(
        xs, state_ret[0].astype(F32).reshape(nseq, RET_W, HEAD_DIM),
        to_dims_major(cache_swa_k[0]), to_dims_major(cache_swa_v[0]),
        w_in_p, w_out_p, lng[1], lnb[1], gnw.reshape(1, RET_W), sink_rows, _sample_constants(t),
        sb=LANES // t, t=t)
    ys = _ffn_ln(ys, *f2, lng[2], lnb[2], tm=tm_s)

    return (yp.reshape(bsz, seq, D_MODEL), ys.reshape(nseq, t, D_MODEL),
            st_p[None], to_pos_major(ck_p), to_pos_major(cv_p),
            st_s.reshape(1, nseq, RET_HEADS, HEAD_DIM, HEAD_DIM),
            to_pos_major(ck_s), to_pos_major(cv_s))
```

```python
import functools

import numpy as np
import jax
import jax.numpy as jnp
from jax import lax
from jax.experimental import pallas as pl
from jax.experimental.pallas import tpu as pltpu

F32 = jnp.float32
BF16 = jnp.bfloat16

D_MODEL = 1024
HEAD_DIM = 64
RET_HEADS = 8
SWA_HEADS = 8
SWA_KV_HEADS = 2
SWA_GROUP = SWA_HEADS // SWA_KV_HEADS
WINDOW = 128
CHUNK = 128
D_FF = 2816
RET_W = RET_HEADS * HEAD_DIM
SWA_W = SWA_HEADS * HEAD_DIM
KV_W = SWA_KV_HEADS * HEAD_DIM
IN_COLS = 4 * RET_W + SWA_W + 2 * KV_W
LN_EPS = 1e-5
GN_EPS = 1e-5
DEPTH = 1
DN_ALPHA = (2.0 * DEPTH) ** 0.25
QK_SCALE = HEAD_DIM ** -0.5
NEG_BIG = -1e30
LOG2E = 1.4426950408889634

LANES = 128
PAIRS = RET_HEADS // 2
GROUPS = RET_HEADS // 4
GROUP_W = 4 * HEAD_DIM
PROJ_ROWS = 256
PROJ_COLS = 512
VMEM_LIMIT_BYTES = 56 * 1024 * 1024

_RQ, _RK, _RV, _RG, _SQ = 0, RET_W, 2 * RET_W, 3 * RET_W, 4 * RET_W
_SK = 4 * RET_W + SWA_W
_SV = _SK + KV_W

_SWA_PAIR_HEADS = [(p, SWA_GROUP + p) for p in range(PAIRS)]
_SWA_HEAD_ORDER = [h for pair in _SWA_PAIR_HEADS for h in pair]

_FF_CHUNKS = ((0, 2816),)
FFN_ROW_BLOCK = 256


def _log_gamma():
    h = np.arange(RET_HEADS, dtype=np.float64)
    return np.log1p(-np.exp2(-5.0 - h))


def _alibi_slopes():
    return np.exp2(-8.0 / SWA_HEADS * np.arange(1, SWA_HEADS + 1, dtype=np.float64))


def _layer_norm(z, g, b):
    mu = jnp.mean(z, axis=-1, keepdims=True)
    d = z - mu
    var = jnp.mean(d * d, axis=-1, keepdims=True)
    return d * lax.rsqrt(var + LN_EPS) * g + b


def _pair_group_norm(o, lo, gain):
    zero = jnp.zeros_like(o)
    s_lo = jnp.sum(jnp.where(lo, o, zero), axis=-1, keepdims=True)
    s_hi = jnp.sum(jnp.where(lo, zero, o), axis=-1, keepdims=True)
    d = o - jnp.where(lo, s_lo, s_hi) * (1.0 / HEAD_DIM)
    d2 = d * d
    v_lo = jnp.sum(jnp.where(lo, d2, zero), axis=-1, keepdims=True)
    v_hi = jnp.sum(jnp.where(lo, zero, d2), axis=-1, keepdims=True)
    var = jnp.where(lo, v_lo, v_hi) * (1.0 / HEAD_DIM)
    return d * lax.rsqrt(var + GN_EPS) * gain


def _dot(a, b):
    return jnp.dot(a, b, preferred_element_type=F32)


def _dot_nt(a, b):
    return lax.dot_general(a, b, (((1,), (1,)), ((), ())), preferred_element_type=F32)


def _dot_tn(a, b):
    return lax.dot_general(a, b, (((0,), (0,)), ((), ())), preferred_element_type=F32)


def _ffn_ln_body(*refs, row_block, ncast):
    x_ref, wg_ref, wu_ref, wd_ref, g_ref, b_ref = refs[:6]
    o_ref = refs[6 + ncast]
    for src, dst in zip(refs[6:6 + ncast], refs[7 + ncast:]):
        dst[...] = src[...].astype(BF16)
    for r0 in range(0, x_ref.shape[0], row_block):
        x = x_ref[r0:r0 + row_block, :]
        xb = x.astype(BF16)
        acc = None
        for c0, cw in _FF_CHUNKS:
            gate = _dot(xb, wg_ref[:, c0:c0 + cw])
            up = _dot(xb, wu_ref[:, c0:c0 + cw])
            hid = (gate * jax.nn.sigmoid(gate) * up).astype(BF16)
            part = _dot(hid, wd_ref[c0:c0 + cw, :])
            acc = part if acc is None else acc + part
        z = DN_ALPHA * x + 0.5 * acc
        o_ref[r0:r0 + row_block, :] = _layer_norm(z, g_ref[...], b_ref[...])


def _const_spec(shape):
    nd = len(shape)
    return pl.BlockSpec(shape, lambda *_: (0,) * nd, pipeline_mode=pl.Buffered(1))


def _ffn_ln(x, wg, wu, wd, g, b, *, tm, cast_next=()):
    rows = x.shape[0]
    assert rows % tm == 0 and tm % FFN_ROW_BLOCK == 0
    steps = rows // tm
    slab_specs = []
    for w in cast_next:
        nslabs = max(n for n in range(1, steps + 1)
                     if steps % n == 0 and w.shape[0] % n == 0 and (w.shape[0] // n) % 16 == 0)
        every = steps // nslabs
        slab_specs.append(pl.BlockSpec((w.shape[0] // nslabs, w.shape[1]),
                                       functools.partial(lambda i, e: (i // e, 0), e=every)))
    flat = list(cast_next)
    outs = pl.pallas_call(
        functools.partial(_ffn_ln_body, row_block=FFN_ROW_BLOCK, ncast=len(flat)),
        grid=(steps,),
        in_specs=[
            pl.BlockSpec((tm, D_MODEL), lambda i: (i, 0)),
            _const_spec((D_MODEL, D_FF)),
            _const_spec((D_MODEL, D_FF)),
            _const_spec((D_FF, D_MODEL)),
            _const_spec((1, D_MODEL)),
            _const_spec((1, D_MODEL)),
        ] + slab_specs,
        out_specs=[pl.BlockSpec((tm, D_MODEL), lambda i: (i, 0))] + slab_specs,
        out_shape=[jax.ShapeDtypeStruct((rows, D_MODEL), F32)]
        + [jax.ShapeDtypeStruct(w.shape, BF16) for w in flat],
        compiler_params=pltpu.CompilerParams(
            dimension_semantics=("arbitrary",), vmem_limit_bytes=VMEM_LIMIT_BYTES),
        name="ffn_ln",
    )(x, wg, wu, wd, g, b, *flat)
    if not cast_next:
        return outs[0]
    return outs[0], tuple(outs[1:])


def _prompt_constants():
    lg = _log_gamma()
    idx = np.arange(CHUNK, dtype=np.float64)
    diff = idx[:, None] - idx[None, :]
    dm = np.where(diff >= 0, np.exp(np.where(diff >= 0, diff, 0.0)[None] * lg[:, None, None]), 0.0)
    dm_st = np.stack([np.block([[dm[4 * g], dm[4 * g + 2]], [dm[4 * g + 1], dm[4 * g + 3]]])
                      for g in range(GROUPS)])
    qw = np.repeat(np.exp((idx + 1.0)[:, None] * lg[None, :]), HEAD_DIM, axis=1)
    kw = np.repeat(np.exp((CHUNK - 1.0 - idx)[:, None] * lg[None, :]), HEAD_DIM, axis=1) * QK_SCALE
    gc = np.exp(CHUNK * lg)
    blk = np.kron(np.eye(4), np.ones((HEAD_DIM, HEAD_DIM)))
    gbd = np.stack([blk * np.repeat(gc[4 * g:4 * g + 4], HEAD_DIM)[:, None] for g in range(GROUPS)])
    slopes = _alibi_slopes()
    i = np.arange(CHUNK)
    j = np.arange(2 * CHUNK)
    dist = i[:, None] + CHUNK - j[None, :]
    within = (dist >= 0) & (dist <= WINDOW)
    valid = [within & (j >= CHUNK)[None, :], within]
    bias = np.stack([np.concatenate([np.where(valid[v], -slopes[h] * dist * LOG2E, NEG_BIG).T
                                     for h in _SWA_HEAD_ORDER], axis=1) for v in range(2)])
    f = lambda a: jnp.asarray(a, F32)
    return f(dm_st), f(qw), f(kw), f(gbd), f(blk), f(bias)


def _mixer_prompt_body(x_ref, xn_ref, win_ref, wout_ref, lng_ref, lnb_ref, gnw_ref, sink_ref,
                       dm_ref, qw_ref, kw_ref, gbd_ref, blk_ref, bias_ref,
                       y_ref, st_ref, ck_ref, cv_ref,
                       pb_ref, proj1_ref, oret_ref, yret_ref, swat_ref, s_ref, kprev_ref, vprev_ref,
                       *, tq):
    step = pl.program_id(1)
    nsteps = pl.num_programs(1)
    flat = pl.program_id(0) * nsteps + step
    slot = lax.rem(flat, 2)
    nchunks = tq // CHUNK
    nblocks = tq // PROJ_ROWS
    col_starts = range(0, IN_COLS, PROJ_COLS)

    @pl.when(step == 0)
    def _():
        s_ref[...] = jnp.zeros_like(s_ref)
        kprev_ref[...] = jnp.zeros_like(kprev_ref)
        vprev_ref[...] = jnp.zeros_like(vprev_ref)

    x = x_ref[0]

    def project_piece(dst, xrows, c0):
        c1 = min(c0 + PROJ_COLS, IN_COLS)
        dst[:, c0:c1] = _dot(xrows.astype(BF16), win_ref[:, c0:c1])

    @pl.when(flat == 0)
    def _():
        for c0 in col_starts:
            project_piece(pb_ref.at[0], x[:PROJ_ROWS], c0)

    def proj(r0, nrows, c0, c1):
        if r0 < PROJ_ROWS:
            return pb_ref[slot, r0:r0 + nrows, c0:c1]
        return proj1_ref[r0 - PROJ_ROWS:r0 - PROJ_ROWS + nrows, c0:c1]

    pending = []

    def emit_pieces(n):
        for _ in range(min(n, len(pending))):
            pending.pop(0)()

    pieces_per_chunk = pl.cdiv(len(col_starts), PROJ_ROWS // CHUNK)

    lane = lax.broadcasted_iota(jnp.int32, (CHUNK, LANES), 1)
    lo = lane < HEAD_DIM
    lane_g = lax.broadcasted_iota(jnp.int32, (CHUNK, GROUP_W), 1)
    lo_g = jnp.bitwise_and(lane_g, LANES - 1) < HEAD_DIM
    blk = blk_ref[...]
    zero_b = jnp.zeros((CHUNK, LANES), BF16)

    def pair_diag(a):
        return jnp.concatenate([jnp.concatenate([a[:, :LANES], zero_b], axis=1),
                                jnp.concatenate([zero_b, a[:, LANES:]], axis=1)], axis=0)

    def head_stack(a):
        zero = jnp.zeros_like(a)
        return jnp.concatenate([jnp.where(lo_g, a, zero), jnp.where(lo_g, zero, a)], axis=0)

    lane_t = lax.broadcasted_iota(jnp.int32, (PROJ_ROWS, LANES), 1)
    lo_t = lane_t < HEAD_DIM

    def finish_block(kb):
        rb = slice(kb * PROJ_ROWS, (kb + 1) * PROJ_ROWS)
        for p in range(PAIRS):
            cols = slice(p * LANES, (p + 1) * LANES)
            gn = _pair_group_norm(oret_ref[rb, cols], lo_t, gnw_ref[:, cols])
            gate = proj(kb * PROJ_ROWS, PROJ_ROWS, _RG + p * LANES, _RG + (p + 1) * LANES)
            yret_ref[rb, cols] = (gate * jax.nn.sigmoid(gate) * gn).astype(BF16)
        mix = (_dot(yret_ref[rb, :], wout_ref[:RET_W, :])
               + _dot_tn(swat_ref[:, rb].astype(BF16), wout_ref[RET_W:, :]))
        y_ref[0, rb, :] = _layer_norm(DN_ALPHA * x[rb] + mix, lng_ref[...], lnb_ref[...])

    k_prev = kprev_ref[...]
    vt_prev = vprev_ref[...]
    for c in range(nchunks):
        r0 = c * CHUNK
        rows = slice(r0, r0 + CHUNK)
        if r0 % PROJ_ROWS == 0:
            nb = r0 // PROJ_ROWS + 1
            if nb < nblocks:
                dst = proj1_ref.at[(nb - 1) * PROJ_ROWS:nb * PROJ_ROWS]
                xrows = x[nb * PROJ_ROWS:(nb + 1) * PROJ_ROWS]
            else:
                dst, xrows = pb_ref.at[1 - slot], xn_ref[0]
            pending.extend(functools.partial(project_piece, dst, xrows, c0) for c0 in col_starts)
        budget = pieces_per_chunk
        for g in range(GROUPS):
            if budget > 0:
                emit_pieces(1)
                budget -= 1
            cols = slice(g * GROUP_W, (g + 1) * GROUP_W)
            q4 = proj(r0, CHUNK, _RQ + g * GROUP_W, _RQ + (g + 1) * GROUP_W)
            k4 = proj(r0, CHUNK, _RK + g * GROUP_W, _RK + (g + 1) * GROUP_W)
            v4 = proj(r0, CHUNK, _RV + g * GROUP_W, _RV + (g + 1) * GROUP_W).astype(BF16)
            k_bd = pair_diag((k4 * QK_SCALE).astype(BF16))
            sc = _dot_nt(head_stack(q4).astype(BF16), k_bd) * dm_ref[g]
            qd_st = head_stack(q4 * qw_ref[:, cols])
            lhs = jnp.concatenate([sc.astype(BF16), qd_st.astype(BF16)], axis=1)
            s_old = s_ref[g]
            rhs = jnp.concatenate([pair_diag(v4), s_old.astype(BF16)], axis=0)
            r = _dot(lhs, rhs)
            oret_ref[rows, cols] = jnp.where(lo_g, r[:CHUNK], r[CHUNK:])
            kd = (k4 * kw_ref[:, cols]).astype(BF16)
            s_ref[g] = gbd_ref[g] * s_old + blk * _dot_tn(kd, v4)
        k_cur = proj(r0, CHUNK, _SK, _SK + KV_W).astype(BF16)
        vt_cur = proj(r0, CHUNK, _SV, _SV + KV_W).T.astype(BF16)
        kk = jnp.concatenate([k_prev, k_cur], axis=0)
        vvt = jnp.concatenate([vt_prev, vt_cur], axis=1)
        q_parts = []
        for p in range(PAIRS):
            q2 = proj(r0, CHUNK, _SQ + p * LANES, _SQ + (p + 1) * LANES) * (QK_SCALE * LOG2E)
            zero = jnp.zeros_like(q2)
            q_parts += [jnp.where(lo, q2, zero).astype(BF16), jnp.where(lo, zero, q2).astype(BF16)]
        st = _dot_nt(kk, jnp.concatenate(q_parts, axis=0))
        if c == 0:
            st = st + bias_ref[jnp.minimum(step, 1)]
        else:
            st = st + bias_ref[1]
        sink = sink_ref[...] * LOG2E
        e_parts, den_parts = [], []
        for p in range(PAIRS):
            if budget > 0:
                emit_pieces(1)
                budget -= 1
            pc = slice(2 * p * CHUNK, 2 * (p + 1) * CHUNK)
            st_p, sink_p = st[:, pc], sink[:, pc]
            m = jnp.maximum(jnp.max(st_p, axis=0, keepdims=True), sink_p)
            e = jnp.exp2(st_p - m)
            den_parts.append(jnp.sum(e, axis=0, keepdims=True) + jnp.exp2(sink_p - m))
            e_parts.append(e.astype(BF16))
        if (r0 + CHUNK) % PROJ_ROWS == 0:
            emit_pieces(len(pending))
        den = jnp.concatenate(den_parts, axis=1)
        ot = _dot(vvt, jnp.concatenate(e_parts, axis=1)) * (1.0 / den)
        for p in range(PAIRS):
            c0 = 2 * p * CHUNK
            swat_ref[p * LANES:p * LANES + HEAD_DIM, rows] = ot[:HEAD_DIM, c0:c0 + CHUNK]
            swat_ref[p * LANES + HEAD_DIM:(p + 1) * LANES, rows] = ot[HEAD_DIM:, c0 + CHUNK:c0 + 2 * CHUNK]
        k_prev, vt_prev = k_cur, vt_cur
        if (r0 + CHUNK) % PROJ_ROWS == 0:
            finish_block(r0 // PROJ_ROWS)
    kprev_ref[...] = k_prev
    vprev_ref[...] = vt_prev

    @pl.when(step == nsteps - 1)
    def _():
        for h in range(RET_HEADS):
            g, i = divmod(h, 4)
            blk_h = s_ref[g, i * HEAD_DIM:(i + 1) * HEAD_DIM, (i // 2) * LANES:(i // 2 + 1) * LANES]
            if i % 2:
                blk_h = pltpu.roll(blk_h, HEAD_DIM, 1)
            st_ref[0, h] = blk_h[:, :HEAD_DIM]
        ck_ref[0] = proj(tq - WINDOW, WINDOW, _SK, _SK + KV_W).T
        cv_ref[0] = proj(tq - WINDOW, WINDOW, _SV, _SV + KV_W).T


def _mixer_prompt(x, w_in_p, w_out_p, lng, lnb, gnw, sink_st, consts, *, tq):
    bsz, seq, _ = x.shape
    assert seq % tq == 0 and tq % PROJ_ROWS == 0 and PROJ_ROWS % CHUNK == 0
    dm_st, qw, kw, gbd, blk, bias = consts
    body = functools.partial(_mixer_prompt_body, tq=tq)
    nsteps = seq // tq
    blocks_per_tile = tq // PROJ_ROWS

    def next_tile_head(b, s):
        nxt = jnp.minimum(b * nsteps + s + 1, bsz * nsteps - 1)
        return nxt // nsteps, (nxt % nsteps) * blocks_per_tile, 0

    return pl.pallas_call(
        body,
        grid=(bsz, nsteps),
        in_specs=[
            pl.BlockSpec((1, tq, D_MODEL), lambda b, s: (b, s, 0)),
            pl.BlockSpec((1, PROJ_ROWS, D_MODEL), next_tile_head),
            _const_spec((D_MODEL, IN_COLS)),
            _const_spec((2 * RET_W, D_MODEL)),
            _const_spec((1, D_MODEL)),
            _const_spec((1, D_MODEL)),
            _const_spec((1, RET_W)),
            _const_spec((1, SWA_HEADS * CHUNK)),
            _const_spec(dm_st.shape),
            _const_spec(qw.shape),
            _const_spec(kw.shape),
            _const_spec(gbd.shape),
            _const_spec(blk.shape),
            _const_spec(bias.shape),
        ],
        out_specs=[
            pl.BlockSpec((1, tq, D_MODEL), lambda b, s: (b, s, 0)),
            pl.BlockSpec((1, RET_HEADS, HEAD_DIM, HEAD_DIM), lambda b, s: (b, 0, 0, 0)),
            pl.BlockSpec((1, KV_W, WINDOW), lambda b, s: (b, 0, 0)),
            pl.BlockSpec((1, KV_W, WINDOW), lambda b, s: (b, 0, 0)),
        ],
        out_shape=[
            jax.ShapeDtypeStruct((bsz, seq, D_MODEL), F32),
            jax.ShapeDtypeStruct((bsz, RET_HEADS, HEAD_DIM, HEAD_DIM), F32),
            jax.ShapeDtypeStruct((bsz, KV_W, WINDOW), F32),
            jax.ShapeDtypeStruct((bsz, KV_W, WINDOW), F32),
        ],
        scratch_shapes=[
            pltpu.VMEM((2, PROJ_ROWS, IN_COLS), F32),
            pltpu.VMEM((tq - PROJ_ROWS, IN_COLS), F32),
            pltpu.VMEM((tq, RET_W), F32),
            pltpu.VMEM((tq, RET_W), BF16),
            pltpu.VMEM((SWA_W, tq), F32),
            pltpu.VMEM((GROUPS, GROUP_W, GROUP_W), F32),
            pltpu.VMEM((CHUNK, KV_W), BF16),
            pltpu.VMEM((KV_W, CHUNK), BF16),
        ],
        compiler_params=pltpu.CompilerParams(
            dimension_semantics=("arbitrary", "arbitrary"), vmem_limit_bytes=VMEM_LIMIT_BYTES),
        name="mixer_prompt",
    )(x, x, w_in_p, w_out_p, lng, lnb, gnw, sink_st, dm_st, qw, kw, gbd, blk, bias)


def _sample_constants(t):
    lg = _log_gamma()
    hh = np.arange(RET_HEADS)
    tt = np.arange(t, dtype=np.float64)
    row_h = np.repeat(hh, t)
    row_t = np.tile(tt, RET_HEADS)
    col_h = np.repeat(hh, HEAD_DIM)
    same = (row_h[:, None] == col_h[None, :]).astype(np.float64)
    mask_q1 = same * QK_SCALE
    mask_qw = same * np.exp((row_t + 1.0) * lg[row_h])[:, None]
    mask_kw = same * (np.exp((t - 1.0 - row_t) * lg[row_h]) * QK_SCALE)[:, None]
    diff = row_t[:, None] - row_t[None, :]
    same_h = row_h[:, None] == row_h[None, :]
    dm = np.where(same_h & (diff >= 0), np.exp(np.where(diff >= 0, diff, 0.0) * lg[row_h][:, None]), 0.0)
    g_t = np.repeat(np.exp(t * lg), HEAD_DIM)[:, None] * np.ones((1, HEAD_DIM))
    slopes = _alibi_slopes()
    j = np.arange(WINDOW + t)
    dist = row_t[:, None] + WINDOW - j[None, :]
    valid = (dist >= 0) & (dist <= WINDOW)
    bias = np.where(valid, -slopes[row_h][:, None] * dist, NEG_BIG)
    f = lambda a: jnp.asarray(a, F32)
    return f(mask_q1), f(mask_qw), f(mask_kw), f(dm), f(g_t), f(bias)


def _mixer_sample_body(x_ref, st_ref, ck_ref, cv_ref, win_ref, wout_ref,
                       lng_ref, lnb_ref, gnw_ref, sink_ref,
                       mq1_ref, mqw_ref, mkw_ref, dm_ref, gt_ref, bias_ref,
                       y_ref, sto_ref, cko_ref, cvo_ref,
                       proj_ref, vh_ref, qs_ref, oh_ref, os_ref, merged_ref, *, sb, t):
    ht = RET_HEADS * t
    nrows = sb * t
    x = x_ref[...]
    proj_ref[...] = _dot(x.astype(BF16), win_ref[...])
    lane = lax.broadcasted_iota(jnp.int32, (nrows, LANES), 1)
    lo = lane < HEAD_DIM
    for p in range(PAIRS):
        v2 = proj_ref[:, _RV + p * LANES:_RV + (p + 1) * LANES]
        vh_ref[2 * p] = v2[:, :HEAD_DIM]
        vh_ref[2 * p + 1] = pltpu.roll(v2, HEAD_DIM, 1)[:, :HEAD_DIM]
        q2 = proj_ref[:, _SQ + p * LANES:_SQ + (p + 1) * LANES] * QK_SCALE
        zero = jnp.zeros_like(q2)
        qs_ref[p] = jnp.where(lo, q2, zero)
        qs_ref[SWA_GROUP + p] = jnp.where(lo, zero, q2)
    oh_ref[...] = jnp.zeros_like(oh_ref)
    knt = proj_ref[:, _SK:_SK + KV_W].T
    vnt = proj_ref[:, _SV:_SV + KV_W].T
    keep = lax.broadcasted_iota(jnp.int32, (KV_W, WINDOW), 1) < WINDOW - t

    mq1 = mq1_ref[...]
    mqw = mqw_ref[...]
    mkw = mkw_ref[...]
    dm = dm_ref[...]
    g_t = gt_ref[...]
    bias = bias_ref[...]
    sink = sink_ref[...]

    def per_seq(b, carry):
        r0 = pl.multiple_of(b * t, t)
        rows = pl.ds(r0, t)
        q_b = proj_ref[rows, _RQ:_RQ + RET_W]
        k_b = proj_ref[rows, _RK:_RK + RET_W]
        q_rep = jnp.concatenate([q_b] * RET_HEADS, axis=0)
        k_rep = jnp.concatenate([k_b] * RET_HEADS, axis=0)
        v_st = vh_ref[:, rows, :].reshape(ht, HEAD_DIM).astype(BF16)
        s0 = st_ref[b]
        sc = _dot_nt((q_rep * mq1).astype(BF16), k_rep.astype(BF16)) * dm
        o = _dot((q_rep * mqw).astype(BF16), s0.astype(BF16)) + _dot(sc.astype(BF16), v_st)
        oh_ref[:, rows, 0:HEAD_DIM] = o.reshape(RET_HEADS, t, HEAD_DIM)
        sto_ref[b] = g_t * s0 + _dot_tn((k_rep * mkw).astype(BF16), v_st)
        k_new = proj_ref[rows, _SK:_SK + KV_W]
        v_new = proj_ref[rows, _SV:_SV + KV_W]
        kt_old = ck_ref[b]
        vt_old = cv_ref[b]
        q_s = qs_ref[:, rows, :].reshape(ht, LANES).astype(BF16)
        s = jnp.concatenate([_dot(q_s, kt_old.astype(BF16)),
                             _dot_nt(q_s, k_new.astype(BF16))], axis=1) + bias
        m = jnp.maximum(jnp.max(s, axis=-1, keepdims=True), sink)
        e = jnp.exp(s - m)
        den = jnp.sum(e, axis=-1, keepdims=True) + jnp.exp(sink - m)
        eb = e.astype(BF16)
        o_s = (_dot_nt(eb[:, :WINDOW], vt_old.astype(BF16))
               + _dot(eb[:, WINDOW:], v_new.astype(BF16))) / den
        os_ref[:, rows, :] = o_s.reshape(SWA_HEADS, t, LANES)
        shift = WINDOW - t - r0
        cko_ref[b] = jnp.where(keep, pltpu.roll(kt_old, WINDOW - t, 1), pltpu.roll(knt, shift, 1))
        cvo_ref[b] = jnp.where(keep, pltpu.roll(vt_old, WINDOW - t, 1), pltpu.roll(vnt, shift, 1))
        return carry

    lax.fori_loop(0, sb, per_seq, 0, unroll=16)

    for p in range(PAIRS):
        cols = slice(p * LANES, (p + 1) * LANES)
        o2 = oh_ref[2 * p] + pltpu.roll(oh_ref[2 * p + 1], HEAD_DIM, 1)
        gn = _pair_group_norm(o2, lo, gnw_ref[:, cols])
        gate = proj_ref[:, _RG + p * LANES:_RG + (p + 1) * LANES]
        merged_ref[:, cols] = (gate * jax.nn.sigmoid(gate) * gn).astype(BF16)
        merged_ref[:, RET_W + p * LANES:RET_W + (p + 1) * LANES] = jnp.where(
            lo, os_ref[p], os_ref[SWA_GROUP + p]).astype(BF16)
    mix = _dot(merged_ref[...], wout_ref[...])
    y_ref[...] = _layer_norm(DN_ALPHA * x + mix, lng_ref[...], lnb_ref[...])


def _mixer_sample(x, state, ck, cv, w_in_p, w_out_p, lng, lnb, gnw, sink_rows, consts, *, sb, t):
    rows_total = x.shape[0]
    nseq = rows_total // t
    assert nseq % sb == 0 and sb * t == LANES and t % 8 == 0
    rows = sb * t
    weights = (w_in_p, w_out_p)
    body = functools.partial(_mixer_sample_body, sb=sb, t=t)
    return pl.pallas_call(
        body,
        grid=(nseq // sb,),
        in_specs=[
            pl.BlockSpec((rows, D_MODEL), lambda i: (i, 0)),
            pl.BlockSpec((sb, RET_W, HEAD_DIM), lambda i: (i, 0, 0)),
            pl.BlockSpec((sb, KV_W, WINDOW), lambda i: (i, 0, 0)),
            pl.BlockSpec((sb, KV_W, WINDOW), lambda i: (i, 0, 0)),
        ] + [_const_spec(w.shape) for w in weights] + [
            _const_spec(lng.shape), _const_spec(lnb.shape), _const_spec(gnw.shape),
            _const_spec(sink_rows.shape),
        ] + [_const_spec(c.shape) for c in consts],
        out_specs=[
            pl.BlockSpec((rows, D_MODEL), lambda i: (i, 0)),
            pl.BlockSpec((sb, RET_W, HEAD_DIM), lambda i: (i, 0, 0)),
            pl.BlockSpec((sb, KV_W, WINDOW), lambda i: (i, 0, 0)),
            pl.BlockSpec((sb, KV_W, WINDOW), lambda i: (i, 0, 0)),
        ],
        out_shape=[
            jax.ShapeDtypeStruct((rows_total, D_MODEL), F32),
            jax.ShapeDtypeStruct((nseq, RET_W, HEAD_DIM), F32),
            jax.ShapeDtypeStruct((nseq, KV_W, WINDOW), F32),
            jax.ShapeDtypeStruct((nseq, KV_W, WINDOW), F32),
        ],
        scratch_shapes=[
            pltpu.VMEM((rows, IN_COLS), F32),
            pltpu.VMEM((RET_HEADS, rows, HEAD_DIM), F32),
            pltpu.VMEM((SWA_HEADS, rows, LANES), F32),
            pltpu.VMEM((RET_HEADS, rows, LANES), F32),
            pltpu.VMEM((SWA_HEADS, rows, LANES), F32),
            pltpu.VMEM((rows, 2 * RET_W), BF16),
        ],
        compiler_params=pltpu.CompilerParams(
            dimension_semantics=("arbitrary",), vmem_limit_bytes=VMEM_LIMIT_BYTES),
        name="mixer_sample",
    )(x, state, ck, cv, *weights, lng, lnb, gnw, sink_rows, *consts)


def _prep_prompt_weights(w_in, w_out, sinks):
    sq = w_in[:, _SQ:_SQ + SWA_W].reshape(D_MODEL, SWA_HEADS, HEAD_DIM)
    sq = sq[:, jnp.asarray(_SWA_HEAD_ORDER), :].reshape(D_MODEL, SWA_W)
    w_in_p = jnp.concatenate([w_in[:, :_SQ], sq, w_in[:, _SK:]], axis=1).astype(BF16)
    wo_s = w_out[RET_W:].reshape(SWA_HEADS, HEAD_DIM, D_MODEL)
    wo_s = wo_s[jnp.asarray(_SWA_HEAD_ORDER)].reshape(SWA_W, D_MODEL)
    w_out_p = jnp.concatenate([w_out[:RET_W], wo_s], axis=0).astype(BF16)
    sink_st = jnp.repeat(sinks.astype(F32)[jnp.asarray(_SWA_HEAD_ORDER)], CHUNK)[None, :]
    return w_in_p, w_out_p, sink_st


def kernel(x_prompt, x_sample, state_ret, cache_swa_k, cache_swa_v, ln_gain, ln_bias, w_in,
           ret_gn_w, swa_sinks, w_out, ffn1_gate, ffn1_up, ffn1_down, ffn2_gate, ffn2_up, ffn2_down):
    assert ln_gain.shape[0] == DEPTH == 1
    bsz, seq, _ = x_prompt.shape
    nseq, t, _ = x_sample.shape
    lng = ln_gain[0].astype(F32).reshape(3, 1, D_MODEL)
    lnb = ln_bias[0].astype(F32).reshape(3, 1, D_MODEL)
    f1 = (ffn1_gate[0].astype(BF16), ffn1_up[0].astype(BF16), ffn1_down[0].astype(BF16))
    gnw = ret_gn_w[0].astype(F32)

    to_dims_major = lambda c: jnp.transpose(c, (0, 2, 3, 1)).reshape(c.shape[0], KV_W, WINDOW)
    to_pos_major = lambda c: jnp.transpose(
        c.reshape(c.shape[0], SWA_KV_HEADS, HEAD_DIM, WINDOW), (0, 3, 1, 2))[None]

    tm_p = min(1024, bsz * seq)
    tm_s = min(512, nseq * t)
    xp = x_prompt.reshape(bsz * seq, D_MODEL)
    xs = x_sample.reshape(nseq * t, D_MODEL)

    xp, f2 = _ffn_ln(xp, *f1, lng[0], lnb[0], tm=tm_p,
                     cast_next=(ffn2_gate[0], ffn2_up[0], ffn2_down[0]))
    w_in_p, w_out_p, sink_st = _prep_prompt_weights(w_in[0], w_out[0], swa_sinks[0])
    yp, st_p, ck_p, cv_p = _mixer_prompt(
        xp.reshape(bsz, seq, D_MODEL), w_in_p, w_out_p, lng[1], lnb[1], gnw.reshape(1, RET_W),
        sink_st, _prompt_constants(), tq=min(1024, seq))
    yp = _ffn_ln(yp.reshape(bsz * seq, D_MODEL), *f2, lng[2], lnb[2], tm=tm_p)

    xs = _ffn_ln(xs, *f1, lng[0], lnb[0], tm=tm_s)
    sink_rows = jnp.repeat(swa_sinks[0].astype(F32), t)[:, None]
    ys, st_s, ck_s, cv_s = _mixer_sample(
        xs, state_ret[0].astype(F32).reshape(nseq, RET_W, HEAD_DIM),
        to_dims_major(cache_swa_k[0]), to_dims_major(cache_swa_v[0]),
        w_in_p, w_out_p, lng[1], lnb[1], gnw.reshape(1, RET_W), sink_rows, _sample_constants(t),
        sb=LANES // t, t=t)
    ys = _ffn_ln(ys, *f2, lng[2], lnb[2], tm=tm_s)

    return (yp.reshape(bsz, seq, D_MODEL), ys.reshape(nseq, t, D_MODEL),
            st_p[None], to_pos_major(ck_p), to_pos_major(cv_p),
            st_s.reshape(1, nseq, RET_HEADS, HEAD_DIM, HEAD_DIM),
            to_pos_major(ck_s), to_pos_major(cv_s))
```

```python
import functools

import numpy as np
import jax
import jax.numpy as jnp
from jax import lax
from jax.experimental import pallas as pl
from jax.experimental.pallas import tpu as pltpu

F32 = jnp.float32
BF16 = jnp.bfloat16

D_MODEL = 1024
HEAD_DIM = 64
RET_HEADS = 8
SWA_HEADS = 8
SWA_KV_HEADS = 2
SWA_GROUP = SWA_HEADS // SWA_KV_HEADS
WINDOW = 128
CHUNK = 128
D_FF = 2816
RET_W = RET_HEADS * HEAD_DIM
SWA_W = SWA_HEADS * HEAD_DIM
KV_W = SWA_KV_HEADS * HEAD_DIM
IN_COLS = 4 * RET_W + SWA_W + 2 * KV_W
LN_EPS = 1e-5
GN_EPS = 1e-5
DEPTH = 1
DN_ALPHA = (2.0 * DEPTH) ** 0.25
QK_SCALE = HEAD_DIM ** -0.5
NEG_BIG = -1e30
LOG2E = 1.4426950408889634

LANES = 128
PAIRS = RET_HEADS // 2
GROUPS = RET_HEADS // 4
GROUP_W = 4 * HEAD_DIM
PROJ_ROWS = 256
PROJ_COLS = 512
VMEM_LIMIT_BYTES = 56 * 1024 * 1024

_RQ, _RK, _RV, _RG, _SQ = 0, RET_W, 2 * RET_W, 3 * RET_W, 4 * RET_W
_SK = 4 * RET_W + SWA_W
_SV = _SK + KV_W

_SWA_PAIR_HEADS = [(p, SWA_GROUP + p) for p in range(PAIRS)]
_SWA_HEAD_ORDER = [h for pair in _SWA_PAIR_HEADS for h in pair]

_FF_CHUNKS = ((0, 2816),)
FFN_ROW_BLOCK = 256


def _log_gamma():
    h = np.arange(RET_HEADS, dtype=np.float64)
    return np.log1p(-np.exp2(-5.0 - h))


def _alibi_slopes():
    return np.exp2(-8.0 / SWA_HEADS * np.arange(1, SWA_HEADS + 1, dtype=np.float64))


def _layer_norm(z, g, b):
    mu = jnp.mean(z, axis=-1, keepdims=True)
    d = z - mu
    var = jnp.mean(d * d, axis=-1, keepdims=True)
    return d * lax.rsqrt(var + LN_EPS) * g + b


def _pair_group_norm(o, lo, gain):
    zero = jnp.zeros_like(o)
    s_lo = jnp.sum(jnp.where(lo, o, zero), axis=-1, keepdims=True)
    s_hi = jnp.sum(jnp.where(lo, zero, o), axis=-1, keepdims=True)
    d = o - jnp.where(lo, s_lo, s_hi) * (1.0 / HEAD_DIM)
    d2 = d * d
    v_lo = jnp.sum(jnp.where(lo, d2, zero), axis=-1, keepdims=True)
    v_hi = jnp.sum(jnp.where(lo, zero, d2), axis=-1, keepdims=True)
    var = jnp.where(lo, v_lo, v_hi) * (1.0 / HEAD_DIM)
    return d * lax.rsqrt(var + GN_EPS) * gain


def _dot(a, b):
    return jnp.dot(a, b, preferred_element_type=F32)


def _dot_nt(a, b):
    return lax.dot_general(a, b, (((1,), (1,)), ((), ())), preferred_element_type=F32)


def _dot_tn(a, b):
    return lax.dot_general(a, b, (((0,), (0,)), ((), ())), preferred_element_type=F32)


def _ffn_ln_body(*refs, row_block, ncast):
    x_ref, wg_ref, wu_ref, wd_ref, g_ref, b_ref = refs[:6]
    o_ref = refs[6 + ncast]
    side_job_at = row_block if x_ref.shape[0] > row_block else 0
    for r0 in range(0, x_ref.shape[0], row_block):
        if r0 == side_job_at:
            for src, dst in zip(refs[6:6 + ncast], refs[7 + ncast:]):
                dst[...] = src[...].astype(BF16)
        x = x_ref[r0:r0 + row_block, :]
        xb = x.astype(BF16)
        acc = None
        for c0, cw in _FF_CHUNKS:
            gate = _dot(xb, wg_ref[:, c0:c0 + cw])
            up = _dot(xb, wu_ref[:, c0:c0 + cw])
            hid = (gate * jax.nn.sigmoid(gate) * up).astype(BF16)
            part = _dot(hid, wd_ref[c0:c0 + cw, :])
            acc = part if acc is None else acc + part
        z = DN_ALPHA * x + 0.5 * acc
        o_ref[r0:r0 + row_block, :] = _layer_norm(z, g_ref[...], b_ref[...])


def _const_spec(shape):
    nd = len(shape)
    return pl.BlockSpec(shape, lambda *_: (0,) * nd, pipeline_mode=pl.Buffered(1))


def _ffn_ln(x, wg, wu, wd, g, b, *, tm, cast_next=()):
    rows = x.shape[0]
    assert rows % tm == 0 and tm % FFN_ROW_BLOCK == 0
    steps = rows // tm
    slab_specs = []
    for w in cast_next:
        nslabs = max(n for n in range(1, steps + 1)
                     if steps % n == 0 and w.shape[0] % n == 0 and (w.shape[0] // n) % 16 == 0)
        every = steps // nslabs
        slab_specs.append(pl.BlockSpec((w.shape[0] // nslabs, w.shape[1]),
                                       functools.partial(lambda i, e: (i // e, 0), e=every)))
    outs = pl.pallas_call(
        functools.partial(_ffn_ln_body, row_block=FFN_ROW_BLOCK, ncast=len(cast_next)),
        grid=(steps,),
        in_specs=[
            pl.BlockSpec((tm, D_MODEL), lambda i: (i, 0)),
            _const_spec((D_MODEL, D_FF)),
            _const_spec((D_MODEL, D_FF)),
            _const_spec((D_FF, D_MODEL)),
            _const_spec((1, D_MODEL)),
            _const_spec((1, D_MODEL)),
        ] + slab_specs,
        out_specs=[pl.BlockSpec((tm, D_MODEL), lambda i: (i, 0))] + slab_specs,
        out_shape=[jax.ShapeDtypeStruct((rows, D_MODEL), F32)]
        + [jax.ShapeDtypeStruct(w.shape, BF16) for w in cast_next],
        compiler_params=pltpu.CompilerParams(
            dimension_semantics=("arbitrary",), vmem_limit_bytes=VMEM_LIMIT_BYTES),
        name="ffn_ln",
    )(x, wg, wu, wd, g, b, *cast_next)
    if not cast_next:
        return outs[0]
    return outs[0], tuple(outs[1:])


def _prompt_constants():
    lg = _log_gamma()
    idx = np.arange(CHUNK, dtype=np.float64)
    diff = idx[:, None] - idx[None, :]
    dm = np.where(diff >= 0, np.exp(np.where(diff >= 0, diff, 0.0)[None] * lg[:, None, None]), 0.0)
    dm_st = np.stack([np.block([[dm[4 * g], dm[4 * g + 2]], [dm[4 * g + 1], dm[4 * g + 3]]])
                      for g in range(GROUPS)])
    qw = np.repeat(np.exp((idx + 1.0)[:, None] * lg[None, :]), HEAD_DIM, axis=1)
    kw = np.repeat(np.exp((CHUNK - 1.0 - idx)[:, None] * lg[None, :]), HEAD_DIM, axis=1) * QK_SCALE
    gc = np.exp(CHUNK * lg)
    blk = np.kron(np.eye(4), np.ones((HEAD_DIM, HEAD_DIM)))
    gbd = np.stack([blk * np.repeat(gc[4 * g:4 * g + 4], HEAD_DIM)[:, None] for g in range(GROUPS)])
    slopes = _alibi_slopes()
    i = np.arange(CHUNK)
    j = np.arange(2 * CHUNK)
    dist = i[:, None] + CHUNK - j[None, :]
    within = (dist >= 0) & (dist <= WINDOW)
    valid = [within & (j >= CHUNK)[None, :], within]
    bias = np.stack([np.concatenate([np.where(valid[v], -slopes[h] * dist * LOG2E, NEG_BIG).T
                                     for h in _SWA_HEAD_ORDER], axis=1) for v in range(2)])
    f = lambda a: jnp.asarray(a, F32)
    return f(dm_st), f(qw), f(kw), f(gbd), f(blk), f(bias)


def _mixer_prompt_body(x_ref, xn_ref, win_ref, wout_ref, lng_ref, lnb_ref, gnw_ref, sink_ref,
                       dm_ref, qw_ref, kw_ref, gbd_ref, blk_ref, bias_ref,
                       y_ref, st_ref, ck_ref, cv_ref,
                       pb_ref, proj1_ref, oret_ref, yret_ref, swat_ref, s_ref, kprev_ref, vprev_ref,
                       *, tq):
    step = pl.program_id(1)
    nsteps = pl.num_programs(1)
    flat = pl.program_id(0) * nsteps + step
    slot = lax.rem(flat, 2)
    nchunks = tq // CHUNK
    nblocks = tq // PROJ_ROWS
    col_starts = range(0, IN_COLS, PROJ_COLS)

    @pl.when(step == 0)
    def _():
        s_ref[...] = jnp.zeros_like(s_ref)
        kprev_ref[...] = jnp.zeros_like(kprev_ref)
        vprev_ref[...] = jnp.zeros_like(vprev_ref)

    x = x_ref[0]

    def project_piece(dst, xrows, c0):
        c1 = min(c0 + PROJ_COLS, IN_COLS)
        dst[:, c0:c1] = _dot(xrows.astype(BF16), win_ref[:, c0:c1])

    @pl.when(flat == 0)
    def _():
        for c0 in col_starts:
            project_piece(pb_ref.at[0], x[:PROJ_ROWS], c0)

    def proj(r0, nrows, c0, c1):
        if r0 < PROJ_ROWS:
            return pb_ref[slot, r0:r0 + nrows, c0:c1]
        return proj1_ref[r0 - PROJ_ROWS:r0 - PROJ_ROWS + nrows, c0:c1]

    pending = []

    def emit_pieces(n):
        for _ in range(min(n, len(pending))):
            pending.pop(0)()

    pieces_per_chunk = pl.cdiv(len(col_starts), PROJ_ROWS // CHUNK)

    lane = lax.broadcasted_iota(jnp.int32, (CHUNK, LANES), 1)
    lo = lane < HEAD_DIM
    lane_g = lax.broadcasted_iota(jnp.int32, (CHUNK, GROUP_W), 1)
    lo_g = jnp.bitwise_and(lane_g, LANES - 1) < HEAD_DIM
    blk = blk_ref[...]
    zero_b = jnp.zeros((CHUNK, LANES), BF16)

    def pair_diag(a):
        return jnp.concatenate([jnp.concatenate([a[:, :LANES], zero_b], axis=1),
                                jnp.concatenate([zero_b, a[:, LANES:]], axis=1)], axis=0)

    def head_stack(a):
        zero = jnp.zeros_like(a)
        return jnp.concatenate([jnp.where(lo_g, a, zero), jnp.where(lo_g, zero, a)], axis=0)

    lane_t = lax.broadcasted_iota(jnp.int32, (PROJ_ROWS, LANES), 1)
    lo_t = lane_t < HEAD_DIM

    def finish_block(kb):
        rb = slice(kb * PROJ_ROWS, (kb + 1) * PROJ_ROWS)
        for p in range(PAIRS):
            cols = slice(p * LANES, (p + 1) * LANES)
            gn = _pair_group_norm(oret_ref[rb, cols], lo_t, gnw_ref[:, cols])
            gate = proj(kb * PROJ_ROWS, PROJ_ROWS, _RG + p * LANES, _RG + (p + 1) * LANES)
            yret_ref[rb, cols] = (gate * jax.nn.sigmoid(gate) * gn).astype(BF16)
        mix = (_dot(yret_ref[rb, :], wout_ref[:RET_W, :])
               + _dot_tn(swat_ref[:, rb].astype(BF16), wout_ref[RET_W:, :]))
        y_ref[0, rb, :] = _layer_norm(DN_ALPHA * x[rb] + mix, lng_ref[...], lnb_ref[...])

    k_prev = kprev_ref[...]
    vt_prev = vprev_ref[...]
    for c in range(nchunks):
        r0 = c * CHUNK
        rows = slice(r0, r0 + CHUNK)
        if r0 % PROJ_ROWS == 0:
            nb = r0 // PROJ_ROWS + 1
            if nb < nblocks:
                dst = proj1_ref.at[(nb - 1) * PROJ_ROWS:nb * PROJ_ROWS]
                xrows = x[nb * PROJ_ROWS:(nb + 1) * PROJ_ROWS]
            else:
                dst, xrows = pb_ref.at[1 - slot], xn_ref[0]
            pending.extend(functools.partial(project_piece, dst, xrows, c0) for c0 in col_starts)
        budget = pieces_per_chunk
        for g in range(GROUPS):
            if budget > 0:
                emit_pieces(1)
                budget -= 1
            cols = slice(g * GROUP_W, (g + 1) * GROUP_W)
            q4 = proj(r0, CHUNK, _RQ + g * GROUP_W, _RQ + (g + 1) * GROUP_W)
            k4 = proj(r0, CHUNK, _RK + g * GROUP_W, _RK + (g + 1) * GROUP_W)
            v4 = proj(r0, CHUNK, _RV + g * GROUP_W, _RV + (g + 1) * GROUP_W).astype(BF16)
            k_bd = pair_diag((k4 * QK_SCALE).astype(BF16))
            sc = _dot_nt(head_stack(q4).astype(BF16), k_bd) * dm_ref[g]
            qd_st = head_stack(q4 * qw_ref[:, cols])
            lhs = jnp.concatenate([sc.astype(BF16), qd_st.astype(BF16)], axis=1)
            s_old = s_ref[g]
            rhs = jnp.concatenate([pair_diag(v4), s_old.astype(BF16)], axis=0)
            r = _dot(lhs, rhs)
            oret_ref[rows, cols] = jnp.where(lo_g, r[:CHUNK], r[CHUNK:])
            kd = (k4 * kw_ref[:, cols]).astype(BF16)
            s_ref[g] = gbd_ref[g] * s_old + blk * _dot_tn(kd, v4)
        k_cur = proj(r0, CHUNK, _SK, _SK + KV_W).astype(BF16)
        vt_cur = proj(r0, CHUNK, _SV, _SV + KV_W).T.astype(BF16)
        kk = jnp.concatenate([k_prev, k_cur], axis=0)
        vvt = jnp.concatenate([vt_prev, vt_cur], axis=1)
        q_parts = []
        for p in range(PAIRS):
            q2 = proj(r0, CHUNK, _SQ + p * LANES, _SQ + (p + 1) * LANES) * (QK_SCALE * LOG2E)
            zero = jnp.zeros_like(q2)
            q_parts += [jnp.where(lo, q2, zero).astype(BF16), jnp.where(lo, zero, q2).astype(BF16)]
        st = _dot_nt(kk, jnp.concatenate(q_parts, axis=0))
        if c == 0:
            st = st + bias_ref[jnp.minimum(step, 1)]
        else:
            st = st + bias_ref[1]
        sink = sink_ref[...] * LOG2E
        e_parts, den_parts = [], []
        for p in range(PAIRS):
            if budget > 0:
                emit_pieces(1)
                budget -= 1
            pc = slice(2 * p * CHUNK, 2 * (p + 1) * CHUNK)
            st_p, sink_p = st[:, pc], sink[:, pc]
            m = jnp.maximum(jnp.max(st_p, axis=0, keepdims=True), sink_p)
            e = jnp.exp2(st_p - m)
            den_parts.append(jnp.sum(e, axis=0, keepdims=True) + jnp.exp2(sink_p - m))
            e_parts.append(e.astype(BF16))
        if (r0 + CHUNK) % PROJ_ROWS == 0:
            emit_pieces(len(pending))
        den = jnp.concatenate(den_parts, axis=1)
        ot = _dot(vvt, jnp.concatenate(e_parts, axis=1)) * (1.0 / den)
        for p in range(PAIRS):
            c0 = 2 * p * CHUNK
            swat_ref[p * LANES:p * LANES + HEAD_DIM, rows] = ot[:HEAD_DIM, c0:c0 + CHUNK]
            swat_ref[p * LANES + HEAD_DIM:(p + 1) * LANES, rows] = ot[HEAD_DIM:, c0 + CHUNK:c0 + 2 * CHUNK]
        k_prev, vt_prev = k_cur, vt_cur
        if (r0 + CHUNK) % PROJ_ROWS == 0:
            finish_block(r0 // PROJ_ROWS)
    kprev_ref[...] = k_prev
    vprev_ref[...] = vt_prev

    @pl.when(step == nsteps - 1)
    def _():
        for h in range(RET_HEADS):
            g, i = divmod(h, 4)
            blk_h = s_ref[g, i * HEAD_DIM:(i + 1) * HEAD_DIM, (i // 2) * LANES:(i // 2 + 1) * LANES]
            if i % 2:
                blk_h = pltpu.roll(blk_h, HEAD_DIM, 1)
            st_ref[0, h] = blk_h[:, :HEAD_DIM]
        ck_ref[0] = proj(tq - WINDOW, WINDOW, _SK, _SK + KV_W).T
        cv_ref[0] = proj(tq - WINDOW, WINDOW, _SV, _SV + KV_W).T


def _mixer_prompt(x, w_in_p, w_out_p, lng, lnb, gnw, sink_st, consts, *, tq):
    bsz, seq, _ = x.shape
    assert seq % tq == 0 and tq % PROJ_ROWS == 0 and PROJ_ROWS % CHUNK == 0
    dm_st, qw, kw, gbd, blk, bias = consts
    body = functools.partial(_mixer_prompt_body, tq=tq)
    nsteps = seq // tq
    blocks_per_tile = tq // PROJ_ROWS

    def next_tile_head(b, s):
        nxt = jnp.minimum(b * nsteps + s + 1, bsz * nsteps - 1)
        return nxt // nsteps, (nxt % nsteps) * blocks_per_tile, 0

    return pl.pallas_call(
        body,
        grid=(bsz, nsteps),
        in_specs=[
            pl.BlockSpec((1, tq, D_MODEL), lambda b, s: (b, s, 0)),
            pl.BlockSpec((1, PROJ_ROWS, D_MODEL), next_tile_head),
            _const_spec((D_MODEL, IN_COLS)),
            _const_spec((2 * RET_W, D_MODEL)),
            _const_spec((1, D_MODEL)),
            _const_spec((1, D_MODEL)),
            _const_spec((1, RET_W)),
            _const_spec((1, SWA_HEADS * CHUNK)),
            _const_spec(dm_st.shape),
            _const_spec(qw.shape),
            _const_spec(kw.shape),
            _const_spec(gbd.shape),
            _const_spec(blk.shape),
            _const_spec(bias.shape),
        ],
        out_specs=[
            pl.BlockSpec((1, tq, D_MODEL), lambda b, s: (b, s, 0)),
            pl.BlockSpec((1, RET_HEADS, HEAD_DIM, HEAD_DIM), lambda b, s: (b, 0, 0, 0)),
            pl.BlockSpec((1, KV_W, WINDOW), lambda b, s: (b, 0, 0)),
            pl.BlockSpec((1, KV_W, WINDOW), lambda b, s: (b, 0, 0)),
        ],
        out_shape=[
            jax.ShapeDtypeStruct((bsz, seq, D_MODEL), F32),
            jax.ShapeDtypeStruct((bsz, RET_HEADS, HEAD_DIM, HEAD_DIM), F32),
            jax.ShapeDtypeStruct((bsz, KV_W, WINDOW), F32),
            jax.ShapeDtypeStruct((bsz, KV_W, WINDOW), F32),
        ],
        scratch_shapes=[
            pltpu.VMEM((2, PROJ_ROWS, IN_COLS), F32),
            pltpu.VMEM((tq - PROJ_ROWS, IN_COLS), F32),
            pltpu.VMEM((tq, RET_W), F32),
            pltpu.VMEM((tq, RET_W), BF16),
            pltpu.VMEM((SWA_W, tq), F32),
            pltpu.VMEM((GROUPS, GROUP_W, GROUP_W), F32),
            pltpu.VMEM((CHUNK, KV_W), BF16),
            pltpu.VMEM((KV_W, CHUNK), BF16),
        ],
        compiler_params=pltpu.CompilerParams(
            dimension_semantics=("arbitrary", "arbitrary"), vmem_limit_bytes=VMEM_LIMIT_BYTES),
        name="mixer_prompt",
    )(x, x, w_in_p, w_out_p, lng, lnb, gnw, sink_st, dm_st, qw, kw, gbd, blk, bias)


def _sample_constants(t):
    lg = _log_gamma()
    hh = np.arange(RET_HEADS)
    tt = np.arange(t, dtype=np.float64)
    row_h = np.repeat(hh, t)
    row_t = np.tile(tt, RET_HEADS)
    col_h = np.repeat(hh, HEAD_DIM)
    same = (row_h[:, None] == col_h[None, :]).astype(np.float64)
    mask_q1 = same * QK_SCALE
    mask_qw = same * np.exp((row_t + 1.0) * lg[row_h])[:, None]
    mask_kw = same * (np.exp((t - 1.0 - row_t) * lg[row_h]) * QK_SCALE)[:, None]
    diff = row_t[:, None] - row_t[None, :]
    same_h = row_h[:, None] == row_h[None, :]
    dm = np.where(same_h & (diff >= 0), np.exp(np.where(diff >= 0, diff, 0.0) * lg[row_h][:, None]), 0.0)
    g_t = np.repeat(np.exp(t * lg), HEAD_DIM)[:, None] * np.ones((1, HEAD_DIM))
    slopes = _alibi_slopes()
    j = np.arange(WINDOW + t)
    dist = row_t[:, None] + WINDOW - j[None, :]
    valid = (dist >= 0) & (dist <= WINDOW)
    bias = np.where(valid, -slopes[row_h][:, None] * dist, NEG_BIG)
    f = lambda a: jnp.asarray(a, F32)
    return f(mask_q1), f(mask_qw), f(mask_kw), f(dm), f(g_t), f(bias)


def _mixer_sample_body(x_ref, st_ref, ck_ref, cv_ref, win_ref, wout_ref,
                       lng_ref, lnb_ref, gnw_ref, sink_ref,
                       mq1_ref, mqw_ref, mkw_ref, dm_ref, gt_ref, bias_ref,
                       y_ref, sto_ref, cko_ref, cvo_ref,
                       proj_ref, vh_ref, qs_ref, oh_ref, os_ref, merged_ref, *, sb, t):
    ht = RET_HEADS * t
    nrows = sb * t
    x = x_ref[...]
    proj_ref[...] = _dot(x.astype(BF16), win_ref[...])
    lane = lax.broadcasted_iota(jnp.int32, (nrows, LANES), 1)
    lo = lane < HEAD_DIM
    for p in range(PAIRS):
        v2 = proj_ref[:, _RV + p * LANES:_RV + (p + 1) * LANES]
        vh_ref[2 * p] = v2[:, :HEAD_DIM]
        vh_ref[2 * p + 1] = pltpu.roll(v2, HEAD_DIM, 1)[:, :HEAD_DIM]
        q2 = proj_ref[:, _SQ + p * LANES:_SQ + (p + 1) * LANES] * QK_SCALE
        zero = jnp.zeros_like(q2)
        qs_ref[p] = jnp.where(lo, q2, zero)
        qs_ref[SWA_GROUP + p] = jnp.where(lo, zero, q2)
    oh_ref[...] = jnp.zeros_like(oh_ref)
    knt = proj_ref[:, _SK:_SK + KV_W].T
    vnt = proj_ref[:, _SV:_SV + KV_W].T
    keep = lax.broadcasted_iota(jnp.int32, (KV_W, WINDOW), 1) < WINDOW - t

    mq1 = mq1_ref[...]
    mqw = mqw_ref[...]
    mkw = mkw_ref[...]
    dm = dm_ref[...]
    g_t = gt_ref[...]
    bias = bias_ref[...]
    sink = sink_ref[...]

    def per_seq(b, carry):
        r0 = pl.multiple_of(b * t, t)
        rows = pl.ds(r0, t)
        q_b = proj_ref[rows, _RQ:_RQ + RET_W]
        k_b = proj_ref[rows, _RK:_RK + RET_W]
        q_rep = jnp.concatenate([q_b] * RET_HEADS, axis=0)
        k_rep = jnp.concatenate([k_b] * RET_HEADS, axis=0)
        v_st = vh_ref[:, rows, :].reshape(ht, HEAD_DIM).astype(BF16)
        s0 = st_ref[b]
        sc = _dot_nt((q_rep * mq1).astype(BF16), k_rep.astype(BF16)) * dm
        o = _dot((q_rep * mqw).astype(BF16), s0.astype(BF16)) + _dot(sc.astype(BF16), v_st)
        oh_ref[:, rows, 0:HEAD_DIM] = o.reshape(RET_HEADS, t, HEAD_DIM)
        sto_ref[b] = g_t * s0 + _dot_tn((k_rep * mkw).astype(BF16), v_st)
        k_new = proj_ref[rows, _SK:_SK + KV_W]
        v_new = proj_ref[rows, _SV:_SV + KV_W]
        kt_old = ck_ref[b]
        vt_old = cv_ref[b]
        q_s = qs_ref[:, rows, :].reshape(ht, LANES).astype(BF16)
        s = jnp.concatenate([_dot(q_s, kt_old.astype(BF16)),
                             _dot_nt(q_s, k_new.astype(BF16))], axis=1) + bias
        m = jnp.maximum(jnp.max(s, axis=-1, keepdims=True), sink)
        e = jnp.exp(s - m)
        den = jnp.sum(e, axis=-1, keepdims=True) + jnp.exp(sink - m)
        eb = e.astype(BF16)
        o_s = (_dot_nt(eb[:, :WINDOW], vt_old.astype(BF16))
               + _dot(eb[:, WINDOW:], v_new.astype(BF16))) / den
        os_ref[:, rows, :] = o_s.reshape(SWA_HEADS, t, LANES)
        shift = WINDOW - t - r0
        cko_ref[b] = jnp.where(keep, pltpu.roll(kt_old, WINDOW - t, 1), pltpu.roll(knt, shift, 1))
        cvo_ref[b] = jnp.where(keep, pltpu.roll(vt_old, WINDOW - t, 1), pltpu.roll(vnt, shift, 1))
        return carry

    lax.fori_loop(0, sb, per_seq, 0, unroll=16)

    for p in range(PAIRS):
        cols = slice(p * LANES, (p + 1) * LANES)
        o2 = oh_ref[2 * p] + pltpu.roll(oh_ref[2 * p + 1], HEAD_DIM, 1)
        gn = _pair_group_norm(o2, lo, gnw_ref[:, cols])
        gate = proj_ref[:, _RG + p * LANES:_RG + (p + 1) * LANES]
        merged_ref[:, cols] = (gate * jax.nn.sigmoid(gate) * gn).astype(BF16)
        merged_ref[:, RET_W + p * LANES:RET_W + (p + 1) * LANES] = jnp.where(
            lo, os_ref[p], os_ref[SWA_GROUP + p]).astype(BF16)
    mix = _dot(merged_ref[...], wout_ref[...])
    y_ref[...] = _layer_norm(DN_ALPHA * x + mix, lng_ref[...], lnb_ref[...])


def _mixer_sample(x, state, ck, cv, w_in_p, w_out_p, lng, lnb, gnw, sink_rows, consts, *, sb, t):
    rows_total = x.shape[0]
    nseq = rows_total // t
    assert nseq % sb == 0 and sb * t == LANES and t % 8 == 0
    rows = sb * t
    weights = (w_in_p, w_out_p)
    body = functools.partial(_mixer_sample_body, sb=sb, t=t)
    return pl.pallas_call(
        body,
        grid=(nseq // sb,),
        in_specs=[
            pl.BlockSpec((rows, D_MODEL), lambda i: (i, 0)),
            pl.BlockSpec((sb, RET_W, HEAD_DIM), lambda i: (i, 0, 0)),
            pl.BlockSpec((sb, KV_W, WINDOW), lambda i: (i, 0, 0)),
            pl.BlockSpec((sb, KV_W, WINDOW), lambda i: (i, 0, 0)),
        ] + [_const_spec(w.shape) for w in weights] + [
            _const_spec(lng.shape), _const_spec(lnb.shape), _const_spec(gnw.shape),
            _const_spec(sink_rows.shape),
        ] + [_const_spec(c.shape) for c in consts],
        out_specs=[
            pl.BlockSpec((rows, D_MODEL), lambda i: (i, 0)),
            pl.BlockSpec((sb, RET_W, HEAD_DIM), lambda i: (i, 0, 0)),
            pl.BlockSpec((sb, KV_W, WINDOW), lambda i: (i, 0, 0)),
            pl.BlockSpec((sb, KV_W, WINDOW), lambda i: (i, 0, 0)),
        ],
        out_shape=[
            jax.ShapeDtypeStruct((rows_total, D_MODEL), F32),
            jax.ShapeDtypeStruct((nseq, RET_W, HEAD_DIM), F32),
            jax.ShapeDtypeStruct((nseq, KV_W, WINDOW), F32),
            jax.ShapeDtypeStruct((nseq, KV_W, WINDOW), F32),
        ],
        scratch_shapes=[
            pltpu.VMEM((rows, IN_COLS), F32),
            pltpu.VMEM((RET_HEADS, rows, HEAD_DIM), F32),
            pltpu.VMEM((SWA_HEADS, rows, LANES), F32),
            pltpu.VMEM((RET_HEADS, rows, LANES), F32),
            pltpu.VMEM((SWA_HEADS, rows, LANES), F32),
            pltpu.VMEM((rows, 2 * RET_W), BF16),
        ],
        compiler_params=pltpu.CompilerParams(
            dimension_semantics=("arbitrary",), vmem_limit_bytes=VMEM_LIMIT_BYTES),
        name="mixer_sample",
    )(x, state, ck, cv, *weights, lng, lnb, gnw, sink_rows, *consts)


def _prep_prompt_weights(w_in, w_out, sinks):
    sq = w_in[:, _SQ:_SQ + SWA_W].reshape(D_MODEL, SWA_HEADS, HEAD_DIM)
    sq = sq[:, jnp.asarray(_SWA_HEAD_ORDER), :].reshape(D_MODEL, SWA_W)
    w_in_p = jnp.concatenate([w_in[:, :_SQ], sq, w_in[:, _SK:]], axis=1).astype(BF16)
    wo_s = w_out[RET_W:].reshape(SWA_HEADS, HEAD_DIM, D_MODEL)
    wo_s = wo_s[jnp.asarray(_SWA_HEAD_ORDER)].reshape(SWA_W, D_MODEL)
    w_out_p = jnp.concatenate([w_out[:RET_W], wo_s], axis=0).astype(BF16)
    sink_st = jnp.repeat(sinks.astype(F32)[jnp.asarray(_SWA_HEAD_ORDER)], CHUNK)[None, :]
    return w_in_p, w_out_p, sink_st


def kernel(x_prompt, x_sample, state_ret, cache_swa_k, cache_swa_v, ln_gain, ln_bias, w_in,
           ret_gn_w, swa_sinks, w_out, ffn1_gate, ffn1_up, ffn1_down, ffn2_gate, ffn2_up, ffn2_down):
    assert ln_gain.shape[0] == DEPTH == 1
    bsz, seq, _ = x_prompt.shape
    nseq, t, _ = x_sample.shape
    lng = ln_gain[0].astype(F32).reshape(3, 1, D_MODEL)
    lnb = ln_bias[0].astype(F32).reshape(3, 1, D_MODEL)
    f1 = (ffn1_gate[0].astype(BF16), ffn1_up[0].astype(BF16), ffn1_down[0].astype(BF16))
    gnw = ret_gn_w[0].astype(F32)

    to_dims_major = lambda c: jnp.transpose(c, (0, 2, 3, 1)).reshape(c.shape[0], KV_W, WINDOW)
    to_pos_major = lambda c: jnp.transpose(
        c.reshape(c.shape[0], SWA_KV_HEADS, HEAD_DIM, WINDOW), (0, 3, 1, 2))[None]

    tm_p = min(1024, bsz * seq)
    tm_s = min(512, nseq * t)
    xp = x_prompt.reshape(bsz * seq, D_MODEL)
    xs = x_sample.reshape(nseq * t, D_MODEL)

    xp, f2 = _ffn_ln(xp, *f1, lng[0], lnb[0], tm=tm_p,
                     cast_next=(ffn2_gate[0], ffn2_up[0], ffn2_down[0]))
    w_in_p, w_out_p, sink_st = _prep_prompt_weights(w_in[0], w_out[0], swa_sinks[0])
    yp, st_p, ck_p, cv_p = _mixer_prompt(
        xp.reshape(bsz, seq, D_MODEL), w_in_p, w_out_p, lng[1], lnb[1], gnw.reshape(1, RET_W),
        sink_st, _prompt_constants(), tq=min(1024, seq))
    yp = _ffn_ln(yp.reshape(bsz * seq, D_MODEL), *f2, lng[2], lnb[2], tm=tm_p)

    xs = _ffn_ln(xs, *f1, lng[0], lnb[0], tm=tm_s)
    sink_rows = jnp.repeat(swa_sinks[0].astype(F32), t)[:, None]
    ys, st_s, ck_s, cv_s = _mixer_sample(
        xs, state_ret[0].astype(F32).reshape(nseq, RET_W, HEAD_DIM),
        to_dims_major(cache_swa_k[0]), to_dims_major(cache_swa_v[0]),
        w_in_p, w_out_p, lng[1], lnb[1], gnw.reshape(1, RET_W), sink_rows, _sample_constants(t),
        sb=LANES // t, t=t)
    ys = _ffn_ln(ys, *f2, lng[2], lnb[2], tm=tm_s)

    return (yp.reshape(bsz, seq, D_MODEL), ys.reshape(nseq, t, D_MODEL),
            st_p[None], to_pos_major(ck_p), to_pos_major(cv_p),
            st_s.reshape(1, nseq, RET_HEADS, HEAD_DIM, HEAD_DIM),
            to_pos_major(ck_s), to_pos_major(cv_s))
```

```python
import functools

import numpy as np
import jax
import jax.numpy as jnp
from jax import lax
from jax.experimental import pallas as pl
from jax.experimental.pallas import tpu as pltpu

F32 = jnp.float32
BF16 = jnp.bfloat16

D_MODEL = 1024
HEAD_DIM = 64
RET_HEADS = 8
SWA_HEADS = 8
SWA_KV_HEADS = 2
SWA_GROUP = SWA_HEADS // SWA_KV_HEADS
WINDOW = 128
CHUNK = 128
D_FF = 2816
RET_W = RET_HEADS * HEAD_DIM
SWA_W = SWA_HEADS * HEAD_DIM
KV_W = SWA_KV_HEADS * HEAD_DIM
IN_COLS = 4 * RET_W + SWA_W + 2 * KV_W
LN_EPS = 1e-5
GN_EPS = 1e-5
DEPTH = 1
DN_ALPHA = (2.0 * DEPTH) ** 0.25
QK_SCALE = HEAD_DIM ** -0.5
NEG_BIG = -1e30
LOG2E = 1.4426950408889634

LANES = 128
PAIRS = RET_HEADS // 2
GROUPS = RET_HEADS // 4
GROUP_W = 4 * HEAD_DIM
PROJ_ROWS = 256
PROJ_COLS = 256
VMEM_LIMIT_BYTES = 56 * 1024 * 1024

_RQ, _RK, _RV, _RG, _SQ = 0, RET_W, 2 * RET_W, 3 * RET_W, 4 * RET_W
_SK = 4 * RET_W + SWA_W
_SV = _SK + KV_W

_SWA_PAIR_HEADS = [(p, SWA_GROUP + p) for p in range(PAIRS)]
_SWA_HEAD_ORDER = [h for pair in _SWA_PAIR_HEADS for h in pair]

_FF_CHUNKS = ((0, 2816),)
FFN_ROW_BLOCK = 256


def _log_gamma():
    h = np.arange(RET_HEADS, dtype=np.float64)
    return np.log1p(-np.exp2(-5.0 - h))


def _alibi_slopes():
    return np.exp2(-8.0 / SWA_HEADS * np.arange(1, SWA_HEADS + 1, dtype=np.float64))


def _layer_norm(z, g, b):
    mu = jnp.mean(z, axis=-1, keepdims=True)
    d = z - mu
    var = jnp.mean(d * d, axis=-1, keepdims=True)
    return d * lax.rsqrt(var + LN_EPS) * g + b


def _pair_group_norm(o, lo, gain):
    zero = jnp.zeros_like(o)
    s_lo = jnp.sum(jnp.where(lo, o, zero), axis=-1, keepdims=True)
    s_hi = jnp.sum(jnp.where(lo, zero, o), axis=-1, keepdims=True)
    d = o - jnp.where(lo, s_lo, s_hi) * (1.0 / HEAD_DIM)
    d2 = d * d
    v_lo = jnp.sum(jnp.where(lo, d2, zero), axis=-1, keepdims=True)
    v_hi = jnp.sum(jnp.where(lo, zero, d2), axis=-1, keepdims=True)
    var = jnp.where(lo, v_lo, v_hi) * (1.0 / HEAD_DIM)
    return d * lax.rsqrt(var + GN_EPS) * gain


def _dot(a, b):
    return jnp.dot(a, b, preferred_element_type=F32)


def _dot_nt(a, b):
    return lax.dot_general(a, b, (((1,), (1,)), ((), ())), preferred_element_type=F32)


def _dot_tn(a, b):
    return lax.dot_general(a, b, (((0,), (0,)), ((), ())), preferred_element_type=F32)


def _ffn_ln_body(*refs, row_block, ncast):
    x_ref, wg_ref, wu_ref, wd_ref, g_ref, b_ref = refs[:6]
    o_ref = refs[6 + ncast]
    side_job_at = row_block if x_ref.shape[0] > row_block else 0
    for r0 in range(0, x_ref.shape[0], row_block):
        if r0 == side_job_at:
            for src, dst in zip(refs[6:6 + ncast], refs[7 + ncast:]):
                dst[...] = src[...].astype(BF16)
        x = x_ref[r0:r0 + row_block, :]
        xb = x.astype(BF16)
        acc = None
        for c0, cw in _FF_CHUNKS:
            gate = _dot(xb, wg_ref[:, c0:c0 + cw])
            up = _dot(xb, wu_ref[:, c0:c0 + cw])
            hid = (gate * jax.nn.sigmoid(gate) * up).astype(BF16)
            part = _dot(hid, wd_ref[c0:c0 + cw, :])
            acc = part if acc is None else acc + part
        z = DN_ALPHA * x + 0.5 * acc
        o_ref[r0:r0 + row_block, :] = _layer_norm(z, g_ref[...], b_ref[...])


def _const_spec(shape):
    nd = len(shape)
    return pl.BlockSpec(shape, lambda *_: (0,) * nd, pipeline_mode=pl.Buffered(1))


def _ffn_ln(x, wg, wu, wd, g, b, *, tm, cast_next=()):
    rows = x.shape[0]
    assert rows % tm == 0 and tm % FFN_ROW_BLOCK == 0
    steps = rows // tm
    slab_specs = []
    for w in cast_next:
        nslabs = max(n for n in range(1, steps + 1)
                     if steps % n == 0 and w.shape[0] % n == 0 and (w.shape[0] // n) % 16 == 0)
        every = steps // nslabs
        slab_specs.append(pl.BlockSpec((w.shape[0] // nslabs, w.shape[1]),
                                       functools.partial(lambda i, e: (i // e, 0), e=every)))
    outs = pl.pallas_call(
        functools.partial(_ffn_ln_body, row_block=FFN_ROW_BLOCK, ncast=len(cast_next)),
        grid=(steps,),
        in_specs=[
            pl.BlockSpec((tm, D_MODEL), lambda i: (i, 0)),
            _const_spec((D_MODEL, D_FF)),
            _const_spec((D_MODEL, D_FF)),
            _const_spec((D_FF, D_MODEL)),
            _const_spec((1, D_MODEL)),
            _const_spec((1, D_MODEL)),
        ] + slab_specs,
        out_specs=[pl.BlockSpec((tm, D_MODEL), lambda i: (i, 0))] + slab_specs,
        out_shape=[jax.ShapeDtypeStruct((rows, D_MODEL), F32)]
        + [jax.ShapeDtypeStruct(w.shape, BF16) for w in cast_next],
        compiler_params=pltpu.CompilerParams(
            dimension_semantics=("arbitrary",), vmem_limit_bytes=VMEM_LIMIT_BYTES),
        name="ffn_ln",
    )(x, wg, wu, wd, g, b, *cast_next)
    if not cast_next:
        return outs[0]
    return outs[0], tuple(outs[1:])


def _prompt_constants():
    lg = _log_gamma()
    idx = np.arange(CHUNK, dtype=np.float64)
    diff = idx[:, None] - idx[None, :]
    dm = np.where(diff >= 0, np.exp(np.where(diff >= 0, diff, 0.0)[None] * lg[:, None, None]), 0.0)
    dm_st = np.stack([np.block([[dm[4 * g], dm[4 * g + 2]], [dm[4 * g + 1], dm[4 * g + 3]]])
                      for g in range(GROUPS)])
    qw = np.repeat(np.exp((idx + 1.0)[:, None] * lg[None, :]), HEAD_DIM, axis=1)
    kw = np.repeat(np.exp((CHUNK - 1.0 - idx)[:, None] * lg[None, :]), HEAD_DIM, axis=1) * QK_SCALE
    gc = np.exp(CHUNK * lg)
    blk = np.kron(np.eye(4), np.ones((HEAD_DIM, HEAD_DIM)))
    gbd = np.stack([blk * np.repeat(gc[4 * g:4 * g + 4], HEAD_DIM)[:, None] for g in range(GROUPS)])
    slopes = _alibi_slopes()
    i = np.arange(CHUNK)
    j = np.arange(2 * CHUNK)
    dist = i[:, None] + CHUNK - j[None, :]
    within = (dist >= 0) & (dist <= WINDOW)
    valid = [within & (j >= CHUNK)[None, :], within]
    bias = np.stack([np.concatenate([np.where(valid[v], -slopes[h] * dist * LOG2E, NEG_BIG).T
                                     for h in _SWA_HEAD_ORDER], axis=1) for v in range(2)])
    f = lambda a: jnp.asarray(a, F32)
    return f(dm_st), f(qw), f(kw), f(gbd), f(blk), f(bias)


def _mixer_prompt_body(x_ref, xn_ref, win_ref, wout_ref, lng_ref, lnb_ref, gnw_ref, sink_ref,
                       dm_ref, qw_ref, kw_ref, gbd_ref, blk_ref, bias_ref,
                       y_ref, st_ref, ck_ref, cv_ref,
                       pb_ref, proj1_ref, oret_ref, yret_ref, swat_ref, s_ref, kprev_ref, vprev_ref,
                       *, tq):
    step = pl.program_id(1)
    nsteps = pl.num_programs(1)
    flat = pl.program_id(0) * nsteps + step
    slot = lax.rem(flat, 2)
    nchunks = tq // CHUNK
    nblocks = tq // PROJ_ROWS
    col_starts = range(0, IN_COLS, PROJ_COLS)

    @pl.when(step == 0)
    def _():
        s_ref[...] = jnp.zeros_like(s_ref)
        kprev_ref[...] = jnp.zeros_like(kprev_ref)
        vprev_ref[...] = jnp.zeros_like(vprev_ref)

    x = x_ref[0]

    def project_piece(dst, xrows, c0):
        c1 = min(c0 + PROJ_COLS, IN_COLS)
        dst[:, c0:c1] = _dot(xrows.astype(BF16), win_ref[:, c0:c1])

    @pl.when(flat == 0)
    def _():
        for c0 in col_starts:
            project_piece(pb_ref.at[0], x[:PROJ_ROWS], c0)

    def proj(r0, nrows, c0, c1):
        if r0 < PROJ_ROWS:
            return pb_ref[slot, r0:r0 + nrows, c0:c1]
        return proj1_ref[r0 - PROJ_ROWS:r0 - PROJ_ROWS + nrows, c0:c1]

    pending = []

    def emit_pieces(n):
        for _ in range(min(n, len(pending))):
            pending.pop(0)()

    pieces_per_chunk = pl.cdiv(len(col_starts), PROJ_ROWS // CHUNK)

    lane = lax.broadcasted_iota(jnp.int32, (CHUNK, LANES), 1)
    lo = lane < HEAD_DIM
    lane_g = lax.broadcasted_iota(jnp.int32, (CHUNK, GROUP_W), 1)
    lo_g = jnp.bitwise_and(lane_g, LANES - 1) < HEAD_DIM
    blk = blk_ref[...]
    zero_b = jnp.zeros((CHUNK, LANES), BF16)

    def pair_diag(a):
        return jnp.concatenate([jnp.concatenate([a[:, :LANES], zero_b], axis=1),
                                jnp.concatenate([zero_b, a[:, LANES:]], axis=1)], axis=0)

    def head_stack(a):
        zero = jnp.zeros_like(a)
        return jnp.concatenate([jnp.where(lo_g, a, zero), jnp.where(lo_g, zero, a)], axis=0)

    lane_t = lax.broadcasted_iota(jnp.int32, (PROJ_ROWS, LANES), 1)
    lo_t = lane_t < HEAD_DIM

    def finish_block(kb):
        rb = slice(kb * PROJ_ROWS, (kb + 1) * PROJ_ROWS)
        for p in range(PAIRS):
            cols = slice(p * LANES, (p + 1) * LANES)
            gn = _pair_group_norm(oret_ref[rb, cols], lo_t, gnw_ref[:, cols])
            gate = proj(kb * PROJ_ROWS, PROJ_ROWS, _RG + p * LANES, _RG + (p + 1) * LANES)
            yret_ref[rb, cols] = (gate * jax.nn.sigmoid(gate) * gn).astype(BF16)
        mix = (_dot(yret_ref[rb, :], wout_ref[:RET_W, :])
               + _dot_tn(swat_ref[:, rb].astype(BF16), wout_ref[RET_W:, :]))
        y_ref[0, rb, :] = _layer_norm(DN_ALPHA * x[rb] + mix, lng_ref[...], lnb_ref[...])

    k_prev = kprev_ref[...]
    vt_prev = vprev_ref[...]
    for c in range(nchunks):
        r0 = c * CHUNK
        rows = slice(r0, r0 + CHUNK)
        if r0 % PROJ_ROWS == 0:
            nb = r0 // PROJ_ROWS + 1
            if nb < nblocks:
                dst = proj1_ref.at[(nb - 1) * PROJ_ROWS:nb * PROJ_ROWS]
                xrows = x[nb * PROJ_ROWS:(nb + 1) * PROJ_ROWS]
            else:
                dst, xrows = pb_ref.at[1 - slot], xn_ref[0]
            pending.extend(functools.partial(project_piece, dst, xrows, c0) for c0 in col_starts)
        budget = pieces_per_chunk
        for g in range(GROUPS):
            if budget > 0:
                emit_pieces(1)
                budget -= 1
            cols = slice(g * GROUP_W, (g + 1) * GROUP_W)
            q4 = proj(r0, CHUNK, _RQ + g * GROUP_W, _RQ + (g + 1) * GROUP_W)
            k4 = proj(r0, CHUNK, _RK + g * GROUP_W, _RK + (g + 1) * GROUP_W)
            v4 = proj(r0, CHUNK, _RV + g * GROUP_W, _RV + (g + 1) * GROUP_W).astype(BF16)
            k_bd = pair_diag((k4 * QK_SCALE).astype(BF16))
            sc = _dot_nt(head_stack(q4).astype(BF16), k_bd) * dm_ref[g]
            qd_st = head_stack(q4 * qw_ref[:, cols])
            lhs = jnp.concatenate([sc.astype(BF16), qd_st.astype(BF16)], axis=1)
            s_old = s_ref[g]
            rhs = jnp.concatenate([pair_diag(v4), s_old.astype(BF16)], axis=0)
            r = _dot(lhs, rhs)
            oret_ref[rows, cols] = jnp.where(lo_g, r[:CHUNK], r[CHUNK:])
            kd = (k4 * kw_ref[:, cols]).astype(BF16)
            s_ref[g] = gbd_ref[g] * s_old + blk * _dot_tn(kd, v4)
        k_cur = proj(r0, CHUNK, _SK, _SK + KV_W).astype(BF16)
        vt_cur = proj(r0, CHUNK, _SV, _SV + KV_W).T.astype(BF16)
        kk = jnp.concatenate([k_prev, k_cur], axis=0)
        vvt = jnp.concatenate([vt_prev, vt_cur], axis=1)
        q_parts = []
        for p in range(PAIRS):
            q2 = proj(r0, CHUNK, _SQ + p * LANES, _SQ + (p + 1) * LANES) * (QK_SCALE * LOG2E)
            zero = jnp.zeros_like(q2)
            q_parts += [jnp.where(lo, q2, zero).astype(BF16), jnp.where(lo, zero, q2).astype(BF16)]
        st = _dot_nt(kk, jnp.concatenate(q_parts, axis=0))
        if c == 0:
            st = st + bias_ref[jnp.minimum(step, 1)]
        else:
            st = st + bias_ref[1]
        sink = sink_ref[...] * LOG2E
        e_parts, den_parts = [], []
        for p in range(PAIRS):
            if budget > 0:
                emit_pieces(1)
                budget -= 1
            pc = slice(2 * p * CHUNK, 2 * (p + 1) * CHUNK)
            st_p, sink_p = st[:, pc], sink[:, pc]
            m = jnp.maximum(jnp.max(st_p, axis=0, keepdims=True), sink_p)
            e = jnp.exp2(st_p - m)
            den_parts.append(jnp.sum(e, axis=0, keepdims=True) + jnp.exp2(sink_p - m))
            e_parts.append(e.astype(BF16))
        if (r0 + CHUNK) % PROJ_ROWS == 0:
            emit_pieces(len(pending))
        den = jnp.concatenate(den_parts, axis=1)
        ot = _dot(vvt, jnp.concatenate(e_parts, axis=1)) * (1.0 / den)
        for p in range(PAIRS):
            c0 = 2 * p * CHUNK
            swat_ref[p * LANES:p * LANES + HEAD_DIM, rows] = ot[:HEAD_DIM, c0:c0 + CHUNK]
            swat_ref[p * LANES + HEAD_DIM:(p + 1) * LANES, rows] = ot[HEAD_DIM:, c0 + CHUNK:c0 + 2 * CHUNK]
        k_prev, vt_prev = k_cur, vt_cur
        if (r0 + CHUNK) % PROJ_ROWS == 0:
            finish_block(r0 // PROJ_ROWS)
    kprev_ref[...] = k_prev
    vprev_ref[...] = vt_prev

    @pl.when(step == nsteps - 1)
    def _():
        for h in range(RET_HEADS):
            g, i = divmod(h, 4)
            blk_h = s_ref[g, i * HEAD_DIM:(i + 1) * HEAD_DIM, (i // 2) * LANES:(i // 2 + 1) * LANES]
            if i % 2:
                blk_h = pltpu.roll(blk_h, HEAD_DIM, 1)
            st_ref[0, h] = blk_h[:, :HEAD_DIM]
        ck_ref[0] = proj(tq - WINDOW, WINDOW, _SK, _SK + KV_W).T
        cv_ref[0] = proj(tq - WINDOW, WINDOW, _SV, _SV + KV_W).T


def _mixer_prompt(x, w_in_p, w_out_p, lng, lnb, gnw, sink_st, consts, *, tq):
    bsz, seq, _ = x.shape
    assert seq % tq == 0 and tq % PROJ_ROWS == 0 and PROJ_ROWS % CHUNK == 0
    dm_st, qw, kw, gbd, blk, bias = consts
    body = functools.partial(_mixer_prompt_body, tq=tq)
    nsteps = seq // tq
    blocks_per_tile = tq // PROJ_ROWS

    def next_tile_head(b, s):
        nxt = jnp.minimum(b * nsteps + s + 1, bsz * nsteps - 1)
        return nxt // nsteps, (nxt % nsteps) * blocks_per_tile, 0

    return pl.pallas_call(
        body,
        grid=(bsz, nsteps),
        in_specs=[
            pl.BlockSpec((1, tq, D_MODEL), lambda b, s: (b, s, 0)),
            pl.BlockSpec((1, PROJ_ROWS, D_MODEL), next_tile_head),
            _const_spec((D_MODEL, IN_COLS)),
            _const_spec((2 * RET_W, D_MODEL)),
            _const_spec((1, D_MODEL)),
            _const_spec((1, D_MODEL)),
            _const_spec((1, RET_W)),
            _const_spec((1, SWA_HEADS * CHUNK)),
            _const_spec(dm_st.shape),
            _const_spec(qw.shape),
            _const_spec(kw.shape),
            _const_spec(gbd.shape),
            _const_spec(blk.shape),
            _const_spec(bias.shape),
        ],
        out_specs=[
            pl.BlockSpec((1, tq, D_MODEL), lambda b, s: (b, s, 0)),
            pl.BlockSpec((1, RET_HEADS, HEAD_DIM, HEAD_DIM), lambda b, s: (b, 0, 0, 0)),
            pl.BlockSpec((1, KV_W, WINDOW), lambda b, s: (b, 0, 0)),
            pl.BlockSpec((1, KV_W, WINDOW), lambda b, s: (b, 0, 0)),
        ],
        out_shape=[
            jax.ShapeDtypeStruct((bsz, seq, D_MODEL), F32),
            jax.ShapeDtypeStruct((bsz, RET_HEADS, HEAD_DIM, HEAD_DIM), F32),
            jax.ShapeDtypeStruct((bsz, KV_W, WINDOW), F32),
            jax.ShapeDtypeStruct((bsz, KV_W, WINDOW), F32),
        ],
        scratch_shapes=[
            pltpu.VMEM((2, PROJ_ROWS, IN_COLS), F32),
            pltpu.VMEM((tq - PROJ_ROWS, IN_COLS), F32),
            pltpu.VMEM((tq, RET_W), F32),
            pltpu.VMEM((tq, RET_W), BF16),
            pltpu.VMEM((SWA_W, tq), F32),
            pltpu.VMEM((GROUPS, GROUP_W, GROUP_W), F32),
            pltpu.VMEM((CHUNK, KV_W), BF16),
            pltpu.VMEM((KV_W, CHUNK), BF16),
        ],
        compiler_params=pltpu.CompilerParams(
            dimension_semantics=("arbitrary", "arbitrary"), vmem_limit_bytes=VMEM_LIMIT_BYTES),
        name="mixer_prompt",
    )(x, x, w_in_p, w_out_p, lng, lnb, gnw, sink_st, dm_st, qw, kw, gbd, blk, bias)


def _sample_constants(t):
    lg = _log_gamma()
    hh = np.arange(RET_HEADS)
    tt = np.arange(t, dtype=np.float64)
    row_h = np.repeat(hh, t)
    row_t = np.tile(tt, RET_HEADS)
    col_h = np.repeat(hh, HEAD_DIM)
    same = (row_h[:, None] == col_h[None, :]).astype(np.float64)
    mask_q1 = same * QK_SCALE
    mask_qw = same * np.exp((row_t + 1.0) * lg[row_h])[:, None]
    mask_kw = same * (np.exp((t - 1.0 - row_t) * lg[row_h]) * QK_SCALE)[:, None]
    diff = row_t[:, None] - row_t[None, :]
    same_h = row_h[:, None] == row_h[None, :]
    dm = np.where(same_h & (diff >= 0), np.exp(np.where(diff >= 0, diff, 0.0) * lg[row_h][:, None]), 0.0)
    g_t = np.repeat(np.exp(t * lg), HEAD_DIM)[:, None] * np.ones((1, HEAD_DIM))
    slopes = _alibi_slopes()
    j = np.arange(WINDOW + t)
    dist = row_t[:, None] + WINDOW - j[None, :]
    valid = (dist >= 0) & (dist <= WINDOW)
    bias = np.where(valid, -slopes[row_h][:, None] * dist, NEG_BIG)
    f = lambda a: jnp.asarray(a, F32)
    return f(mask_q1), f(mask_qw), f(mask_kw), f(dm), f(g_t), f(bias)


def _mixer_sample_body(x_ref, st_ref, ck_ref, cv_ref, win_ref, wout_ref,
                       lng_ref, lnb_ref, gnw_ref, sink_ref,
                       mq1_ref, mqw_ref, mkw_ref, dm_ref, gt_ref, bias_ref,
                       y_ref, sto_ref, cko_ref, cvo_ref,
                       proj_ref, vh_ref, qs_ref, oh_ref, os_ref, merged_ref, *, sb, t):
    ht = RET_HEADS * t
    nrows = sb * t
    x = x_ref[...]
    proj_ref[...] = _dot(x.astype(BF16), win_ref[...])
    lane = lax.broadcasted_iota(jnp.int32, (nrows, LANES), 1)
    lo = lane < HEAD_DIM
    for p in range(PAIRS):
        v2 = proj_ref[:, _RV + p * LANES:_RV + (p + 1) * LANES]
        vh_ref[2 * p] = v2[:, :HEAD_DIM]
        vh_ref[2 * p + 1] = pltpu.roll(v2, HEAD_DIM, 1)[:, :HEAD_DIM]
        q2 = proj_ref[:, _SQ + p * LANES:_SQ + (p + 1) * LANES] * QK_SCALE
        zero = jnp.zeros_like(q2)
        qs_ref[p] = jnp.where(lo, q2, zero)
        qs_ref[SWA_GROUP + p] = jnp.where(lo, zero, q2)
    oh_ref[...] = jnp.zeros_like(oh_ref)
    knt = proj_ref[:, _SK:_SK + KV_W].T
    vnt = proj_ref[:, _SV:_SV + KV_W].T
    keep = lax.broadcasted_iota(jnp.int32, (KV_W, WINDOW), 1) < WINDOW - t

    mq1 = mq1_ref[...]
    mqw = mqw_ref[...]
    mkw = mkw_ref[...]
    dm = dm_ref[...]
    g_t = gt_ref[...]
    bias = bias_ref[...]
    sink = sink_ref[...]

    def per_seq(b, carry):
        r0 = pl.multiple_of(b * t, t)
        rows = pl.ds(r0, t)
        q_b = proj_ref[rows, _RQ:_RQ + RET_W]
        k_b = proj_ref[rows, _RK:_RK + RET_W]
        q_rep = jnp.concatenate([q_b] * RET_HEADS, axis=0)
        k_rep = jnp.concatenate([k_b] * RET_HEADS, axis=0)
        v_st = vh_ref[:, rows, :].reshape(ht, HEAD_DIM).astype(BF16)
        s0 = st_ref[b]
        sc = _dot_nt((q_rep * mq1).astype(BF16), k_rep.astype(BF16)) * dm
        o = _dot((q_rep * mqw).astype(BF16), s0.astype(BF16)) + _dot(sc.astype(BF16), v_st)
        oh_ref[:, rows, 0:HEAD_DIM] = o.reshape(RET_HEADS, t, HEAD_DIM)
        sto_ref[b] = g_t * s0 + _dot_tn((k_rep * mkw).astype(BF16), v_st)
        k_new = proj_ref[rows, _SK:_SK + KV_W]
        v_new = proj_ref[rows, _SV:_SV + KV_W]
        kt_old = ck_ref[b]
        vt_old = cv_ref[b]
        q_s = qs_ref[:, rows, :].reshape(ht, LANES).astype(BF16)
        s = jnp.concatenate([_dot(q_s, kt_old.astype(BF16)),
                             _dot_nt(q_s, k_new.astype(BF16))], axis=1) + bias
        m = jnp.maximum(jnp.max(s, axis=-1, keepdims=True), sink)
        e = jnp.exp(s - m)
        den = jnp.sum(e, axis=-1, keepdims=True) + jnp.exp(sink - m)
        eb = e.astype(BF16)
        o_s = (_dot_nt(eb[:, :WINDOW], vt_old.astype(BF16))
               + _dot(eb[:, WINDOW:], v_new.astype(BF16))) / den
        os_ref[:, rows, :] = o_s.reshape(SWA_HEADS, t, LANES)
        shift = WINDOW - t - r0
        cko_ref[b] = jnp.where(keep, pltpu.roll(kt_old, WINDOW - t, 1), pltpu.roll(knt, shift, 1))
        cvo_ref[b] = jnp.where(keep, pltpu.roll(vt_old, WINDOW - t, 1), pltpu.roll(vnt, shift, 1))
        return carry

    lax.fori_loop(0, sb, per_seq, 0, unroll=16)

    for p in range(PAIRS):
        cols = slice(p * LANES, (p + 1) * LANES)
        o2 = oh_ref[2 * p] + pltpu.roll(oh_ref[2 * p + 1], HEAD_DIM, 1)
        gn = _pair_group_norm(o2, lo, gnw_ref[:, cols])
        gate = proj_ref[:, _RG + p * LANES:_RG + (p + 1) * LANES]
        merged_ref[:, cols] = (gate * jax.nn.sigmoid(gate) * gn).astype(BF16)
        merged_ref[:, RET_W + p * LANES:RET_W + (p + 1) * LANES] = jnp.where(
            lo, os_ref[p], os_ref[SWA_GROUP + p]).astype(BF16)
    mix = _dot(merged_ref[...], wout_ref[...])
    y_ref[...] = _layer_norm(DN_ALPHA * x + mix, lng_ref[...], lnb_ref[...])


def _mixer_sample(x, state, ck, cv, w_in_p, w_out_p, lng, lnb, gnw, sink_rows, consts, *, sb, t):
    rows_total = x.shape[0]
    nseq = rows_total // t
    assert nseq % sb == 0 and sb * t == LANES and t % 8 == 0
    rows = sb * t
    weights = (w_in_p, w_out_p)
    body = functools.partial(_mixer_sample_body, sb=sb, t=t)
    return pl.pallas_call(
        body,
        grid=(nseq // sb,),
        in_specs=[
            pl.BlockSpec((rows, D_MODEL), lambda i: (i, 0)),
            pl.BlockSpec((sb, RET_W, HEAD_DIM), lambda i: (i, 0, 0)),
            pl.BlockSpec((sb, KV_W, WINDOW), lambda i: (i, 0, 0)),
            pl.BlockSpec((sb, KV_W, WINDOW), lambda i: (i, 0, 0)),
        ] + [_const_spec(w.shape) for w in weights] + [
            _const_spec(lng.shape), _const_spec(lnb.shape), _const_spec(gnw.shape),
            _const_spec(sink_rows.shape),
        ] + [_const_spec(c.shape) for c in consts],
        out_specs=[
            pl.BlockSpec((rows, D_MODEL), lambda i: (i, 0)),
            pl.BlockSpec((sb, RET_W, HEAD_DIM), lambda i: (i, 0, 0)),
            pl.BlockSpec((sb, KV_W, WINDOW), lambda i: (i, 0, 0)),
            pl.BlockSpec((sb, KV_W, WINDOW), lambda i: (i, 0, 0)),
        ],
        out_shape=[
            jax.ShapeDtypeStruct((rows_total, D_MODEL), F32),
            jax.ShapeDtypeStruct((nseq, RET_W, HEAD_DIM), F32),
            jax.ShapeDtypeStruct((nseq, KV_W, WINDOW), F32),
            jax.ShapeDtypeStruct((nseq, KV_W, WINDOW), F32),
        ],
        scratch_shapes=[
            pltpu.VMEM((rows, IN_COLS), F32),
            pltpu.VMEM((RET_HEADS, rows, HEAD_DIM), F32),
            pltpu.VMEM((SWA_HEADS, rows, LANES), F32),
            pltpu.VMEM((RET_HEADS, rows, LANES), F32),
            pltpu.VMEM((SWA_HEADS, rows, LANES), F32),
            pltpu.VMEM((rows, 2 * RET_W), BF16),
        ],
        compiler_params=pltpu.CompilerParams(
            dimension_semantics=("arbitrary",), vmem_limit_bytes=VMEM_LIMIT_BYTES),
        name="mixer_sample",
    )(x, state, ck, cv, *weights, lng, lnb, gnw, sink_rows, *consts)


def _prep_prompt_weights(w_in, w_out, sinks):
    sq = w_in[:, _SQ:_SQ + SWA_W].reshape(D_MODEL, SWA_HEADS, HEAD_DIM)
    sq = sq[:, jnp.asarray(_SWA_HEAD_ORDER), :].reshape(D_MODEL, SWA_W)
    w_in_p = jnp.concatenate([w_in[:, :_SQ], sq, w_in[:, _SK:]], axis=1).astype(BF16)
    wo_s = w_out[RET_W:].reshape(SWA_HEADS, HEAD_DIM, D_MODEL)
    wo_s = wo_s[jnp.asarray(_SWA_HEAD_ORDER)].reshape(SWA_W, D_MODEL)
    w_out_p = jnp.concatenate([w_out[:RET_W], wo_s], axis=0).astype(BF16)
    sink_st = jnp.repeat(sinks.astype(F32)[jnp.asarray(_SWA_HEAD_ORDER)], CHUNK)[None, :]
    return w_in_p, w_out_p, sink_st


def kernel(x_prompt, x_sample, state_ret, cache_swa_k, cache_swa_v, ln_gain, ln_bias, w_in,
           ret_gn_w, swa_sinks, w_out, ffn1_gate, ffn1_up, ffn1_down, ffn2_gate, ffn2_up, ffn2_down):
    assert ln_gain.shape[0] == DEPTH == 1
    bsz, seq, _ = x_prompt.shape
    nseq, t, _ = x_sample.shape
    lng = ln_gain[0].astype(F32).reshape(3, 1, D_MODEL)
    lnb = ln_bias[0].astype(F32).reshape(3, 1, D_MODEL)
    f1 = (ffn1_gate[0].astype(BF16), ffn1_up[0].astype(BF16), ffn1_down[0].astype(BF16))
    gnw = ret_gn_w[0].astype(F32)

    to_dims_major = lambda c: jnp.transpose(c, (0, 2, 3, 1)).reshape(c.shape[0], KV_W, WINDOW)
    to_pos_major = lambda c: jnp.transpose(
        c.reshape(c.shape[0], SWA_KV_HEADS, HEAD_DIM, WINDOW), (0, 3, 1, 2))[None]

    tm_p = min(1024, bsz * seq)
    tm_s = min(512, nseq * t)
    xp = x_prompt.reshape(bsz * seq, D_MODEL)
    xs = x_sample.reshape(nseq * t, D_MODEL)

    xp, f2 = _ffn_ln(xp, *f1, lng[0], lnb[0], tm=tm_p,
                     cast_next=(ffn2_gate[0], ffn2_up[0], ffn2_down[0]))
    w_in_p, w_out_p, sink_st = _prep_prompt_weights(w_in[0], w_out[0], swa_sinks[0])
    yp, st_p, ck_p, cv_p = _mixer_prompt(
        xp.reshape(bsz, seq, D_MODEL), w_in_p, w_out_p, lng[1], lnb[1], gnw.reshape(1, RET_W),
        sink_st, _prompt_constants(), tq=min(1024, seq))
    yp = _ffn_ln(yp.reshape(bsz * seq, D_MODEL), *f2, lng[2], lnb[2], tm=tm_p)

    xs = _ffn_ln(xs, *f1, lng[0], lnb[0], tm=tm_s)
    sink_rows = jnp.repeat(swa_sinks[0].astype(F32), t)[:, None]
    ys, st_s, ck_s, cv_s = _mixer_sample(
        xs, state_ret[0].astype(F32).reshape(nseq, RET_W, HEAD_DIM),
        to_dims_major(cache_swa_k[0]), to_dims_major(cache_swa_v[0]),
        w_in_p, w_out_p, lng[1], lnb[1], gnw.reshape(1, RET_W), sink_rows, _sample_constants(t),
        sb=LANES // t, t=t)
    ys = _ffn_ln(ys, *f2, lng[2], lnb[2], tm=tm_s)

    return (yp.reshape(bsz, seq, D_MODEL), ys.reshape(nseq, t, D_MODEL),
            st_p[None], to_pos_major(ck_p), to_pos_major(cv_p),
            st_s.reshape(1, nseq, RET_HEADS, HEAD_DIM, HEAD_DIM),
            to_pos_major(ck_s), to_pos_major(cv_s))
```

```python
import functools

import numpy as np
import jax
import jax.numpy as jnp
from jax import lax
from jax.experimental import pallas as pl
from jax.experimental.pallas import tpu as pltpu

F32 = jnp.float32
BF16 = jnp.bfloat16

D_MODEL = 1024
HEAD_DIM = 64
RET_HEADS = 8
SWA_HEADS = 8
SWA_KV_HEADS = 2
SWA_GROUP = SWA_HEADS // SWA_KV_HEADS
WINDOW = 128
CHUNK = 128
D_FF = 2816
RET_W = RET_HEADS * HEAD_DIM
SWA_W = SWA_HEADS * HEAD_DIM
KV_W = SWA_KV_HEADS * HEAD_DIM
IN_COLS = 4 * RET_W + SWA_W + 2 * KV_W
LN_EPS = 1e-5
GN_EPS = 1e-5
DEPTH = 1
DN_ALPHA = (2.0 * DEPTH) ** 0.25
QK_SCALE = HEAD_DIM ** -0.5
NEG_BIG = -1e30
LOG2E = 1.4426950408889634

LANES = 128
PAIRS = RET_HEADS // 2
GROUPS = RET_HEADS // 4
GROUP_W = 4 * HEAD_DIM
PROJ_ROWS = 256
PROJ_COLS = 256
VMEM_LIMIT_BYTES = 56 * 1024 * 1024

_RQ, _RK, _RV, _RG, _SQ = 0, RET_W, 2 * RET_W, 3 * RET_W, 4 * RET_W
_SK = 4 * RET_W + SWA_W
_SV = _SK + KV_W

_SWA_PAIR_HEADS = [(p, SWA_GROUP + p) for p in range(PAIRS)]
_SWA_HEAD_ORDER = [h for pair in _SWA_PAIR_HEADS for h in pair]

_FF_CHUNKS = ((0, 2816),)
FFN_ROW_BLOCK = 256


def _log_gamma():
    h = np.arange(RET_HEADS, dtype=np.float64)
    return np.log1p(-np.exp2(-5.0 - h))


def _alibi_slopes():
    return np.exp2(-8.0 / SWA_HEADS * np.arange(1, SWA_HEADS + 1, dtype=np.float64))


def _layer_norm(z, g, b):
    mu = jnp.mean(z, axis=-1, keepdims=True)
    d = z - mu
    var = jnp.mean(d * d, axis=-1, keepdims=True)
    return d * lax.rsqrt(var + LN_EPS) * g + b


def _pair_group_norm(o, lo, gain):
    zero = jnp.zeros_like(o)
    s_lo = jnp.sum(jnp.where(lo, o, zero), axis=-1, keepdims=True)
    s_hi = jnp.sum(jnp.where(lo, zero, o), axis=-1, keepdims=True)
    d = o - jnp.where(lo, s_lo, s_hi) * (1.0 / HEAD_DIM)
    d2 = d * d
    v_lo = jnp.sum(jnp.where(lo, d2, zero), axis=-1, keepdims=True)
    v_hi = jnp.sum(jnp.where(lo, zero, d2), axis=-1, keepdims=True)
    var = jnp.where(lo, v_lo, v_hi) * (1.0 / HEAD_DIM)
    return d * lax.rsqrt(var + GN_EPS) * gain


def _dot(a, b):
    return jnp.dot(a, b, preferred_element_type=F32)


def _dot_nt(a, b):
    return lax.dot_general(a, b, (((1,), (1,)), ((), ())), preferred_element_type=F32)


def _dot_tn(a, b):
    return lax.dot_general(a, b, (((0,), (0,)), ((), ())), preferred_element_type=F32)


def _ffn_ln_body(*refs, row_block, ncast):
    x_ref, wg_ref, wu_ref, wd_ref, g_ref, b_ref = refs[:6]
    o_ref = refs[6 + ncast]
    side_job_at = row_block if x_ref.shape[0] > row_block else 0
    for r0 in range(0, x_ref.shape[0], row_block):
        if r0 == side_job_at:
            for src, dst in zip(refs[6:6 + ncast], refs[7 + ncast:]):
                dst[...] = src[...].astype(BF16)
        x = x_ref[r0:r0 + row_block, :]
        xb = x.astype(BF16)
        acc = None
        for c0, cw in _FF_CHUNKS:
            gate = _dot(xb, wg_ref[:, c0:c0 + cw])
            up = _dot(xb, wu_ref[:, c0:c0 + cw])
            hid = (gate * jax.nn.sigmoid(gate) * up).astype(BF16)
            part = _dot(hid, wd_ref[c0:c0 + cw, :])
            acc = part if acc is None else acc + part
        z = DN_ALPHA * x + 0.5 * acc
        o_ref[r0:r0 + row_block, :] = _layer_norm(z, g_ref[...], b_ref[...])


def _const_spec(shape):
    nd = len(shape)
    return pl.BlockSpec(shape, lambda *_: (0,) * nd, pipeline_mode=pl.Buffered(1))


def _ffn_ln(x, wg, wu, wd, g, b, *, tm, cast_next=()):
    rows = x.shape[0]
    assert rows % tm == 0 and tm % FFN_ROW_BLOCK == 0
    steps = rows // tm
    slab_specs = []
    for w in cast_next:
        nslabs = max(n for n in range(1, steps + 1)
                     if steps % n == 0 and w.shape[0] % n == 0 and (w.shape[0] // n) % 16 == 0)
        every = steps // nslabs
        slab_specs.append(pl.BlockSpec((w.shape[0] // nslabs, w.shape[1]),
                                       functools.partial(lambda i, e: (i // e, 0), e=every)))
    outs = pl.pallas_call(
        functools.partial(_ffn_ln_body, row_block=FFN_ROW_BLOCK, ncast=len(cast_next)),
        grid=(steps,),
        in_specs=[
            pl.BlockSpec((tm, D_MODEL), lambda i: (i, 0)),
            _const_spec((D_MODEL, D_FF)),
            _const_spec((D_MODEL, D_FF)),
            _const_spec((D_FF, D_MODEL)),
            _const_spec((1, D_MODEL)),
            _const_spec((1, D_MODEL)),
        ] + slab_specs,
        out_specs=[pl.BlockSpec((tm, D_MODEL), lambda i: (i, 0))] + slab_specs,
        out_shape=[jax.ShapeDtypeStruct((rows, D_MODEL), F32)]
        + [jax.ShapeDtypeStruct(w.shape, BF16) for w in cast_next],
        compiler_params=pltpu.CompilerParams(
            dimension_semantics=("arbitrary",), vmem_limit_bytes=VMEM_LIMIT_BYTES),
        name="ffn_ln",
    )(x, wg, wu, wd, g, b, *cast_next)
    if not cast_next:
        return outs[0]
    return outs[0], tuple(outs[1:])


def _prompt_constants():
    lg = _log_gamma()
    idx = np.arange(CHUNK, dtype=np.float64)
    diff = idx[:, None] - idx[None, :]
    dm = np.where(diff >= 0, np.exp(np.where(diff >= 0, diff, 0.0)[None] * lg[:, None, None]), 0.0)
    dm_st = np.stack([np.block([[dm[4 * g], dm[4 * g + 2]], [dm[4 * g + 1], dm[4 * g + 3]]])
                      for g in range(GROUPS)])
    qw = np.repeat(np.exp((idx + 1.0)[:, None] * lg[None, :]), HEAD_DIM, axis=1)
    kw = np.repeat(np.exp((CHUNK - 1.0 - idx)[:, None] * lg[None, :]), HEAD_DIM, axis=1) * QK_SCALE
    gc = np.exp(CHUNK * lg)
    blk = np.kron(np.eye(4), np.ones((HEAD_DIM, HEAD_DIM)))
    gbd = np.stack([blk * np.repeat(gc[4 * g:4 * g + 4], HEAD_DIM)[:, None] for g in range(GROUPS)])
    slopes = _alibi_slopes()
    i = np.arange(CHUNK)
    j = np.arange(2 * CHUNK)
    dist = i[:, None] + CHUNK - j[None, :]
    within = (dist >= 0) & (dist <= WINDOW)
    valid = [within & (j >= CHUNK)[None, :], within]
    bias = np.stack([np.concatenate([np.where(valid[v], -slopes[h] * dist * LOG2E, NEG_BIG).T
                                     for h in _SWA_HEAD_ORDER], axis=1) for v in range(2)])
    f = lambda a: jnp.asarray(a, F32)
    return f(dm_st), f(qw), f(kw), f(gbd), f(blk), f(bias)


def _mixer_prompt_body(x_ref, xn_ref, win_ref, wout_ref, lng_ref, lnb_ref, gnw_ref, sink_ref,
                       dm_ref, qw_ref, kw_ref, gbd_ref, blk_ref, bias_ref,
                       y_ref, st_ref, ck_ref, cv_ref,
                       pb_ref, proj1_ref, oret_ref, yret_ref, swat_ref, s_ref, kprev_ref, vprev_ref,
                       *, tq):
    step = pl.program_id(1)
    nsteps = pl.num_programs(1)
    flat = pl.program_id(0) * nsteps + step
    slot = lax.rem(flat, 2)
    nchunks = tq // CHUNK
    nblocks = tq // PROJ_ROWS
    col_starts = range(0, IN_COLS, PROJ_COLS)

    @pl.when(step == 0)
    def _():
        s_ref[...] = jnp.zeros_like(s_ref)
        kprev_ref[...] = jnp.zeros_like(kprev_ref)
        vprev_ref[...] = jnp.zeros_like(vprev_ref)

    x = x_ref[0]

    def project_piece(dst, xrows, c0):
        c1 = min(c0 + PROJ_COLS, IN_COLS)
        dst[:, c0:c1] = _dot(xrows.astype(BF16), win_ref[:, c0:c1])

    @pl.when(flat == 0)
    def _():
        for c0 in col_starts:
            project_piece(pb_ref.at[0], x[:PROJ_ROWS], c0)

    def proj(r0, nrows, c0, c1):
        if r0 < PROJ_ROWS:
            return pb_ref[slot, r0:r0 + nrows, c0:c1]
        return proj1_ref[r0 - PROJ_ROWS:r0 - PROJ_ROWS + nrows, c0:c1]

    pending = []

    def emit_pieces(n):
        for _ in range(min(n, len(pending))):
            pending.pop(0)()

    pieces_per_chunk = pl.cdiv(len(col_starts), PROJ_ROWS // CHUNK)

    lane = lax.broadcasted_iota(jnp.int32, (CHUNK, LANES), 1)
    lo = lane < HEAD_DIM
    lane_g = lax.broadcasted_iota(jnp.int32, (CHUNK, GROUP_W), 1)
    lo_g = jnp.bitwise_and(lane_g, LANES - 1) < HEAD_DIM
    blk = blk_ref[...]
    zero_b = jnp.zeros((CHUNK, LANES), BF16)

    def pair_diag(a):
        return jnp.concatenate([jnp.concatenate([a[:, :LANES], zero_b], axis=1),
                                jnp.concatenate([zero_b, a[:, LANES:]], axis=1)], axis=0)

    def head_stack(a):
        zero = jnp.zeros_like(a)
        return jnp.concatenate([jnp.where(lo_g, a, zero), jnp.where(lo_g, zero, a)], axis=0)

    lane_t = lax.broadcasted_iota(jnp.int32, (PROJ_ROWS, LANES), 1)
    lo_t = lane_t < HEAD_DIM

    def finish_block(kb):
        rb = slice(kb * PROJ_ROWS, (kb + 1) * PROJ_ROWS)
        for p in range(PAIRS):
            cols = slice(p * LANES, (p + 1) * LANES)
            gn = _pair_group_norm(oret_ref[rb, cols], lo_t, gnw_ref[:, cols])
            gate = proj(kb * PROJ_ROWS, PROJ_ROWS, _RG + p * LANES, _RG + (p + 1) * LANES)
            yret_ref[rb, cols] = (gate * jax.nn.sigmoid(gate) * gn).astype(BF16)
        mix = (_dot(yret_ref[rb, :], wout_ref[:RET_W, :])
               + _dot_tn(swat_ref[:, rb].astype(BF16), wout_ref[RET_W:, :]))
        y_ref[0, rb, :] = _layer_norm(DN_ALPHA * x[rb] + mix, lng_ref[...], lnb_ref[...])

    k_prev = kprev_ref[...]
    vt_prev = vprev_ref[...]
    for c in range(nchunks):
        r0 = c * CHUNK
        rows = slice(r0, r0 + CHUNK)
        if r0 % PROJ_ROWS == 0:
            nb = r0 // PROJ_ROWS + 1
            if nb < nblocks:
                dst = proj1_ref.at[(nb - 1) * PROJ_ROWS:nb * PROJ_ROWS]
                xrows = x[nb * PROJ_ROWS:(nb + 1) * PROJ_ROWS]
            else:
                dst, xrows = pb_ref.at[1 - slot], xn_ref[0]
            pending.extend(functools.partial(project_piece, dst, xrows, c0) for c0 in col_starts)
        budget = pieces_per_chunk
        for g in range(GROUPS):
            if budget > 0:
                emit_pieces(1)
                budget -= 1
            cols = slice(g * GROUP_W, (g + 1) * GROUP_W)
            q4 = proj(r0, CHUNK, _RQ + g * GROUP_W, _RQ + (g + 1) * GROUP_W)
            k4 = proj(r0, CHUNK, _RK + g * GROUP_W, _RK + (g + 1) * GROUP_W)
            v4 = proj(r0, CHUNK, _RV + g * GROUP_W, _RV + (g + 1) * GROUP_W).astype(BF16)
            k_bd = pair_diag((k4 * QK_SCALE).astype(BF16))
            sc = _dot_nt(head_stack(q4).astype(BF16), k_bd) * dm_ref[g]
            qd_st = head_stack(q4 * qw_ref[:, cols])
            lhs = jnp.concatenate([sc.astype(BF16), qd_st.astype(BF16)], axis=1)
            s_old = s_ref[g]
            rhs = jnp.concatenate([pair_diag(v4), s_old.astype(BF16)], axis=0)
            r = _dot(lhs, rhs)
            oret_ref[rows, cols] = jnp.where(lo_g, r[:CHUNK], r[CHUNK:])
            kd = (k4 * kw_ref[:, cols]).astype(BF16)
            s_ref[g] = gbd_ref[g] * s_old + blk * _dot_tn(kd, v4)
        k_cur = proj(r0, CHUNK, _SK, _SK + KV_W).astype(BF16)
        vt_cur = proj(r0, CHUNK, _SV, _SV + KV_W).T.astype(BF16)
        kk = jnp.concatenate([k_prev, k_cur], axis=0)
        vvt = jnp.concatenate([vt_prev, vt_cur], axis=1)
        q_parts = []
        for p in range(PAIRS):
            q2 = proj(r0, CHUNK, _SQ + p * LANES, _SQ + (p + 1) * LANES) * (QK_SCALE * LOG2E)
            zero = jnp.zeros_like(q2)
            q_parts += [jnp.where(lo, q2, zero).astype(BF16), jnp.where(lo, zero, q2).astype(BF16)]
        st = _dot_nt(kk, jnp.concatenate(q_parts, axis=0))
        if c == 0:
            st = st + bias_ref[jnp.minimum(step, 1)]
        else:
            st = st + bias_ref[1]
        sink = sink_ref[...] * LOG2E
        e_parts, den_parts = [], []
        for p in range(SWA_HEADS):
            if budget > 0 and p % 2 == 0:
                emit_pieces(1)
                budget -= 1
            pc = slice(p * CHUNK, (p + 1) * CHUNK)
            st_p, sink_p = st[:, pc], sink[:, pc]
            m = jnp.maximum(jnp.max(st_p, axis=0, keepdims=True), sink_p)
            e = jnp.exp2(st_p - m)
            den_parts.append(jnp.sum(e, axis=0, keepdims=True) + jnp.exp2(sink_p - m))
            e_parts.append(e.astype(BF16))
        if (r0 + CHUNK) % PROJ_ROWS == 0:
            emit_pieces(len(pending))
        den = jnp.concatenate(den_parts, axis=1)
        ot = _dot(vvt, jnp.concatenate(e_parts, axis=1)) * (1.0 / den)
        for p in range(PAIRS):
            c0 = 2 * p * CHUNK
            swat_ref[p * LANES:p * LANES + HEAD_DIM, rows] = ot[:HEAD_DIM, c0:c0 + CHUNK]
            swat_ref[p * LANES + HEAD_DIM:(p + 1) * LANES, rows] = ot[HEAD_DIM:, c0 + CHUNK:c0 + 2 * CHUNK]
        k_prev, vt_prev = k_cur, vt_cur
        if (r0 + CHUNK) % PROJ_ROWS == 0:
            finish_block(r0 // PROJ_ROWS)
    kprev_ref[...] = k_prev
    vprev_ref[...] = vt_prev

    @pl.when(step == nsteps - 1)
    def _():
        for h in range(RET_HEADS):
            g, i = divmod(h, 4)
            blk_h = s_ref[g, i * HEAD_DIM:(i + 1) * HEAD_DIM, (i // 2) * LANES:(i // 2 + 1) * LANES]
            if i % 2:
                blk_h = pltpu.roll(blk_h, HEAD_DIM, 1)
            st_ref[0, h] = blk_h[:, :HEAD_DIM]
        ck_ref[0] = proj(tq - WINDOW, WINDOW, _SK, _SK + KV_W).T
        cv_ref[0] = proj(tq - WINDOW, WINDOW, _SV, _SV + KV_W).T


def _mixer_prompt(x, w_in_p, w_out_p, lng, lnb, gnw, sink_st, consts, *, tq):
    bsz, seq, _ = x.shape
    assert seq % tq == 0 and tq % PROJ_ROWS == 0 and PROJ_ROWS % CHUNK == 0
    dm_st, qw, kw, gbd, blk, bias = consts
    body = functools.partial(_mixer_prompt_body, tq=tq)
    nsteps = seq // tq
    blocks_per_tile = tq // PROJ_ROWS

    def next_tile_head(b, s):
        nxt = jnp.minimum(b * nsteps + s + 1, bsz * nsteps - 1)
        return nxt // nsteps, (nxt % nsteps) * blocks_per_tile, 0

    return pl.pallas_call(
        body,
        grid=(bsz, nsteps),
        in_specs=[
            pl.BlockSpec((1, tq, D_MODEL), lambda b, s: (b, s, 0)),
            pl.BlockSpec((1, PROJ_ROWS, D_MODEL), next_tile_head),
            _const_spec((D_MODEL, IN_COLS)),
            _const_spec((2 * RET_W, D_MODEL)),
            _const_spec((1, D_MODEL)),
            _const_spec((1, D_MODEL)),
            _const_spec((1, RET_W)),
            _const_spec((1, SWA_HEADS * CHUNK)),
            _const_spec(dm_st.shape),
            _const_spec(qw.shape),
            _const_spec(kw.shape),
            _const_spec(gbd.shape),
            _const_spec(blk.shape),
            _const_spec(bias.shape),
        ],
        out_specs=[
            pl.BlockSpec((1, tq, D_MODEL), lambda b, s: (b, s, 0)),
            pl.BlockSpec((1, RET_HEADS, HEAD_DIM, HEAD_DIM), lambda b, s: (b, 0, 0, 0)),
            pl.BlockSpec((1, KV_W, WINDOW), lambda b, s: (b, 0, 0)),
            pl.BlockSpec((1, KV_W, WINDOW), lambda b, s: (b, 0, 0)),
        ],
        out_shape=[
            jax.ShapeDtypeStruct((bsz, seq, D_MODEL), F32),
            jax.ShapeDtypeStruct((bsz, RET_HEADS, HEAD_DIM, HEAD_DIM), F32),
            jax.ShapeDtypeStruct((bsz, KV_W, WINDOW), F32),
            jax.ShapeDtypeStruct((bsz, KV_W, WINDOW), F32),
        ],
        scratch_shapes=[
            pltpu.VMEM((2, PROJ_ROWS, IN_COLS), F32),
            pltpu.VMEM((tq - PROJ_ROWS, IN_COLS), F32),
            pltpu.VMEM((tq, RET_W), F32),
            pltpu.VMEM((tq, RET_W), BF16),
            pltpu.VMEM((SWA_W, tq), F32),
            pltpu.VMEM((GROUPS, GROUP_W, GROUP_W), F32),
            pltpu.VMEM((CHUNK, KV_W), BF16),
            pltpu.VMEM((KV_W, CHUNK), BF16),
        ],
        compiler_params=pltpu.CompilerParams(
            dimension_semantics=("arbitrary", "arbitrary"), vmem_limit_bytes=VMEM_LIMIT_BYTES),
        name="mixer_prompt",
    )(x, x, w_in_p, w_out_p, lng, lnb, gnw, sink_st, dm_st, qw, kw, gbd, blk, bias)


def _sample_constants(t):
    lg = _log_gamma()
    hh = np.arange(RET_HEADS)
    tt = np.arange(t, dtype=np.float64)
    row_h = np.repeat(hh, t)
    row_t = np.tile(tt, RET_HEADS)
    col_h = np.repeat(hh, HEAD_DIM)
    same = (row_h[:, None] == col_h[None, :]).astype(np.float64)
    mask_q1 = same * QK_SCALE
    mask_qw = same * np.exp((row_t + 1.0) * lg[row_h])[:, None]
    mask_kw = same * (np.exp((t - 1.0 - row_t) * lg[row_h]) * QK_SCALE)[:, None]
    diff = row_t[:, None] - row_t[None, :]
    same_h = row_h[:, None] == row_h[None, :]
    dm = np.where(same_h & (diff >= 0), np.exp(np.where(diff >= 0, diff, 0.0) * lg[row_h][:, None]), 0.0)
    g_t = np.repeat(np.exp(t * lg), HEAD_DIM)[:, None] * np.ones((1, HEAD_DIM))
    slopes = _alibi_slopes()
    j = np.arange(WINDOW + t)
    dist = row_t[:, None] + WINDOW - j[None, :]
    valid = (dist >= 0) & (dist <= WINDOW)
    bias = np.where(valid, -slopes[row_h][:, None] * dist, NEG_BIG)
    f = lambda a: jnp.asarray(a, F32)
    return f(mask_q1), f(mask_qw), f(mask_kw), f(dm), f(g_t), f(bias)


def _mixer_sample_body(x_ref, st_ref, ck_ref, cv_ref, win_ref, wout_ref,
                       lng_ref, lnb_ref, gnw_ref, sink_ref,
                       mq1_ref, mqw_ref, mkw_ref, dm_ref, gt_ref, bias_ref,
                       y_ref, sto_ref, cko_ref, cvo_ref,
                       proj_ref, vh_ref, qs_ref, oh_ref, os_ref, merged_ref, *, sb, t):
    ht = RET_HEADS * t
    nrows = sb * t
    x = x_ref[...]
    proj_ref[...] = _dot(x.astype(BF16), win_ref[...])
    lane = lax.broadcasted_iota(jnp.int32, (nrows, LANES), 1)
    lo = lane < HEAD_DIM
    for p in range(PAIRS):
        v2 = proj_ref[:, _RV + p * LANES:_RV + (p + 1) * LANES]
        vh_ref[2 * p] = v2[:, :HEAD_DIM]
        vh_ref[2 * p + 1] = pltpu.roll(v2, HEAD_DIM, 1)[:, :HEAD_DIM]
        q2 = proj_ref[:, _SQ + p * LANES:_SQ + (p + 1) * LANES] * QK_SCALE
        zero = jnp.zeros_like(q2)
        qs_ref[p] = jnp.where(lo, q2, zero)
        qs_ref[SWA_GROUP + p] = jnp.where(lo, zero, q2)
    oh_ref[...] = jnp.zeros_like(oh_ref)
    knt = proj_ref[:, _SK:_SK + KV_W].T
    vnt = proj_ref[:, _SV:_SV + KV_W].T
    keep = lax.broadcasted_iota(jnp.int32, (KV_W, WINDOW), 1) < WINDOW - t

    mq1 = mq1_ref[...]
    mqw = mqw_ref[...]
    mkw = mkw_ref[...]
    dm = dm_ref[...]
    g_t = gt_ref[...]
    bias = bias_ref[...]
    sink = sink_ref[...]

    def per_seq(b, carry):
        r0 = pl.multiple_of(b * t, t)
        rows = pl.ds(r0, t)
        q_b = proj_ref[rows, _RQ:_RQ + RET_W]
        k_b = proj_ref[rows, _RK:_RK + RET_W]
        q_rep = jnp.concatenate([q_b] * RET_HEADS, axis=0)
        k_rep = jnp.concatenate([k_b] * RET_HEADS, axis=0)
        v_st = vh_ref[:, rows, :].reshape(ht, HEAD_DIM).astype(BF16)
        s0 = st_ref[b]
        sc = _dot_nt((q_rep * mq1).astype(BF16), k_rep.astype(BF16)) * dm
        o = _dot((q_rep * mqw).astype(BF16), s0.astype(BF16)) + _dot(sc.astype(BF16), v_st)
        oh_ref[:, rows, 0:HEAD_DIM] = o.reshape(RET_HEADS, t, HEAD_DIM)
        sto_ref[b] = g_t * s0 + _dot_tn((k_rep * mkw).astype(BF16), v_st)
        k_new = proj_ref[rows, _SK:_SK + KV_W]
        v_new = proj_ref[rows, _SV:_SV + KV_W]
        kt_old = ck_ref[b]
        vt_old = cv_ref[b]
        q_s = qs_ref[:, rows, :].reshape(ht, LANES).astype(BF16)
        s = jnp.concatenate([_dot(q_s, kt_old.astype(BF16)),
                             _dot_nt(q_s, k_new.astype(BF16))], axis=1) + bias
        m = jnp.maximum(jnp.max(s, axis=-1, keepdims=True), sink)
        e = jnp.exp(s - m)
        den = jnp.sum(e, axis=-1, keepdims=True) + jnp.exp(sink - m)
        eb = e.astype(BF16)
        o_s = (_dot_nt(eb[:, :WINDOW], vt_old.astype(BF16))
               + _dot(eb[:, WINDOW:], v_new.astype(BF16))) / den
        os_ref[:, rows, :] = o_s.reshape(SWA_HEADS, t, LANES)
        shift = WINDOW - t - r0
        cko_ref[b] = jnp.where(keep, pltpu.roll(kt_old, WINDOW - t, 1), pltpu.roll(knt, shift, 1))
        cvo_ref[b] = jnp.where(keep, pltpu.roll(vt_old, WINDOW - t, 1), pltpu.roll(vnt, shift, 1))
        return carry

    lax.fori_loop(0, sb, per_seq, 0, unroll=16)

    for p in range(PAIRS):
        cols = slice(p * LANES, (p + 1) * LANES)
        o2 = oh_ref[2 * p] + pltpu.roll(oh_ref[2 * p + 1], HEAD_DIM, 1)
        gn = _pair_group_norm(o2, lo, gnw_ref[:, cols])
        gate = proj_ref[:, _RG + p * LANES:_RG + (p + 1) * LANES]
        merged_ref[:, cols] = (gate * jax.nn.sigmoid(gate) * gn).astype(BF16)
        merged_ref[:, RET_W + p * LANES:RET_W + (p + 1) * LANES] = jnp.where(
            lo, os_ref[p], os_ref[SWA_GROUP + p]).astype(BF16)
    mix = _dot(merged_ref[...], wout_ref[...])
    y_ref[...] = _layer_norm(DN_ALPHA * x + mix, lng_ref[...], lnb_ref[...])


def _mixer_sample(x, state, ck, cv, w_in_p, w_out_p, lng, lnb, gnw, sink_rows, consts, *, sb, t):
    rows_total = x.shape[0]
    nseq = rows_total // t
    assert nseq % sb == 0 and sb * t == LANES and t % 8 == 0
    rows = sb * t
    weights = (w_in_p, w_out_p)
    body = functools.partial(_mixer_sample_body, sb=sb, t=t)
    return pl.pallas_call(
        body,
        grid=(nseq // sb,),
        in_specs=[
            pl.BlockSpec((rows, D_MODEL), lambda i: (i, 0)),
            pl.BlockSpec((sb, RET_W, HEAD_DIM), lambda i: (i, 0, 0)),
            pl.BlockSpec((sb, KV_W, WINDOW), lambda i: (i, 0, 0)),
            pl.BlockSpec((sb, KV_W, WINDOW), lambda i: (i, 0, 0)),
        ] + [_const_spec(w.shape) for w in weights] + [
            _const_spec(lng.shape), _const_spec(lnb.shape), _const_spec(gnw.shape),
            _const_spec(sink_rows.shape),
        ] + [_const_spec(c.shape) for c in consts],
        out_specs=[
            pl.BlockSpec((rows, D_MODEL), lambda i: (i, 0)),
            pl.BlockSpec((sb, RET_W, HEAD_DIM), lambda i: (i, 0, 0)),
            pl.BlockSpec((sb, KV_W, WINDOW), lambda i: (i, 0, 0)),
            pl.BlockSpec((sb, KV_W, WINDOW), lambda i: (i, 0, 0)),
        ],
        out_shape=[
            jax.ShapeDtypeStruct((rows_total, D_MODEL), F32),
            jax.ShapeDtypeStruct((nseq, RET_W, HEAD_DIM), F32),
            jax.ShapeDtypeStruct((nseq, KV_W, WINDOW), F32),
            jax.ShapeDtypeStruct((nseq, KV_W, WINDOW), F32),
        ],
        scratch_shapes=[
            pltpu.VMEM((rows, IN_COLS), F32),
            pltpu.VMEM((RET_HEADS, rows, HEAD_DIM), F32),
            pltpu.VMEM((SWA_HEADS, rows, LANES), F32),
            pltpu.VMEM((RET_HEADS, rows, LANES), F32),
            pltpu.VMEM((SWA_HEADS, rows, LANES), F32),
            pltpu.VMEM((rows, 2 * RET_W), BF16),
        ],
        compiler_params=pltpu.CompilerParams(
            dimension_semantics=("arbitrary",), vmem_limit_bytes=VMEM_LIMIT_BYTES),
        name="mixer_sample",
    )(x, state, ck, cv, *weights, lng, lnb, gnw, sink_rows, *consts)


def _prep_prompt_weights(w_in, w_out, sinks):
    sq = w_in[:, _SQ:_SQ + SWA_W].reshape(D_MODEL, SWA_HEADS, HEAD_DIM)
    sq = sq[:, jnp.asarray(_SWA_HEAD_ORDER), :].reshape(D_MODEL, SWA_W)
    w_in_p = jnp.concatenate([w_in[:, :_SQ], sq, w_in[:, _SK:]], axis=1).astype(BF16)
    wo_s = w_out[RET_W:].reshape(SWA_HEADS, HEAD_DIM, D_MODEL)
    wo_s = wo_s[jnp.asarray(_SWA_HEAD_ORDER)].reshape(SWA_W, D_MODEL)
    w_out_p = jnp.concatenate([w_out[:RET_W], wo_s], axis=0).astype(BF16)
    sink_st = jnp.repeat(sinks.astype(F32)[jnp.asarray(_SWA_HEAD_ORDER)], CHUNK)[None, :]
    return w_in_p, w_out_p, sink_st


def kernel(x_prompt, x_sample, state_ret, cache_swa_k, cache_swa_v, ln_gain, ln_bias, w_in,
           ret_gn_w, swa_sinks, w_out, ffn1_gate, ffn1_up, ffn1_down, ffn2_gate, ffn2_up, ffn2_down):
    assert ln_gain.shape[0] == DEPTH == 1
    bsz, seq, _ = x_prompt.shape
    nseq, t, _ = x_sample.shape
    lng = ln_gain[0].astype(F32).reshape(3, 1, D_MODEL)
    lnb = ln_bias[0].astype(F32).reshape(3, 1, D_MODEL)
    f1 = (ffn1_gate[0].astype(BF16), ffn1_up[0].astype(BF16), ffn1_down[0].astype(BF16))
    gnw = ret_gn_w[0].astype(F32)

    to_dims_major = lambda c: jnp.transpose(c, (0, 2, 3, 1)).reshape(c.shape[0], KV_W, WINDOW)
    to_pos_major = lambda c: jnp.transpose(
        c.reshape(c.shape[0], SWA_KV_HEADS, HEAD_DIM, WINDOW), (0, 3, 1, 2))[None]

    tm_p = min(1024, bsz * seq)
    tm_s = min(512, nseq * t)
    xp = x_prompt.reshape(bsz * seq, D_MODEL)
    xs = x_sample.reshape(nseq * t, D_MODEL)

    xp, f2 = _ffn_ln(xp, *f1, lng[0], lnb[0], tm=tm_p,
                     cast_next=(ffn2_gate[0], ffn2_up[0], ffn2_down[0]))
    w_in_p, w_out_p, sink_st = _prep_prompt_weights(w_in[0], w_out[0], swa_sinks[0])
    yp, st_p, ck_p, cv_p = _mixer_prompt(
        xp.reshape(bsz, seq, D_MODEL), w_in_p, w_out_p, lng[1], lnb[1], gnw.reshape(1, RET_W),
        sink_st, _prompt_constants(), tq=min(1024, seq))
    yp = _ffn_ln(yp.reshape(bsz * seq, D_MODEL), *f2, lng[2], lnb[2], tm=tm_p)

    xs = _ffn_ln(xs, *f1, lng[0], lnb[0], tm=tm_s)
    sink_rows = jnp.repeat(swa_sinks[0].astype(F32), t)[:, None]
    ys, st_s, ck_s, cv_s = _mixer_sample(
        xs, state_ret[0].astype(F32).reshape(nseq, RET_W, HEAD_DIM),
        to_dims_major(cache_swa_k[0]), to_dims_major(cache_swa_v[0]),
        w_in_p, w_out_p, lng[1], lnb[1], gnw.reshape(1, RET_W), sink_rows, _sample_constants(t),
        sb=LANES // t, t=t)
    ys = _ffn_ln(ys, *f2, lng[2], lnb[2], tm=tm_s)

    return (yp.reshape(bsz, seq, D_MODEL), ys.reshape(nseq, t, D_MODEL),
            st_p[None], to_pos_major(ck_p), to_pos_major(cv_p),
            st_s.reshape(1, nseq, RET_HEADS, HEAD_DIM, HEAD_DIM),
            to_pos_major(ck_s), to_pos_major(cv_s))
```

```python
import functools

import numpy as np
import jax
import jax.numpy as jnp
from jax import lax
from jax.experimental import pallas as pl
from jax.experimental.pallas import tpu as pltpu

F32 = jnp.float32
BF16 = jnp.bfloat16

D_MODEL = 1024
HEAD_DIM = 64
RET_HEADS = 8
SWA_HEADS = 8
SWA_KV_HEADS = 2
SWA_GROUP = SWA_HEADS // SWA_KV_HEADS
WINDOW = 128
CHUNK = 128
D_FF = 2816
RET_W = RET_HEADS * HEAD_DIM
SWA_W = SWA_HEADS * HEAD_DIM
KV_W = SWA_KV_HEADS * HEAD_DIM
IN_COLS = 4 * RET_W + SWA_W + 2 * KV_W
LN_EPS = 1e-5
GN_EPS = 1e-5
DEPTH = 1
DN_ALPHA = (2.0 * DEPTH) ** 0.25
QK_SCALE = HEAD_DIM ** -0.5
NEG_BIG = -1e30
LOG2E = 1.4426950408889634

LANES = 128
PAIRS = RET_HEADS // 2
GROUPS = RET_HEADS // 4
GROUP_W = 4 * HEAD_DIM
PROJ_ROWS = 256
PROJ_COLS = 256
VMEM_LIMIT_BYTES = 56 * 1024 * 1024

_RQ, _RK, _RV, _RG, _SQ = 0, RET_W, 2 * RET_W, 3 * RET_W, 4 * RET_W
_SK = 4 * RET_W + SWA_W
_SV = _SK + KV_W

_SWA_PAIR_HEADS = [(p, SWA_GROUP + p) for p in range(PAIRS)]
_SWA_HEAD_ORDER = [h for pair in _SWA_PAIR_HEADS for h in pair]

_FF_CHUNKS = ((0, 2816),)
FFN_ROW_BLOCK = 256


def _log_gamma():
    h = np.arange(RET_HEADS, dtype=np.float64)
    return np.log1p(-np.exp2(-5.0 - h))


def _alibi_slopes():
    return np.exp2(-8.0 / SWA_HEADS * np.arange(1, SWA_HEADS + 1, dtype=np.float64))


def _layer_norm(z, g, b):
    mu = jnp.mean(z, axis=-1, keepdims=True)
    d = z - mu
    var = jnp.mean(d * d, axis=-1, keepdims=True)
    return d * lax.rsqrt(var + LN_EPS) * g + b


def _pair_group_norm(o, lo, gain):
    zero = jnp.zeros_like(o)
    s_lo = jnp.sum(jnp.where(lo, o, zero), axis=-1, keepdims=True)
    s_hi = jnp.sum(jnp.where(lo, zero, o), axis=-1, keepdims=True)
    d = o - jnp.where(lo, s_lo, s_hi) * (1.0 / HEAD_DIM)
    d2 = d * d
    v_lo = jnp.sum(jnp.where(lo, d2, zero), axis=-1, keepdims=True)
    v_hi = jnp.sum(jnp.where(lo, zero, d2), axis=-1, keepdims=True)
    var = jnp.where(lo, v_lo, v_hi) * (1.0 / HEAD_DIM)
    return d * lax.rsqrt(var + GN_EPS) * gain


def _dot(a, b):
    return jnp.dot(a, b, preferred_element_type=F32)


def _dot_nt(a, b):
    return lax.dot_general(a, b, (((1,), (1,)), ((), ())), preferred_element_type=F32)


def _dot_tn(a, b):
    return lax.dot_general(a, b, (((0,), (0,)), ((), ())), preferred_element_type=F32)


def _ffn_ln_body(*refs, row_block, ncast):
    x_ref, wg_ref, wu_ref, wd_ref, g_ref, b_ref = refs[:6]
    o_ref = refs[6 + ncast]
    side_job_at = row_block if x_ref.shape[0] > row_block else 0
    for r0 in range(0, x_ref.shape[0], row_block):
        if r0 == side_job_at:
            for src, dst in zip(refs[6:6 + ncast], refs[7 + ncast:]):
                dst[...] = src[...].astype(BF16)
        x = x_ref[r0:r0 + row_block, :]
        xb = x.astype(BF16)
        acc = None
        for c0, cw in _FF_CHUNKS:
            gate = _dot(xb, wg_ref[:, c0:c0 + cw])
            up = _dot(xb, wu_ref[:, c0:c0 + cw])
            hid = (gate * jax.nn.sigmoid(gate) * up).astype(BF16)
            part = _dot(hid, wd_ref[c0:c0 + cw, :])
            acc = part if acc is None else acc + part
        z = DN_ALPHA * x + 0.5 * acc
        o_ref[r0:r0 + row_block, :] = _layer_norm(z, g_ref[...], b_ref[...])


def _const_spec(shape):
    nd = len(shape)
    return pl.BlockSpec(shape, lambda *_: (0,) * nd, pipeline_mode=pl.Buffered(1))


def _ffn_ln(x, wg, wu, wd, g, b, *, tm, cast_next=()):
    rows = x.shape[0]
    assert rows % tm == 0 and tm % FFN_ROW_BLOCK == 0
    steps = rows // tm
    slab_specs = []
    for w in cast_next:
        nslabs = max(n for n in range(1, steps + 1)
                     if steps % n == 0 and w.shape[0] % n == 0 and (w.shape[0] // n) % 16 == 0)
        every = steps // nslabs
        slab_specs.append(pl.BlockSpec((w.shape[0] // nslabs, w.shape[1]),
                                       functools.partial(lambda i, e: (i // e, 0), e=every)))
    outs = pl.pallas_call(
        functools.partial(_ffn_ln_body, row_block=FFN_ROW_BLOCK, ncast=len(cast_next)),
        grid=(steps,),
        in_specs=[
            pl.BlockSpec((tm, D_MODEL), lambda i: (i, 0)),
            _const_spec((D_MODEL, D_FF)),
            _const_spec((D_MODEL, D_FF)),
            _const_spec((D_FF, D_MODEL)),
            _const_spec((1, D_MODEL)),
            _const_spec((1, D_MODEL)),
        ] + slab_specs,
        out_specs=[pl.BlockSpec((tm, D_MODEL), lambda i: (i, 0))] + slab_specs,
        out_shape=[jax.ShapeDtypeStruct((rows, D_MODEL), F32)]
        + [jax.ShapeDtypeStruct(w.shape, BF16) for w in cast_next],
        compiler_params=pltpu.CompilerParams(
            dimension_semantics=("arbitrary",), vmem_limit_bytes=VMEM_LIMIT_BYTES),
        name="ffn_ln",
    )(x, wg, wu, wd, g, b, *cast_next)
    if not cast_next:
        return outs[0]
    return outs[0], tuple(outs[1:])


def _prompt_constants():
    lg = _log_gamma()
    idx = np.arange(CHUNK, dtype=np.float64)
    diff = idx[:, None] - idx[None, :]
    dm = np.where(diff >= 0, np.exp(np.where(diff >= 0, diff, 0.0)[None] * lg[:, None, None]), 0.0)
    dm_st = np.stack([np.block([[dm[4 * g], dm[4 * g + 2]], [dm[4 * g + 1], dm[4 * g + 3]]])
                      for g in range(GROUPS)])
    qw = np.repeat(np.exp((idx + 1.0)[:, None] * lg[None, :]), HEAD_DIM, axis=1)
    kw = np.repeat(np.exp((CHUNK - 1.0 - idx)[:, None] * lg[None, :]), HEAD_DIM, axis=1) * QK_SCALE
    gc = np.exp(CHUNK * lg)
    blk = np.kron(np.eye(4), np.ones((HEAD_DIM, HEAD_DIM)))
    gbd = np.stack([blk * np.repeat(gc[4 * g:4 * g + 4], HEAD_DIM)[:, None] for g in range(GROUPS)])
    slopes = _alibi_slopes()
    i = np.arange(CHUNK)
    j = np.arange(2 * CHUNK)
    dist = i[:, None] + CHUNK - j[None, :]
    within = (dist >= 0) & (dist <= WINDOW)
    valid = [within & (j >= CHUNK)[None, :], within]
    bias = np.stack([np.concatenate([np.where(valid[v], -slopes[h] * dist * LOG2E, NEG_BIG).T
                                     for h in _SWA_HEAD_ORDER], axis=1) for v in range(2)])
    f = lambda a: jnp.asarray(a, F32)
    return f(dm_st), f(qw), f(kw), f(gbd), f(blk), f(bias)


def _mixer_prompt_body(x_ref, xn_ref, win_ref, wout_ref, lng_ref, lnb_ref, gnw_ref, sink_ref,
                       dm_ref, qw_ref, kw_ref, gbd_ref, blk_ref, bias_ref,
                       y_ref, st_ref, ck_ref, cv_ref,
                       pb_ref, proj1_ref, oret_ref, yret_ref, swat_ref, s_ref, kprev_ref, vprev_ref,
                       *, tq):
    step = pl.program_id(1)
    nsteps = pl.num_programs(1)
    flat = pl.program_id(0) * nsteps + step
    slot = lax.rem(flat, 2)
    nchunks = tq // CHUNK
    nblocks = tq // PROJ_ROWS
    col_starts = range(0, IN_COLS, PROJ_COLS)

    @pl.when(step == 0)
    def _():
        s_ref[...] = jnp.zeros_like(s_ref)
        kprev_ref[...] = jnp.zeros_like(kprev_ref)
        vprev_ref[...] = jnp.zeros_like(vprev_ref)

    x = x_ref[0]

    def project_piece(dst, xrows, c0):
        c1 = min(c0 + PROJ_COLS, IN_COLS)
        dst[:, c0:c1] = _dot(xrows.astype(BF16), win_ref[:, c0:c1])

    @pl.when(flat == 0)
    def _():
        for c0 in col_starts:
            project_piece(pb_ref.at[0], x[:PROJ_ROWS], c0)

    def proj(r0, nrows, c0, c1):
        if r0 < PROJ_ROWS:
            return pb_ref[slot, r0:r0 + nrows, c0:c1]
        return proj1_ref[r0 - PROJ_ROWS:r0 - PROJ_ROWS + nrows, c0:c1]

    pending = []

    def emit_pieces(n):
        for _ in range(min(n, len(pending))):
            pending.pop(0)()

    pieces_per_chunk = pl.cdiv(len(col_starts), PROJ_ROWS // CHUNK)

    lane = lax.broadcasted_iota(jnp.int32, (CHUNK, LANES), 1)
    lo = lane < HEAD_DIM
    lane_g = lax.broadcasted_iota(jnp.int32, (CHUNK, GROUP_W), 1)
    lo_g = jnp.bitwise_and(lane_g, LANES - 1) < HEAD_DIM
    blk = blk_ref[...]
    zero_b = jnp.zeros((CHUNK, LANES), BF16)

    def pair_diag(a):
        return jnp.concatenate([jnp.concatenate([a[:, :LANES], zero_b], axis=1),
                                jnp.concatenate([zero_b, a[:, LANES:]], axis=1)], axis=0)

    def head_stack(a):
        zero = jnp.zeros_like(a)
        return jnp.concatenate([jnp.where(lo_g, a, zero), jnp.where(lo_g, zero, a)], axis=0)

    lane_t = lax.broadcasted_iota(jnp.int32, (PROJ_ROWS, LANES), 1)
    lo_t = lane_t < HEAD_DIM

    def finish_block(kb):
        rb = slice(kb * PROJ_ROWS, (kb + 1) * PROJ_ROWS)
        for p in range(PAIRS):
            cols = slice(p * LANES, (p + 1) * LANES)
            gn = _pair_group_norm(oret_ref[rb, cols], lo_t, gnw_ref[:, cols])
            gate = proj(kb * PROJ_ROWS, PROJ_ROWS, _RG + p * LANES, _RG + (p + 1) * LANES)
            yret_ref[rb, cols] = (gate * jax.nn.sigmoid(gate) * gn).astype(BF16)
        mix = (_dot(yret_ref[rb, :], wout_ref[:RET_W, :])
               + _dot_tn(swat_ref[:, rb].astype(BF16), wout_ref[RET_W:, :]))
        y_ref[0, rb, :] = _layer_norm(DN_ALPHA * x[rb] + mix, lng_ref[...], lnb_ref[...])

    k_prev = kprev_ref[...]
    vt_prev = vprev_ref[...]
    for c in range(nchunks):
        r0 = c * CHUNK
        rows = slice(r0, r0 + CHUNK)
        if r0 % PROJ_ROWS == 0:
            nb = r0 // PROJ_ROWS + 1
            if nb < nblocks:
                dst = proj1_ref.at[(nb - 1) * PROJ_ROWS:nb * PROJ_ROWS]
                xrows = x[nb * PROJ_ROWS:(nb + 1) * PROJ_ROWS]
            else:
                dst, xrows = pb_ref.at[1 - slot], xn_ref[0]
            pending.extend(functools.partial(project_piece, dst, xrows, c0) for c0 in col_starts)
        budget = pieces_per_chunk
        for g in range(GROUPS):
            if budget > 0:
                emit_pieces(1)
                budget -= 1
            cols = slice(g * GROUP_W, (g + 1) * GROUP_W)
            q4 = proj(r0, CHUNK, _RQ + g * GROUP_W, _RQ + (g + 1) * GROUP_W)
            k4 = proj(r0, CHUNK, _RK + g * GROUP_W, _RK + (g + 1) * GROUP_W)
            v4 = proj(r0, CHUNK, _RV + g * GROUP_W, _RV + (g + 1) * GROUP_W).astype(BF16)
            k_bd = pair_diag((k4 * QK_SCALE).astype(BF16))
            sc = _dot_nt(head_stack(q4).astype(BF16), k_bd) * dm_ref[g]
            qd_st = head_stack(q4 * qw_ref[:, cols])
            lhs = jnp.concatenate([sc.astype(BF16), qd_st.astype(BF16)], axis=1)
            s_old = s_ref[g]
            rhs = jnp.concatenate([pair_diag(v4), s_old.astype(BF16)], axis=0)
            r = _dot(lhs, rhs)
            oret_ref[rows, cols] = jnp.where(lo_g, r[:CHUNK], r[CHUNK:])
            kd = (k4 * kw_ref[:, cols]).astype(BF16)
            s_ref[g] = gbd_ref[g] * s_old + blk * _dot_tn(kd, v4)
        k_cur = proj(r0, CHUNK, _SK, _SK + KV_W).astype(BF16)
        vt_cur = proj(r0, CHUNK, _SV, _SV + KV_W).T.astype(BF16)
        kk = jnp.concatenate([k_prev, k_cur], axis=0)
        vvt = jnp.concatenate([vt_prev, vt_cur], axis=1)
        q_parts = []
        for p in range(PAIRS):
            q2 = proj(r0, CHUNK, _SQ + p * LANES, _SQ + (p + 1) * LANES) * (QK_SCALE * LOG2E)
            zero = jnp.zeros_like(q2)
            q_parts += [jnp.where(lo, q2, zero).astype(BF16), jnp.where(lo, zero, q2).astype(BF16)]
        st = _dot_nt(kk, jnp.concatenate(q_parts, axis=0))
        if c == 0:
            st = st + bias_ref[jnp.minimum(step, 1)]
        else:
            st = st + bias_ref[1]
        sink = sink_ref[...] * LOG2E
        e_parts, den_parts = [], []
        for p in range(SWA_HEADS):
            if budget > 0 and p % 2 == 1:
                emit_pieces(1)
                budget -= 1
            pc = slice(p * CHUNK, (p + 1) * CHUNK)
            st_p, sink_p = st[:, pc], sink[:, pc]
            m = jnp.maximum(jnp.max(st_p, axis=0, keepdims=True), sink_p)
            e = jnp.exp2(st_p - m)
            den_parts.append(jnp.sum(e, axis=0, keepdims=True) + jnp.exp2(sink_p - m))
            e_parts.append(e.astype(BF16))
        if (r0 + CHUNK) % PROJ_ROWS == 0:
            emit_pieces(len(pending))
        den = jnp.concatenate(den_parts, axis=1)
        ot = _dot(vvt, jnp.concatenate(e_parts, axis=1)) * (1.0 / den)
        for p in range(PAIRS):
            c0 = 2 * p * CHUNK
            swat_ref[p * LANES:p * LANES + HEAD_DIM, rows] = ot[:HEAD_DIM, c0:c0 + CHUNK]
            swat_ref[p * LANES + HEAD_DIM:(p + 1) * LANES, rows] = ot[HEAD_DIM:, c0 + CHUNK:c0 + 2 * CHUNK]
        k_prev, vt_prev = k_cur, vt_cur
        if (r0 + CHUNK) % PROJ_ROWS == 0:
            finish_block(r0 // PROJ_ROWS)
    kprev_ref[...] = k_prev
    vprev_ref[...] = vt_prev

    @pl.when(step == nsteps - 1)
    def _():
        for h in range(RET_HEADS):
            g, i = divmod(h, 4)
            blk_h = s_ref[g, i * HEAD_DIM:(i + 1) * HEAD_DIM, (i // 2) * LANES:(i // 2 + 1) * LANES]
            if i % 2:
                blk_h = pltpu.roll(blk_h, HEAD_DIM, 1)
            st_ref[0, h] = blk_h[:, :HEAD_DIM]
        ck_ref[0] = proj(tq - WINDOW, WINDOW, _SK, _SK + KV_W).T
        cv_ref[0] = proj(tq - WINDOW, WINDOW, _SV, _SV + KV_W).T


def _mixer_prompt(x, w_in_p, w_out_p, lng, lnb, gnw, sink_st, consts, *, tq):
    bsz, seq, _ = x.shape
    assert seq % tq == 0 and tq % PROJ_ROWS == 0 and PROJ_ROWS % CHUNK == 0
    dm_st, qw, kw, gbd, blk, bias = consts
    body = functools.partial(_mixer_prompt_body, tq=tq)
    nsteps = seq // tq
    blocks_per_tile = tq // PROJ_ROWS

    def next_tile_head(b, s):
        nxt = jnp.minimum(b * nsteps + s + 1, bsz * nsteps - 1)
        return nxt // nsteps, (nxt % nsteps) * blocks_per_tile, 0

    return pl.pallas_call(
        body,
        grid=(bsz, nsteps),
        in_specs=[
            pl.BlockSpec((1, tq, D_MODEL), lambda b, s: (b, s, 0)),
            pl.BlockSpec((1, PROJ_ROWS, D_MODEL), next_tile_head),
            _const_spec((D_MODEL, IN_COLS)),
            _const_spec((2 * RET_W, D_MODEL)),
            _const_spec((1, D_MODEL)),
            _const_spec((1, D_MODEL)),
            _const_spec((1, RET_W)),
            _const_spec((1, SWA_HEADS * CHUNK)),
            _const_spec(dm_st.shape),
            _const_spec(qw.shape),
            _const_spec(kw.shape),
            _const_spec(gbd.shape),
            _const_spec(blk.shape),
            _const_spec(bias.shape),
        ],
        out_specs=[
            pl.BlockSpec((1, tq, D_MODEL), lambda b, s: (b, s, 0)),
            pl.BlockSpec((1, RET_HEADS, HEAD_DIM, HEAD_DIM), lambda b, s: (b, 0, 0, 0)),
            pl.BlockSpec((1, KV_W, WINDOW), lambda b, s: (b, 0, 0)),
            pl.BlockSpec((1, KV_W, WINDOW), lambda b, s: (b, 0, 0)),
        ],
        out_shape=[
            jax.ShapeDtypeStruct((bsz, seq, D_MODEL), F32),
            jax.ShapeDtypeStruct((bsz, RET_HEADS, HEAD_DIM, HEAD_DIM), F32),
            jax.ShapeDtypeStruct((bsz, KV_W, WINDOW), F32),
            jax.ShapeDtypeStruct((bsz, KV_W, WINDOW), F32),
        ],
        scratch_shapes=[
            pltpu.VMEM((2, PROJ_ROWS, IN_COLS), F32),
            pltpu.VMEM((tq - PROJ_ROWS, IN_COLS), F32),
            pltpu.VMEM((tq, RET_W), F32),
            pltpu.VMEM((tq, RET_W), BF16),
            pltpu.VMEM((SWA_W, tq), F32),
            pltpu.VMEM((GROUPS, GROUP_W, GROUP_W), F32),
            pltpu.VMEM((CHUNK, KV_W), BF16),
            pltpu.VMEM((KV_W, CHUNK), BF16),
        ],
        compiler_params=pltpu.CompilerParams(
            dimension_semantics=("arbitrary", "arbitrary"), vmem_limit_bytes=VMEM_LIMIT_BYTES),
        name="mixer_prompt",
    )(x, x, w_in_p, w_out_p, lng, lnb, gnw, sink_st, dm_st, qw, kw, gbd, blk, bias)


def _sample_constants(t):
    lg = _log_gamma()
    hh = np.arange(RET_HEADS)
    tt = np.arange(t, dtype=np.float64)
    row_h = np.repeat(hh, t)
    row_t = np.tile(tt, RET_HEADS)
    col_h = np.repeat(hh, HEAD_DIM)
    same = (row_h[:, None] == col_h[None, :]).astype(np.float64)
    mask_q1 = same * QK_SCALE
    mask_qw = same * np.exp((row_t + 1.0) * lg[row_h])[:, None]
    mask_kw = same * (np.exp((t - 1.0 - row_t) * lg[row_h]) * QK_SCALE)[:, None]
    diff = row_t[:, None] - row_t[None, :]
    same_h = row_h[:, None] == row_h[None, :]
    dm = np.where(same_h & (diff >= 0), np.exp(np.where(diff >= 0, diff, 0.0) * lg[row_h][:, None]), 0.0)
    g_t = np.repeat(np.exp(t * lg), HEAD_DIM)[:, None] * np.ones((1, HEAD_DIM))
    slopes = _alibi_slopes()
    j = np.arange(WINDOW + t)
    dist = row_t[:, None] + WINDOW - j[None, :]
    valid = (dist >= 0) & (dist <= WINDOW)
    bias = np.where(valid, -slopes[row_h][:, None] * dist, NEG_BIG)
    f = lambda a: jnp.asarray(a, F32)
    return f(mask_q1), f(mask_qw), f(mask_kw), f(dm), f(g_t), f(bias)


def _mixer_sample_body(x_ref, st_ref, ck_ref, cv_ref, win_ref, wout_ref,
                       lng_ref, lnb_ref, gnw_ref, sink_ref,
                       mq1_ref, mqw_ref, mkw_ref, dm_ref, gt_ref, bias_ref,
                       y_ref, sto_ref, cko_ref, cvo_ref,
                       proj_ref, vh_ref, qs_ref, oh_ref, os_ref, merged_ref, *, sb, t):
    ht = RET_HEADS * t
    nrows = sb * t
    x = x_ref[...]
    proj_ref[...] = _dot(x.astype(BF16), win_ref[...])
    lane = lax.broadcasted_iota(jnp.int32, (nrows, LANES), 1)
    lo = lane < HEAD_DIM
    for p in range(PAIRS):
        v2 = proj_ref[:, _RV + p * LANES:_RV + (p + 1) * LANES]
        vh_ref[2 * p] = v2[:, :HEAD_DIM]
        vh_ref[2 * p + 1] = pltpu.roll(v2, HEAD_DIM, 1)[:, :HEAD_DIM]
        q2 = proj_ref[:, _SQ + p * LANES:_SQ + (p + 1) * LANES] * QK_SCALE
        zero = jnp.zeros_like(q2)
        qs_ref[p] = jnp.where(lo, q2, zero)
        qs_ref[SWA_GROUP + p] = jnp.where(lo, zero, q2)
    oh_ref[...] = jnp.zeros_like(oh_ref)
    knt = proj_ref[:, _SK:_SK + KV_W].T
    vnt = proj_ref[:, _SV:_SV + KV_W].T
    keep = lax.broadcasted_iota(jnp.int32, (KV_W, WINDOW), 1) < WINDOW - t

    mq1 = mq1_ref[...]
    mqw = mqw_ref[...]
    mkw = mkw_ref[...]
    dm = dm_ref[...]
    g_t = gt_ref[...]
    bias = bias_ref[...]
    sink = sink_ref[...]

    def per_seq(b, carry):
        r0 = pl.multiple_of(b * t, t)
        rows = pl.ds(r0, t)
        q_b = proj_ref[rows, _RQ:_RQ + RET_W]
        k_b = proj_ref[rows, _RK:_RK + RET_W]
        q_rep = jnp.concatenate([q_b] * RET_HEADS, axis=0)
        k_rep = jnp.concatenate([k_b] * RET_HEADS, axis=0)
        v_st = vh_ref[:, rows, :].reshape(ht, HEAD_DIM).astype(BF16)
        s0 = st_ref[b]
        sc = _dot_nt((q_rep * mq1).astype(BF16), k_rep.astype(BF16)) * dm
        o = _dot((q_rep * mqw).astype(BF16), s0.astype(BF16)) + _dot(sc.astype(BF16), v_st)
        oh_ref[:, rows, 0:HEAD_DIM] = o.reshape(RET_HEADS, t, HEAD_DIM)
        sto_ref[b] = g_t * s0 + _dot_tn((k_rep * mkw).astype(BF16), v_st)
        k_new = proj_ref[rows, _SK:_SK + KV_W]
        v_new = proj_ref[rows, _SV:_SV + KV_W]
        kt_old = ck_ref[b]
        vt_old = cv_ref[b]
        q_s = qs_ref[:, rows, :].reshape(ht, LANES).astype(BF16)
        s = jnp.concatenate([_dot(q_s, kt_old.astype(BF16)),
                             _dot_nt(q_s, k_new.astype(BF16))], axis=1) + bias
        m = jnp.maximum(jnp.max(s, axis=-1, keepdims=True), sink)
        e = jnp.exp(s - m)
        den = jnp.sum(e, axis=-1, keepdims=True) + jnp.exp(sink - m)
        eb = e.astype(BF16)
        o_s = (_dot_nt(eb[:, :WINDOW], vt_old.astype(BF16))
               + _dot(eb[:, WINDOW:], v_new.astype(BF16))) / den
        os_ref[:, rows, :] = o_s.reshape(SWA_HEADS, t, LANES)
        shift = WINDOW - t - r0
        cko_ref[b] = jnp.where(keep, pltpu.roll(kt_old, WINDOW - t, 1), pltpu.roll(knt, shift, 1))
        cvo_ref[b] = jnp.where(keep, pltpu.roll(vt_old, WINDOW - t, 1), pltpu.roll(vnt, shift, 1))
        return carry

    lax.fori_loop(0, sb, per_seq, 0, unroll=16)

    for p in range(PAIRS):
        cols = slice(p * LANES, (p + 1) * LANES)
        o2 = oh_ref[2 * p] + pltpu.roll(oh_ref[2 * p + 1], HEAD_DIM, 1)
        gn = _pair_group_norm(o2, lo, gnw_ref[:, cols])
        gate = proj_ref[:, _RG + p * LANES:_RG + (p + 1) * LANES]
        merged_ref[:, cols] = (gate * jax.nn.sigmoid(gate) * gn).astype(BF16)
        merged_ref[:, RET_W + p * LANES:RET_W + (p + 1) * LANES] = jnp.where(
            lo, os_ref[p], os_ref[SWA_GROUP + p]).astype(BF16)
    mix = _dot(merged_ref[...], wout_ref[...])
    y_ref[...] = _layer_norm(DN_ALPHA * x + mix, lng_ref[...], lnb_ref[...])


def _mixer_sample(x, state, ck, cv, w_in_p, w_out_p, lng, lnb, gnw, sink_rows, consts, *, sb, t):
    rows_total = x.shape[0]
    nseq = rows_total // t
    assert nseq % sb == 0 and sb * t == LANES and t % 8 == 0
    rows = sb * t
    weights = (w_in_p, w_out_p)
    body = functools.partial(_mixer_sample_body, sb=sb, t=t)
    return pl.pallas_call(
        body,
        grid=(nseq // sb,),
        in_specs=[
            pl.BlockSpec((rows, D_MODEL), lambda i: (i, 0)),
            pl.BlockSpec((sb, RET_W, HEAD_DIM), lambda i: (i, 0, 0)),
            pl.BlockSpec((sb, KV_W, WINDOW), lambda i: (i, 0, 0)),
            pl.BlockSpec((sb, KV_W, WINDOW), lambda i: (i, 0, 0)),
        ] + [_const_spec(w.shape) for w in weights] + [
            _const_spec(lng.shape), _const_spec(lnb.shape), _const_spec(gnw.shape),
            _const_spec(sink_rows.shape),
        ] + [_const_spec(c.shape) for c in consts],
        out_specs=[
            pl.BlockSpec((rows, D_MODEL), lambda i: (i, 0)),
            pl.BlockSpec((sb, RET_W, HEAD_DIM), lambda i: (i, 0, 0)),
            pl.BlockSpec((sb, KV_W, WINDOW), lambda i: (i, 0, 0)),
            pl.BlockSpec((sb, KV_W, WINDOW), lambda i: (i, 0, 0)),
        ],
        out_shape=[
            jax.ShapeDtypeStruct((rows_total, D_MODEL), F32),
            jax.ShapeDtypeStruct((nseq, RET_W, HEAD_DIM), F32),
            jax.ShapeDtypeStruct((nseq, KV_W, WINDOW), F32),
            jax.ShapeDtypeStruct((nseq, KV_W, WINDOW), F32),
        ],
        scratch_shapes=[
            pltpu.VMEM((rows, IN_COLS), F32),
            pltpu.VMEM((RET_HEADS, rows, HEAD_DIM), F32),
            pltpu.VMEM((SWA_HEADS, rows, LANES), F32),
            pltpu.VMEM((RET_HEADS, rows, LANES), F32),
            pltpu.VMEM((SWA_HEADS, rows, LANES), F32),
            pltpu.VMEM((rows, 2 * RET_W), BF16),
        ],
        compiler_params=pltpu.CompilerParams(
            dimension_semantics=("arbitrary",), vmem_limit_bytes=VMEM_LIMIT_BYTES),
        name="mixer_sample",
    )(x, state, ck, cv, *weights, lng, lnb, gnw, sink_rows, *consts)


def _prep_prompt_weights(w_in, w_out, sinks):
    sq = w_in[:, _SQ:_SQ + SWA_W].reshape(D_MODEL, SWA_HEADS, HEAD_DIM)
    sq = sq[:, jnp.asarray(_SWA_HEAD_ORDER), :].reshape(D_MODEL, SWA_W)
    w_in_p = jnp.concatenate([w_in[:, :_SQ], sq, w_in[:, _SK:]], axis=1).astype(BF16)
    wo_s = w_out[RET_W:].reshape(SWA_HEADS, HEAD_DIM, D_MODEL)
    wo_s = wo_s[jnp.asarray(_SWA_HEAD_ORDER)].reshape(SWA_W, D_MODEL)
    w_out_p = jnp.concatenate([w_out[:RET_W], wo_s], axis=0).astype(BF16)
    sink_st = jnp.repeat(sinks.astype(F32)[jnp.asarray(_SWA_HEAD_ORDER)], CHUNK)[None, :]
    return w_in_p, w_out_p, sink_st


def kernel(x_prompt, x_sample, state_ret, cache_swa_k, cache_swa_v, ln_gain, ln_bias, w_in,
           ret_gn_w, swa_sinks, w_out, ffn1_gate, ffn1_up, ffn1_down, ffn2_gate, ffn2_up, ffn2_down):
    assert ln_gain.shape[0] == DEPTH == 1
    bsz, seq, _ = x_prompt.shape
    nseq, t, _ = x_sample.shape
    lng = ln_gain[0].astype(F32).reshape(3, 1, D_MODEL)
    lnb = ln_bias[0].astype(F32).reshape(3, 1, D_MODEL)
    f1 = (ffn1_gate[0].astype(BF16), ffn1_up[0].astype(BF16), ffn1_down[0].astype(BF16))
    gnw = ret_gn_w[0].astype(F32)

    to_dims_major = lambda c: jnp.transpose(c, (0, 2, 3, 1)).reshape(c.shape[0], KV_W, WINDOW)
    to_pos_major = lambda c: jnp.transpose(
        c.reshape(c.shape[0], SWA_KV_HEADS, HEAD_DIM, WINDOW), (0, 3, 1, 2))[None]

    tm_p = min(1024, bsz * seq)
    tm_s = min(512, nseq * t)
    xp = x_prompt.reshape(bsz * seq, D_MODEL)
    xs = x_sample.reshape(nseq * t, D_MODEL)

    xp, f2 = _ffn_ln(xp, *f1, lng[0], lnb[0], tm=tm_p,
                     cast_next=(ffn2_gate[0], ffn2_up[0], ffn2_down[0]))
    w_in_p, w_out_p, sink_st = _prep_prompt_weights(w_in[0], w_out[0], swa_sinks[0])
    yp, st_p, ck_p, cv_p = _mixer_prompt(
        xp.reshape(bsz, seq, D_MODEL), w_in_p, w_out_p, lng[1], lnb[1], gnw.reshape(1, RET_W),
        sink_st, _prompt_constants(), tq=min(1024, seq))
    yp = _ffn_ln(yp.reshape(bsz * seq, D_MODEL), *f2, lng[2], lnb[2], tm=tm_p)

    xs = _ffn_ln(xs, *f1, lng[0], lnb[0], tm=tm_s)
    sink_rows = jnp.repeat(swa_sinks[0].astype(F32), t)[:, None]
    ys, st_s, ck_s, cv_s = _mixer_sample(
        xs, state_ret[0].astype(F32).reshape(nseq, RET_W, HEAD_DIM),
        to_dims_major(cache_swa_k[0]), to_dims_major(cache_swa_v[0]),
        w_in_p, w_out_p, lng[1], lnb[1], gnw.reshape(1, RET_W), sink_rows, _sample_constants(t),
        sb=LANES // t, t=t)
    ys = _ffn_ln(ys, *f2, lng[2], lnb[2], tm=tm_s)

    return (yp.reshape(bsz, seq, D_MODEL), ys.reshape(nseq, t, D_MODEL),
            st_p[None], to_pos_major(ck_p), to_pos_major(cv_p),
            st_s.reshape(1, nseq, RET_HEADS, HEAD_DIM, HEAD_DIM),
            to_pos_major(ck_s), to_pos_major(cv_s))
```

```python
import functools

import numpy as np
import jax
import jax.numpy as jnp
from jax import lax
from jax.experimental import pallas as pl
from jax.experimental.pallas import tpu as pltpu

F32 = jnp.float32
BF16 = jnp.bfloat16

D_MODEL = 1024
HEAD_DIM = 64
RET_HEADS = 8
SWA_HEADS = 8
SWA_KV_HEADS = 2
SWA_GROUP = SWA_HEADS // SWA_KV_HEADS
WINDOW = 128
CHUNK = 128
D_FF = 2816
RET_W = RET_HEADS * HEAD_DIM
SWA_W = SWA_HEADS * HEAD_DIM
KV_W = SWA_KV_HEADS * HEAD_DIM
IN_COLS = 4 * RET_W + SWA_W + 2 * KV_W
LN_EPS = 1e-5
GN_EPS = 1e-5
DEPTH = 1
DN_ALPHA = (2.0 * DEPTH) ** 0.25
QK_SCALE = HEAD_DIM ** -0.5
NEG_BIG = -1e30
LOG2E = 1.4426950408889634

LANES = 128
PAIRS = RET_HEADS // 2
GROUPS = RET_HEADS // 4
GROUP_W = 4 * HEAD_DIM
PROJ_ROWS = 256
PROJ_COLS = 256
VMEM_LIMIT_BYTES = 56 * 1024 * 1024

_RQ, _RK, _RV, _RG, _SQ = 0, RET_W, 2 * RET_W, 3 * RET_W, 4 * RET_W
_SK = 4 * RET_W + SWA_W
_SV = _SK + KV_W

_SWA_PAIR_HEADS = [(p, SWA_GROUP + p) for p in range(PAIRS)]
_SWA_HEAD_ORDER = [h for pair in _SWA_PAIR_HEADS for h in pair]

_FF_CHUNKS = ((0, 2816),)
FFN_ROW_BLOCK = 256


def _log_gamma():
    h = np.arange(RET_HEADS, dtype=np.float64)
    return np.log1p(-np.exp2(-5.0 - h))


def _alibi_slopes():
    return np.exp2(-8.0 / SWA_HEADS * np.arange(1, SWA_HEADS + 1, dtype=np.float64))


def _layer_norm(z, g, b):
    mu = jnp.mean(z, axis=-1, keepdims=True)
    d = z - mu
    var = jnp.mean(d * d, axis=-1, keepdims=True)
    return d * lax.rsqrt(var + LN_EPS) * g + b


def _pair_group_norm(o, lo, gain):
    zero = jnp.zeros_like(o)
    s_lo = jnp.sum(jnp.where(lo, o, zero), axis=-1, keepdims=True)
    s_hi = jnp.sum(jnp.where(lo, zero, o), axis=-1, keepdims=True)
    d = o - jnp.where(lo, s_lo, s_hi) * (1.0 / HEAD_DIM)
    d2 = d * d
    v_lo = jnp.sum(jnp.where(lo, d2, zero), axis=-1, keepdims=True)
    v_hi = jnp.sum(jnp.where(lo, zero, d2), axis=-1, keepdims=True)
    var = jnp.where(lo, v_lo, v_hi) * (1.0 / HEAD_DIM)
    return d * lax.rsqrt(var + GN_EPS) * gain


def _dot(a, b):
    return jnp.dot(a, b, preferred_element_type=F32)


def _dot_nt(a, b):
    return lax.dot_general(a, b, (((1,), (1,)), ((), ())), preferred_element_type=F32)


def _dot_tn(a, b):
    return lax.dot_general(a, b, (((0,), (0,)), ((), ())), preferred_element_type=F32)


def _ffn_ln_body(*refs, row_block, ncast):
    x_ref, wg_ref, wu_ref, wd_ref, g_ref, b_ref = refs[:6]
    o_ref = refs[6 + ncast]
    side_job_at = row_block if x_ref.shape[0] > row_block else 0
    for r0 in range(0, x_ref.shape[0], row_block):
        if r0 == side_job_at:
            for src, dst in zip(refs[6:6 + ncast], refs[7 + ncast:]):
                dst[...] = src[...].astype(BF16)
        x = x_ref[r0:r0 + row_block, :]
        xb = x.astype(BF16)
        acc = None
        for c0, cw in _FF_CHUNKS:
            gate = _dot(xb, wg_ref[:, c0:c0 + cw])
            up = _dot(xb, wu_ref[:, c0:c0 + cw])
            hid = (gate * jax.nn.sigmoid(gate) * up).astype(BF16)
            part = _dot(hid, wd_ref[c0:c0 + cw, :])
            acc = part if acc is None else acc + part
        z = DN_ALPHA * x + 0.5 * acc
        o_ref[r0:r0 + row_block, :] = _layer_norm(z, g_ref[...], b_ref[...])


def _const_spec(shape):
    nd = len(shape)
    return pl.BlockSpec(shape, lambda *_: (0,) * nd, pipeline_mode=pl.Buffered(1))


def _ffn_ln(x, wg, wu, wd, g, b, *, tm, cast_next=()):
    rows = x.shape[0]
    assert rows % tm == 0 and tm % FFN_ROW_BLOCK == 0
    steps = rows // tm
    slab_specs = []
    for w in cast_next:
        nslabs = max(n for n in range(1, steps + 1)
                     if steps % n == 0 and w.shape[0] % n == 0 and (w.shape[0] // n) % 16 == 0)
        every = steps // nslabs
        slab_specs.append(pl.BlockSpec((w.shape[0] // nslabs, w.shape[1]),
                                       functools.partial(lambda i, e: (i // e, 0), e=every)))
    outs = pl.pallas_call(
        functools.partial(_ffn_ln_body, row_block=FFN_ROW_BLOCK, ncast=len(cast_next)),
        grid=(steps,),
        in_specs=[
            pl.BlockSpec((tm, D_MODEL), lambda i: (i, 0)),
            _const_spec((D_MODEL, D_FF)),
            _const_spec((D_MODEL, D_FF)),
            _const_spec((D_FF, D_MODEL)),
            _const_spec((1, D_MODEL)),
            _const_spec((1, D_MODEL)),
        ] + slab_specs,
        out_specs=[pl.BlockSpec((tm, D_MODEL), lambda i: (i, 0))] + slab_specs,
        out_shape=[jax.ShapeDtypeStruct((rows, D_MODEL), F32)]
        + [jax.ShapeDtypeStruct(w.shape, BF16) for w in cast_next],
        compiler_params=pltpu.CompilerParams(
            dimension_semantics=("arbitrary",), vmem_limit_bytes=VMEM_LIMIT_BYTES),
        name="ffn_ln",
    )(x, wg, wu, wd, g, b, *cast_next)
    if not cast_next:
        return outs[0]
    return outs[0], tuple(outs[1:])


def _prompt_constants():
    lg = _log_gamma()
    idx = np.arange(CHUNK, dtype=np.float64)
    diff = idx[:, None] - idx[None, :]
    dm = np.where(diff >= 0, np.exp(np.where(diff >= 0, diff, 0.0)[None] * lg[:, None, None]), 0.0)
    dm_st = np.stack([np.block([[dm[4 * g], dm[4 * g + 2]], [dm[4 * g + 1], dm[4 * g + 3]]])
                      for g in range(GROUPS)])
    qw = np.repeat(np.exp((idx + 1.0)[:, None] * lg[None, :]), HEAD_DIM, axis=1)
    kw = np.repeat(np.exp((CHUNK - 1.0 - idx)[:, None] * lg[None, :]), HEAD_DIM, axis=1) * QK_SCALE
    gc = np.exp(CHUNK * lg)
    blk = np.kron(np.eye(4), np.ones((HEAD_DIM, HEAD_DIM)))
    gbd = np.stack([blk * np.repeat(gc[4 * g:4 * g + 4], HEAD_DIM)[:, None] for g in range(GROUPS)])
    slopes = _alibi_slopes()
    i = np.arange(CHUNK)
    j = np.arange(2 * CHUNK)
    dist = i[:, None] + CHUNK - j[None, :]
    within = (dist >= 0) & (dist <= WINDOW)
    valid = [within & (j >= CHUNK)[None, :], within]
    bias = np.stack([np.concatenate([np.where(valid[v], -slopes[h] * dist * LOG2E, NEG_BIG).T
                                     for h in _SWA_HEAD_ORDER], axis=1) for v in range(2)])
    f = lambda a: jnp.asarray(a, F32)
    return f(dm_st), f(qw), f(kw), f(gbd), f(blk), f(bias)


def _mixer_prompt_body(x_ref, xn_ref, win_ref, wout_ref, lng_ref, lnb_ref, gnw_ref, sink_ref,
                       dm_ref, qw_ref, kw_ref, gbd_ref, blk_ref, bias_ref,
                       y_ref, st_ref, ck_ref, cv_ref,
                       pb_ref, proj1_ref, oret_ref, yret_ref, swat_ref, s_ref, kprev_ref, vprev_ref,
                       *, tq):
    step = pl.program_id(1)
    nsteps = pl.num_programs(1)
    flat = pl.program_id(0) * nsteps + step
    slot = lax.rem(flat, 2)
    nchunks = tq // CHUNK
    nblocks = tq // PROJ_ROWS
    col_starts = range(0, IN_COLS, PROJ_COLS)

    @pl.when(step == 0)
    def _():
        s_ref[...] = jnp.zeros_like(s_ref)
        kprev_ref[...] = jnp.zeros_like(kprev_ref)
        vprev_ref[...] = jnp.zeros_like(vprev_ref)


    def project_piece(dst, xrows, c0):
        c1 = min(c0 + PROJ_COLS, IN_COLS)
        dst[:, c0:c1] = _dot(xrows.astype(BF16), win_ref[:, c0:c1])

    @pl.when(flat == 0)
    def _():
        for c0 in col_starts:
            project_piece(pb_ref.at[0], x_ref[0, :PROJ_ROWS, :], c0)

    def proj(r0, nrows, c0, c1):
        if r0 < PROJ_ROWS:
            return pb_ref[slot, r0:r0 + nrows, c0:c1]
        return proj1_ref[r0 - PROJ_ROWS:r0 - PROJ_ROWS + nrows, c0:c1]

    pending = []

    def emit_pieces(n):
        for _ in range(min(n, len(pending))):
            pending.pop(0)()

    pieces_per_chunk = pl.cdiv(len(col_starts), PROJ_ROWS // CHUNK)

    lane = lax.broadcasted_iota(jnp.int32, (CHUNK, LANES), 1)
    lo = lane < HEAD_DIM
    lane_g = lax.broadcasted_iota(jnp.int32, (CHUNK, GROUP_W), 1)
    lo_g = jnp.bitwise_and(lane_g, LANES - 1) < HEAD_DIM
    blk = blk_ref[...]
    zero_b = jnp.zeros((CHUNK, LANES), BF16)

    def pair_diag(a):
        return jnp.concatenate([jnp.concatenate([a[:, :LANES], zero_b], axis=1),
                                jnp.concatenate([zero_b, a[:, LANES:]], axis=1)], axis=0)

    def head_stack(a):
        zero = jnp.zeros_like(a)
        return jnp.concatenate([jnp.where(lo_g, a, zero), jnp.where(lo_g, zero, a)], axis=0)

    lane_t = lax.broadcasted_iota(jnp.int32, (PROJ_ROWS, LANES), 1)
    lo_t = lane_t < HEAD_DIM

    def finish_block(kb):
        rb = slice(kb * PROJ_ROWS, (kb + 1) * PROJ_ROWS)
        for p in range(PAIRS):
            cols = slice(p * LANES, (p + 1) * LANES)
            gn = _pair_group_norm(oret_ref[rb, cols], lo_t, gnw_ref[:, cols])
            gate = proj(kb * PROJ_ROWS, PROJ_ROWS, _RG + p * LANES, _RG + (p + 1) * LANES)
            yret_ref[rb, cols] = (gate * jax.nn.sigmoid(gate) * gn).astype(BF16)
        mix = (_dot(yret_ref[rb, :], wout_ref[:RET_W, :])
               + _dot_tn(swat_ref[:, rb].astype(BF16), wout_ref[RET_W:, :]))
        y_ref[0, rb, :] = _layer_norm(DN_ALPHA * x_ref[0, rb, :] + mix, lng_ref[...], lnb_ref[...])

    k_prev = kprev_ref[...]
    vt_prev = vprev_ref[...]
    for c in range(nchunks):
        r0 = c * CHUNK
        rows = slice(r0, r0 + CHUNK)
        if r0 % PROJ_ROWS == 0:
            nb = r0 // PROJ_ROWS + 1
            if nb < nblocks:
                dst = proj1_ref.at[(nb - 1) * PROJ_ROWS:nb * PROJ_ROWS]
                xrows = x_ref[0, nb * PROJ_ROWS:(nb + 1) * PROJ_ROWS, :]
            else:
                dst, xrows = pb_ref.at[1 - slot], xn_ref[0]
            pending.extend(functools.partial(project_piece, dst, xrows, c0) for c0 in col_starts)
        budget = pieces_per_chunk
        for g in range(GROUPS):
            if budget > 0:
                emit_pieces(1)
                budget -= 1
            cols = slice(g * GROUP_W, (g + 1) * GROUP_W)
            q4 = proj(r0, CHUNK, _RQ + g * GROUP_W, _RQ + (g + 1) * GROUP_W)
            k4 = proj(r0, CHUNK, _RK + g * GROUP_W, _RK + (g + 1) * GROUP_W)
            v4 = proj(r0, CHUNK, _RV + g * GROUP_W, _RV + (g + 1) * GROUP_W).astype(BF16)
            k_bd = pair_diag((k4 * QK_SCALE).astype(BF16))
            sc = _dot_nt(head_stack(q4).astype(BF16), k_bd) * dm_ref[g]
            qd_st = head_stack(q4 * qw_ref[:, cols])
            lhs = jnp.concatenate([sc.astype(BF16), qd_st.astype(BF16)], axis=1)
            s_old = s_ref[g]
            rhs = jnp.concatenate([pair_diag(v4), s_old.astype(BF16)], axis=0)
            r = _dot(lhs, rhs)
            oret_ref[rows, cols] = jnp.where(lo_g, r[:CHUNK], r[CHUNK:])
            kd = (k4 * kw_ref[:, cols]).astype(BF16)
            s_ref[g] = gbd_ref[g] * s_old + blk * _dot_tn(kd, v4)
        k_cur = proj(r0, CHUNK, _SK, _SK + KV_W).astype(BF16)
        vt_cur = proj(r0, CHUNK, _SV, _SV + KV_W).T.astype(BF16)
        kk = jnp.concatenate([k_prev, k_cur], axis=0)
        vvt = jnp.concatenate([vt_prev, vt_cur], axis=1)
        q_parts = []
        for p in range(PAIRS):
            q2 = proj(r0, CHUNK, _SQ + p * LANES, _SQ + (p + 1) * LANES) * (QK_SCALE * LOG2E)
            zero = jnp.zeros_like(q2)
            q_parts += [jnp.where(lo, q2, zero).astype(BF16), jnp.where(lo, zero, q2).astype(BF16)]
        st = _dot_nt(kk, jnp.concatenate(q_parts, axis=0))
        if c == 0:
            st = st + bias_ref[jnp.minimum(step, 1)]
        else:
            st = st + bias_ref[1]
        sink = sink_ref[...] * LOG2E
        e_parts, den_parts = [], []
        for p in range(SWA_HEADS):
            if budget > 0 and p % 2 == 1:
                emit_pieces(1)
                budget -= 1
            pc = slice(p * CHUNK, (p + 1) * CHUNK)
            st_p, sink_p = st[:, pc], sink[:, pc]
            m = jnp.maximum(jnp.max(st_p, axis=0, keepdims=True), sink_p)
            e = jnp.exp2(st_p - m)
            den_parts.append(jnp.sum(e, axis=0, keepdims=True) + jnp.exp2(sink_p - m))
            e_parts.append(e.astype(BF16))
        if (r0 + CHUNK) % PROJ_ROWS == 0:
            emit_pieces(len(pending))
        den = jnp.concatenate(den_parts, axis=1)
        ot = _dot(vvt, jnp.concatenate(e_parts, axis=1)) * (1.0 / den)
        for p in range(PAIRS):
            c0 = 2 * p * CHUNK
            swat_ref[p * LANES:p * LANES + HEAD_DIM, rows] = ot[:HEAD_DIM, c0:c0 + CHUNK]
            swat_ref[p * LANES + HEAD_DIM:(p + 1) * LANES, rows] = ot[HEAD_DIM:, c0 + CHUNK:c0 + 2 * CHUNK]
        k_prev, vt_prev = k_cur, vt_cur
        if (r0 + CHUNK) % PROJ_ROWS == 0:
            finish_block(r0 // PROJ_ROWS)
    kprev_ref[...] = k_prev
    vprev_ref[...] = vt_prev

    @pl.when(step == nsteps - 1)
    def _():
        for h in range(RET_HEADS):
            g, i = divmod(h, 4)
            blk_h = s_ref[g, i * HEAD_DIM:(i + 1) * HEAD_DIM, (i // 2) * LANES:(i // 2 + 1) * LANES]
            if i % 2:
                blk_h = pltpu.roll(blk_h, HEAD_DIM, 1)
            st_ref[0, h] = blk_h[:, :HEAD_DIM]
        ck_ref[0] = proj(tq - WINDOW, WINDOW, _SK, _SK + KV_W).T
        cv_ref[0] = proj(tq - WINDOW, WINDOW, _SV, _SV + KV_W).T


def _mixer_prompt(x, w_in_p, w_out_p, lng, lnb, gnw, sink_st, consts, *, tq):
    bsz, seq, _ = x.shape
    assert seq % tq == 0 and tq % PROJ_ROWS == 0 and PROJ_ROWS % CHUNK == 0
    dm_st, qw, kw, gbd, blk, bias = consts
    body = functools.partial(_mixer_prompt_body, tq=tq)
    nsteps = seq // tq
    blocks_per_tile = tq // PROJ_ROWS

    def next_tile_head(b, s):
        nxt = jnp.minimum(b * nsteps + s + 1, bsz * nsteps - 1)
        return nxt // nsteps, (nxt % nsteps) * blocks_per_tile, 0

    return pl.pallas_call(
        body,
        grid=(bsz, nsteps),
        in_specs=[
            pl.BlockSpec((1, tq, D_MODEL), lambda b, s: (b, s, 0)),
            pl.BlockSpec((1, PROJ_ROWS, D_MODEL), next_tile_head),
            _const_spec((D_MODEL, IN_COLS)),
            _const_spec((2 * RET_W, D_MODEL)),
            _const_spec((1, D_MODEL)),
            _const_spec((1, D_MODEL)),
            _const_spec((1, RET_W)),
            _const_spec((1, SWA_HEADS * CHUNK)),
            _const_spec(dm_st.shape),
            _const_spec(qw.shape),
            _const_spec(kw.shape),
            _const_spec(gbd.shape),
            _const_spec(blk.shape),
            _const_spec(bias.shape),
        ],
        out_specs=[
            pl.BlockSpec((1, tq, D_MODEL), lambda b, s: (b, s, 0)),
            pl.BlockSpec((1, RET_HEADS, HEAD_DIM, HEAD_DIM), lambda b, s: (b, 0, 0, 0)),
            pl.BlockSpec((1, KV_W, WINDOW), lambda b, s: (b, 0, 0)),
            pl.BlockSpec((1, KV_W, WINDOW), lambda b, s: (b, 0, 0)),
        ],
        out_shape=[
            jax.ShapeDtypeStruct((bsz, seq, D_MODEL), F32),
            jax.ShapeDtypeStruct((bsz, RET_HEADS, HEAD_DIM, HEAD_DIM), F32),
            jax.ShapeDtypeStruct((bsz, KV_W, WINDOW), F32),
            jax.ShapeDtypeStruct((bsz, KV_W, WINDOW), F32),
        ],
        scratch_shapes=[
            pltpu.VMEM((2, PROJ_ROWS, IN_COLS), F32),
            pltpu.VMEM((tq - PROJ_ROWS, IN_COLS), F32),
            pltpu.VMEM((tq, RET_W), F32),
            pltpu.VMEM((tq, RET_W), BF16),
            pltpu.VMEM((SWA_W, tq), F32),
            pltpu.VMEM((GROUPS, GROUP_W, GROUP_W), F32),
            pltpu.VMEM((CHUNK, KV_W), BF16),
            pltpu.VMEM((KV_W, CHUNK), BF16),
        ],
        compiler_params=pltpu.CompilerParams(
            dimension_semantics=("arbitrary", "arbitrary"), vmem_limit_bytes=VMEM_LIMIT_BYTES),
        name="mixer_prompt",
    )(x, x, w_in_p, w_out_p, lng, lnb, gnw, sink_st, dm_st, qw, kw, gbd, blk, bias)


def _sample_constants(t):
    lg = _log_gamma()
    hh = np.arange(RET_HEADS)
    tt = np.arange(t, dtype=np.float64)
    row_h = np.repeat(hh, t)
    row_t = np.tile(tt, RET_HEADS)
    col_h = np.repeat(hh, HEAD_DIM)
    same = (row_h[:, None] == col_h[None, :]).astype(np.float64)
    mask_q1 = same * QK_SCALE
    mask_qw = same * np.exp((row_t + 1.0) * lg[row_h])[:, None]
    mask_kw = same * (np.exp((t - 1.0 - row_t) * lg[row_h]) * QK_SCALE)[:, None]
    diff = row_t[:, None] - row_t[None, :]
    same_h = row_h[:, None] == row_h[None, :]
    dm = np.where(same_h & (diff >= 0), np.exp(np.where(diff >= 0, diff, 0.0) * lg[row_h][:, None]), 0.0)
    g_t = np.repeat(np.exp(t * lg), HEAD_DIM)[:, None] * np.ones((1, HEAD_DIM))
    slopes = _alibi_slopes()
    j = np.arange(WINDOW + t)
    dist = row_t[:, None] + WINDOW - j[None, :]
    valid = (dist >= 0) & (dist <= WINDOW)
    bias = np.where(valid, -slopes[row_h][:, None] * dist, NEG_BIG)
    f = lambda a: jnp.asarray(a, F32)
    return f(mask_q1), f(mask_qw), f(mask_kw), f(dm), f(g_t), f(bias)


def _mixer_sample_body(x_ref, st_ref, ck_ref, cv_ref, win_ref, wout_ref,
                       lng_ref, lnb_ref, gnw_ref, sink_ref,
                       mq1_ref, mqw_ref, mkw_ref, dm_ref, gt_ref, bias_ref,
                       y_ref, sto_ref, cko_ref, cvo_ref,
                       proj_ref, vh_ref, qs_ref, oh_ref, os_ref, merged_ref, *, sb, t):
    ht = RET_HEADS * t
    nrows = sb * t
    x = x_ref[...]
    proj_ref[...] = _dot(x.astype(BF16), win_ref[...])
    lane = lax.broadcasted_iota(jnp.int32, (nrows, LANES), 1)
    lo = lane < HEAD_DIM
    for p in range(PAIRS):
        v2 = proj_ref[:, _RV + p * LANES:_RV + (p + 1) * LANES]
        vh_ref[2 * p] = v2[:, :HEAD_DIM]
        vh_ref[2 * p + 1] = pltpu.roll(v2, HEAD_DIM, 1)[:, :HEAD_DIM]
        q2 = proj_ref[:, _SQ + p * LANES:_SQ + (p + 1) * LANES] * QK_SCALE
        zero = jnp.zeros_like(q2)
        qs_ref[p] = jnp.where(lo, q2, zero)
        qs_ref[SWA_GROUP + p] = jnp.where(lo, zero, q2)
    oh_ref[...] = jnp.zeros_like(oh_ref)
    knt = proj_ref[:, _SK:_SK + KV_W].T
    vnt = proj_ref[:, _SV:_SV + KV_W].T
    keep = lax.broadcasted_iota(jnp.int32, (KV_W, WINDOW), 1) < WINDOW - t

    mq1 = mq1_ref[...]
    mqw = mqw_ref[...]
    mkw = mkw_ref[...]
    dm = dm_ref[...]
    g_t = gt_ref[...]
    bias = bias_ref[...]
    sink = sink_ref[...]

    def per_seq(b, carry):
        r0 = pl.multiple_of(b * t, t)
        rows = pl.ds(r0, t)
        q_b = proj_ref[rows, _RQ:_RQ + RET_W]
        k_b = proj_ref[rows, _RK:_RK + RET_W]
        q_rep = jnp.concatenate([q_b] * RET_HEADS, axis=0)
        k_rep = jnp.concatenate([k_b] * RET_HEADS, axis=0)
        v_st = vh_ref[:, rows, :].reshape(ht, HEAD_DIM).astype(BF16)
        s0 = st_ref[b]
        sc = _dot_nt((q_rep * mq1).astype(BF16), k_rep.astype(BF16)) * dm
        o = _dot((q_rep * mqw).astype(BF16), s0.astype(BF16)) + _dot(sc.astype(BF16), v_st)
        oh_ref[:, rows, 0:HEAD_DIM] = o.reshape(RET_HEADS, t, HEAD_DIM)
        sto_ref[b] = g_t * s0 + _dot_tn((k_rep * mkw).astype(BF16), v_st)
        k_new = proj_ref[rows, _SK:_SK + KV_W]
        v_new = proj_ref[rows, _SV:_SV + KV_W]
        kt_old = ck_ref[b]
        vt_old = cv_ref[b]
        q_s = qs_ref[:, rows, :].reshape(ht, LANES).astype(BF16)
        s = jnp.concatenate([_dot(q_s, kt_old.astype(BF16)),
                             _dot_nt(q_s, k_new.astype(BF16))], axis=1) + bias
        m = jnp.maximum(jnp.max(s, axis=-1, keepdims=True), sink)
        e = jnp.exp(s - m)
        den = jnp.sum(e, axis=-1, keepdims=True) + jnp.exp(sink - m)
        eb = e.astype(BF16)
        o_s = (_dot_nt(eb[:, :WINDOW], vt_old.astype(BF16))
               + _dot(eb[:, WINDOW:], v_new.astype(BF16))) / den
        os_ref[:, rows, :] = o_s.reshape(SWA_HEADS, t, LANES)
        shift = WINDOW - t - r0
        cko_ref[b] = jnp.where(keep, pltpu.roll(kt_old, WINDOW - t, 1), pltpu.roll(knt, shift, 1))
        cvo_ref[b] = jnp.where(keep, pltpu.roll(vt_old, WINDOW - t, 1), pltpu.roll(vnt, shift, 1))
        return carry

    lax.fori_loop(0, sb, per_seq, 0, unroll=16)

    for p in range(PAIRS):
        cols = slice(p * LANES, (p + 1) * LANES)
        o2 = oh_ref[2 * p] + pltpu.roll(oh_ref[2 * p + 1], HEAD_DIM, 1)
        gn = _pair_group_norm(o2, lo, gnw_ref[:, cols])
        gate = proj_ref[:, _RG + p * LANES:_RG + (p + 1) * LANES]
        merged_ref[:, cols] = (gate * jax.nn.sigmoid(gate) * gn).astype(BF16)
        merged_ref[:, RET_W + p * LANES:RET_W + (p + 1) * LANES] = jnp.where(
            lo, os_ref[p], os_ref[SWA_GROUP + p]).astype(BF16)
    mix = _dot(merged_ref[...], wout_ref[...])
    y_ref[...] = _layer_norm(DN_ALPHA * x + mix, lng_ref[...], lnb_ref[...])


def _mixer_sample(x, state, ck, cv, w_in_p, w_out_p, lng, lnb, gnw, sink_rows, consts, *, sb, t):
    rows_total = x.shape[0]
    nseq = rows_total // t
    assert nseq % sb == 0 and sb * t == LANES and t % 8 == 0
    rows = sb * t
    weights = (w_in_p, w_out_p)
    body = functools.partial(_mixer_sample_body, sb=sb, t=t)
    return pl.pallas_call(
        body,
        grid=(nseq // sb,),
        in_specs=[
            pl.BlockSpec((rows, D_MODEL), lambda i: (i, 0)),
            pl.BlockSpec((sb, RET_W, HEAD_DIM), lambda i: (i, 0, 0)),
            pl.BlockSpec((sb, KV_W, WINDOW), lambda i: (i, 0, 0)),
            pl.BlockSpec((sb, KV_W, WINDOW), lambda i: (i, 0, 0)),
        ] + [_const_spec(w.shape) for w in weights] + [
            _const_spec(lng.shape), _const_spec(lnb.shape), _const_spec(gnw.shape),
            _const_spec(sink_rows.shape),
        ] + [_const_spec(c.shape) for c in consts],
        out_specs=[
            pl.BlockSpec((rows, D_MODEL), lambda i: (i, 0)),
            pl.BlockSpec((sb, RET_W, HEAD_DIM), lambda i: (i, 0, 0)),
            pl.BlockSpec((sb, KV_W, WINDOW), lambda i: (i, 0, 0)),
            pl.BlockSpec((sb, KV_W, WINDOW), lambda i: (i, 0, 0)),
        ],
        out_shape=[
            jax.ShapeDtypeStruct((rows_total, D_MODEL), F32),
            jax.ShapeDtypeStruct((nseq, RET_W, HEAD_DIM), F32),
            jax.ShapeDtypeStruct((nseq, KV_W, WINDOW), F32),
            jax.ShapeDtypeStruct((nseq, KV_W, WINDOW), F32),
        ],
        scratch_shapes=[
            pltpu.VMEM((rows, IN_COLS), F32),
            pltpu.VMEM((RET_HEADS, rows, HEAD_DIM), F32),
            pltpu.VMEM((SWA_HEADS, rows, LANES), F32),
            pltpu.VMEM((RET_HEADS, rows, LANES), F32),
            pltpu.VMEM((SWA_HEADS, rows, LANES), F32),
            pltpu.VMEM((rows, 2 * RET_W), BF16),
        ],
        compiler_params=pltpu.CompilerParams(
            dimension_semantics=("arbitrary",), vmem_limit_bytes=VMEM_LIMIT_BYTES),
        name="mixer_sample",
    )(x, state, ck, cv, *weights, lng, lnb, gnw, sink_rows, *consts)


def _prep_prompt_weights(w_in, w_out, sinks):
    sq = w_in[:, _SQ:_SQ + SWA_W].reshape(D_MODEL, SWA_HEADS, HEAD_DIM)
    sq = sq[:, jnp.asarray(_SWA_HEAD_ORDER), :].reshape(D_MODEL, SWA_W)
    w_in_p = jnp.concatenate([w_in[:, :_SQ], sq, w_in[:, _SK:]], axis=1).astype(BF16)
    wo_s = w_out[RET_W:].reshape(SWA_HEADS, HEAD_DIM, D_MODEL)
    wo_s = wo_s[jnp.asarray(_SWA_HEAD_ORDER)].reshape(SWA_W, D_MODEL)
    w_out_p = jnp.concatenate([w_out[:RET_W], wo_s], axis=0).astype(BF16)
    sink_st = jnp.repeat(sinks.astype(F32)[jnp.asarray(_SWA_HEAD_ORDER)], CHUNK)[None, :]
    return w_in_p, w_out_p, sink_st


def kernel(x_prompt, x_sample, state_ret, cache_swa_k, cache_swa_v, ln_gain, ln_bias, w_in,
           ret_gn_w, swa_sinks, w_out, ffn1_gate, ffn1_up, ffn1_down, ffn2_gate, ffn2_up, ffn2_down):
    assert ln_gain.shape[0] == DEPTH == 1
    bsz, seq, _ = x_prompt.shape
    nseq, t, _ = x_sample.shape
    lng = ln_gain[0].astype(F32).reshape(3, 1, D_MODEL)
    lnb = ln_bias[0].astype(F32).reshape(3, 1, D_MODEL)
    f1 = (ffn1_gate[0].astype(BF16), ffn1_up[0].astype(BF16), ffn1_down[0].astype(BF16))
    gnw = ret_gn_w[0].astype(F32)

    to_dims_major = lambda c: jnp.transpose(c, (0, 2, 3, 1)).reshape(c.shape[0], KV_W, WINDOW)
    to_pos_major = lambda c: jnp.transpose(
        c.reshape(c.shape[0], SWA_KV_HEADS, HEAD_DIM, WINDOW), (0, 3, 1, 2))[None]

    tm_p = min(1024, bsz * seq)
    tm_s = min(512, nseq * t)
    xp = x_prompt.reshape(bsz * seq, D_MODEL)
    xs = x_sample.reshape(nseq * t, D_MODEL)

    xp, f2 = _ffn_ln(xp, *f1, lng[0], lnb[0], tm=tm_p,
                     cast_next=(ffn2_gate[0], ffn2_up[0], ffn2_down[0]))
    w_in_p, w_out_p, sink_st = _prep_prompt_weights(w_in[0], w_out[0], swa_sinks[0])
    yp, st_p, ck_p, cv_p = _mixer_prompt(
        xp.reshape(bsz, seq, D_MODEL), w_in_p, w_out_p, lng[1], lnb[1], gnw.reshape(1, RET_W),
        sink_st, _prompt_constants(), tq=min(1024, seq))
    yp = _ffn_ln(yp.reshape(bsz * seq, D_MODEL), *f2, lng[2], lnb[2], tm=tm_p)

    xs = _ffn_ln(xs, *f1, lng[0], lnb[0], tm=tm_s)
    sink_rows = jnp.repeat(swa_sinks[0].astype(F32), t)[:, None]
    ys, st_s, ck_s, cv_s = _mixer_sample(
        xs, state_ret[0].astype(F32).reshape(nseq, RET_W, HEAD_DIM),
        to_dims_major(cache_swa_k[0]), to_dims_major(cache_swa_v[0]),
        w_in_p, w_out_p, lng[1], lnb[1], gnw.reshape(1, RET_W), sink_rows, _sample_constants(t),
        sb=LANES // t, t=t)
    ys = _ffn_ln(ys, *f2, lng[2], lnb[2], tm=tm_s)

    return (yp.reshape(bsz, seq, D_MODEL), ys.reshape(nseq, t, D_MODEL),
            st_p[None], to_pos_major(ck_p), to_pos_major(cv_p),
            st_s.reshape(1, nseq, RET_HEADS, HEAD_DIM, HEAD_DIM),
            to_pos_major(ck_s), to_pos_major(cv_s))
```
